```python
import jax, jax.numpy as jnp
from jax import lax
import numpy as np

D_MODEL = 2048
BATCH = 8
SEQ = 4096
DEPTH = 2

HEAD_DIM = 64
ATTN_WIDTH = D_MODEL // 2
ATTN_HEADS = ATTN_WIDTH // HEAD_DIM
DILATED_BRANCHES = ((128, 1), (512, 4), (2048, 16))
ATTN_BLOCK = 128

SSD_WIDTH = D_MODEL // 2
SSD_HEAD_DIM = 64
SSD_HEADS = SSD_WIDTH // SSD_HEAD_DIM
SSD_GROUPS = 2
SSD_STATE = 128
SSD_CONV = 4
SSD_CHUNK = 128
CONV_CH = SSD_WIDTH + 2 * SSD_GROUPS * SSD_STATE

MIX_WIDTH = ATTN_WIDTH + SSD_WIDTH
IN_PROJ = 3 * ATTN_WIDTH + SSD_WIDTH + CONV_CH + SSD_HEADS
D_FF = 4 * D_MODEL
NORM_EPS = 1e-5

kernel_name = "hybrid_ssd_dilated_alibi_block"


def alibi_slopes(n_heads):
    return jnp.asarray(2.0 ** (-8.0 * (np.arange(n_heads) + 1) / n_heads), dtype=jnp.float32)


def rmsnorm(x, g):
    x32 = x.astype(jnp.float32)
    y = x32 * lax.rsqrt(jnp.mean(x32 * x32, axis=-1, keepdims=True) + NORM_EPS)
    return (y * g.astype(jnp.float32)).astype(x.dtype)


def dilated_branch(q, k, v, slopes, window, dilation):
    b, s, h, dh = q.shape
    L = s // dilation
    nb = -(-L // ATTN_BLOCK)
    Lp = nb * ATTN_BLOCK
    steps = window // dilation

    def to_blocks(t):
        t = t.astype(jnp.float32).reshape(b, L, dilation, h, dh).transpose(0, 2, 3, 1, 4)
        t = jnp.pad(t, ((0, 0), (0, 0), (0, 0), (0, Lp - L), (0, 0)))
        return t.reshape(b, dilation, h, nb, ATTN_BLOCK, dh)

    def with_prev(t):
        prev = jnp.pad(t, ((0, 0), (0, 0), (0, 0), (1, 0), (0, 0), (0, 0)))[:, :, :, :-1]
        return jnp.concatenate([prev, t], axis=4)

    qb, kb, vb = to_blocks(q), to_blocks(k), to_blocks(v)
    kc, vc = with_prev(kb), with_prev(vb)
    scores = jnp.einsum('brhnid,brhnjd->brhnij', qb, kc) * (dh ** -0.5)

    i = jnp.arange(ATTN_BLOCK)[:, None]
    j = jnp.arange(2 * ATTN_BLOCK)[None, :]
    delta = i - j + ATTN_BLOCK
    key_pos = jnp.arange(nb)[:, None, None] * ATTN_BLOCK - ATTN_BLOCK + j
    valid = (delta >= 0) & (delta <= steps) & (key_pos >= 0)
    bias = -slopes[:, None, None, None] * (delta * dilation).astype(jnp.float32)
    scores = jnp.where(valid, scores + bias, -jnp.inf)

    m = jnp.max(scores, axis=-1, keepdims=True)
    p = jnp.exp(scores - m)
    den = jnp.sum(p, axis=-1)
    out = jnp.einsum('brhnij,brhnjd->brhnid', p, vc) / den[..., None]
    lse = m[..., 0] + jnp.log(den)

    out = out.reshape(b, dilation, h, Lp, dh)[:, :, :, :L].transpose(0, 3, 1, 2, 4).reshape(b, s, h, dh)
    lse = lse.reshape(b, dilation, h, Lp)[:, :, :, :L].transpose(0, 3, 1, 2).reshape(b, s, h)
    return out, lse


def dilated_attention(q, k, v, slopes):
    outs, lses = [], []
    for window, dilation in DILATED_BRANCHES:
        o, l = dilated_branch(q, k, v, slopes, window, dilation)
        outs.append(o)
        lses.append(l)
    w = jax.nn.softmax(jnp.stack(lses, axis=0), axis=0)
    out = jnp.sum(w[..., None] * jnp.stack(outs, axis=0), axis=0)
    return out.astype(q.dtype)


def causal_depthwise_conv(u, w, bias):
    out = lax.conv_general_dilated(
        u, w[:, None, :], window_strides=(1,), padding=((SSD_CONV - 1, 0),),
        dimension_numbers=('NWC', 'WIO', 'NWC'), feature_group_count=u.shape[-1])
    return out + bias


def segsum_exp(a):
    cum = jnp.cumsum(a, axis=-1)
    diff = cum[..., :, None] - cum[..., None, :]
    T = a.shape[-1]
    mask = jnp.tril(jnp.ones((T, T), dtype=bool))
    return jnp.exp(jnp.where(mask, diff, -jnp.inf))


def ssd_scan(x, dt, a, b_in, c_in):
    bs, s, h, p = x.shape
    g, n, Q = SSD_GROUPS, SSD_STATE, SSD_CHUNK
    e = h // g
    nc = s // Q
    x = x.astype(jnp.float32)
    X = (x * dt[..., None]).reshape(bs, nc, Q, g, e, p)
    dA = (dt * a).reshape(bs, nc, Q, g, e).transpose(0, 3, 4, 1, 2)
    Bc = b_in.astype(jnp.float32).reshape(bs, nc, Q, g, n)
    Cc = c_in.astype(jnp.float32).reshape(bs, nc, Q, g, n)
    a_cum = jnp.cumsum(dA, axis=-1)

    Lmat = segsum_exp(dA)
    cb = jnp.einsum('bclgn,bcsgn->bgcls', Cc, Bc)
    y_diag = jnp.einsum('bgecls,bcsgep->bclgep', cb[:, :, None] * Lmat, X)

    decay_states = jnp.exp(a_cum[..., -1:] - a_cum)
    states = jnp.einsum('bcsgn,bgecs,bcsgep->bcgepn', Bc, decay_states, X)
    states = jnp.concatenate([jnp.zeros_like(states[:, :1]), states], axis=1)
    decay_chunk = segsum_exp(jnp.pad(a_cum[..., -1], ((0, 0), (0, 0), (0, 0), (1, 0))))
    states = jnp.einsum('bgezc,bcgepn->bzgepn', decay_chunk, states)[:, :-1]

    y_off = jnp.einsum('bclgn,bcgepn,bgecl->bclgep', Cc, states, jnp.exp(a_cum))
    return (y_diag + y_off).reshape(bs, s, h, p)


def hybrid_layer(x, ln1_g, w_in, conv_w, conv_b, dt_bias, a_log, d_skip,
                 attn_norm_g, ssd_norm_g, w_out, ln2_g, w_mlp_in, w_mlp_out, slopes):
    b, s, _ = x.shape
    h = rmsnorm(x, ln1_g)
    proj = h @ w_in
    cuts = [ATTN_WIDTH, 2 * ATTN_WIDTH, 3 * ATTN_WIDTH,
            3 * ATTN_WIDTH + SSD_WIDTH, 3 * ATTN_WIDTH + SSD_WIDTH + CONV_CH]
    q, k, v, z, xbc, dt_raw = jnp.split(proj, cuts, axis=-1)

    q = q.reshape(b, s, ATTN_HEADS, HEAD_DIM)
    k = k.reshape(b, s, ATTN_HEADS, HEAD_DIM)
    v = v.reshape(b, s, ATTN_HEADS, HEAD_DIM)
    attn = dilated_attention(q, k, v, slopes).reshape(b, s, ATTN_WIDTH)
    attn = rmsnorm(attn, attn_norm_g)

    xbc = jax.nn.silu(causal_depthwise_conv(xbc, conv_w, conv_b))
    xs, bm, cm = jnp.split(xbc, [SSD_WIDTH, SSD_WIDTH + SSD_GROUPS * SSD_STATE], axis=-1)
    xs = xs.reshape(b, s, SSD_HEADS, SSD_HEAD_DIM)
    dt = jax.nn.softplus(dt_raw.astype(jnp.float32) + dt_bias.astype(jnp.float32))
    a = -jnp.exp(a_log.astype(jnp.float32))
    y = ssd_scan(xs, dt, a, bm.reshape(b, s, SSD_GROUPS, SSD_STATE),
                 cm.reshape(b, s, SSD_GROUPS, SSD_STATE)).astype(x.dtype)
    y = y + d_skip[:, None] * xs
    y = y.reshape(b, s, SSD_WIDTH) * jax.nn.silu(z)
    y = rmsnorm(y.reshape(b, s, SSD_GROUPS, SSD_WIDTH // SSD_GROUPS),
                ssd_norm_g.reshape(SSD_GROUPS, SSD_WIDTH // SSD_GROUPS)).reshape(b, s, SSD_WIDTH)

    x = x + jnp.concatenate([attn, y], axis=-1) @ w_out

    h = rmsnorm(x, ln2_g)
    x = x + jnp.square(jax.nn.relu(h @ w_mlp_in)) @ w_mlp_out
    return x


def setup_inputs(seed: int = 0) -> dict:
    key = jax.random.key(seed)
    ks = jax.random.split(key, 16)
    f32 = jnp.float32
    dt0 = jnp.exp(jax.random.uniform(ks[5], (DEPTH, SSD_HEADS), f32,
                                     minval=float(np.log(1e-3)), maxval=float(np.log(1e-1))))
    return {
        "x": jax.random.normal(ks[0], (BATCH, SEQ, D_MODEL), f32),
        "ln1_g": 1.0 + 0.02 * jax.random.normal(ks[1], (DEPTH, D_MODEL), f32),
        "w_in": jax.random.normal(ks[2], (DEPTH, D_MODEL, IN_PROJ), f32) * D_MODEL ** -0.5,
        "conv_w": jax.random.normal(ks[3], (DEPTH, SSD_CONV, CONV_CH), f32) * SSD_CONV ** -0.5,
        "conv_b": 0.02 * jax.random.normal(ks[4], (DEPTH, CONV_CH), f32),
        "dt_bias": dt0 + jnp.log(-jnp.expm1(-dt0)),
        "a_log": jnp.log(jax.random.uniform(ks[6], (DEPTH, SSD_HEADS), f32, minval=1.0, maxval=16.0)),
        "d_skip": 1.0 + 0.1 * jax.random.normal(ks[7], (DEPTH, SSD_HEADS), f32),
        "attn_norm_g": 1.0 + 0.02 * jax.random.normal(ks[8], (DEPTH, ATTN_WIDTH), f32),
        "ssd_norm_g": 1.0 + 0.02 * jax.random.normal(ks[9], (DEPTH, SSD_WIDTH), f32),
        "w_out": jax.random.normal(ks[10], (DEPTH, MIX_WIDTH, D_MODEL), f32) * MIX_WIDTH ** -0.5,
        "ln2_g": 1.0 + 0.02 * jax.random.normal(ks[11], (DEPTH, D_MODEL), f32),
        "w_mlp_in": jax.random.normal(ks[12], (DEPTH, D_MODEL, D_FF), f32) * D_MODEL ** -0.5,
        "w_mlp_out": jax.random.normal(ks[13], (DEPTH, D_FF, D_MODEL), f32) * D_FF ** -0.5,
        "final_norm_g": 1.0 + 0.02 * jax.random.normal(ks[14], (D_MODEL,), f32),
    }


def reference(x, ln1_g, w_in, conv_w, conv_b, dt_bias, a_log, d_skip,
              attn_norm_g, ssd_norm_g, w_out, ln2_g, w_mlp_in, w_mlp_out, final_norm_g):
    slopes = alibi_slopes(ATTN_HEADS)
    for l in range(DEPTH):
        x = hybrid_layer(x, ln1_g[l], w_in[l], conv_w[l], conv_b[l], dt_bias[l], a_log[l],
                         d_skip[l], attn_norm_g[l], ssd_norm_g[l], w_out[l], ln2_g[l],
                         w_mlp_in[l], w_mlp_out[l], slopes)
    return rmsnorm(x, final_norm_g)
```

```python
import functools

import jax
import jax.numpy as jnp
import numpy as np
from jax import lax
from jax.experimental import pallas as pl
from jax.experimental.pallas import tpu as pltpu

D_MODEL = 2048
DEPTH = 2
HEAD_DIM = 64
ATTN_WIDTH = D_MODEL // 2
ATTN_HEADS = ATTN_WIDTH // HEAD_DIM
DILATED_BRANCHES = ((128, 1), (512, 4), (2048, 16))
ATTN_BLOCK = 128
SSD_WIDTH = D_MODEL // 2
SSD_HEAD_DIM = 64
SSD_HEADS = SSD_WIDTH // SSD_HEAD_DIM
SSD_GROUPS = 2
SSD_STATE = 128
SSD_CONV = 4
SSD_CHUNK = 128
BC_WIDTH = 2 * SSD_GROUPS * SSD_STATE
CONV_CH = SSD_WIDTH + BC_WIDTH
MAIN_WIDTH = 3 * ATTN_WIDTH + SSD_WIDTH + CONV_CH
D_FF = 4 * D_MODEL
NORM_EPS = 1e-5

LANES = 128
SUBLANES = 8
HEAD_PAIR = LANES // HEAD_DIM
NEG_BIG = -1e30
VMEM_LIMIT = 56 * 1024 * 1024

F32 = jnp.float32
BF16 = jnp.bfloat16


def _dot(a, b):
    return jnp.dot(a, b, preferred_element_type=F32)


def _dot_nt(a, b):
    return lax.dot_general(a, b, (((1,), (1,)), ((), ())), preferred_element_type=F32)


def _rms_scale(x):
    return lax.rsqrt(jnp.mean(x * x, axis=-1, keepdims=True) + NORM_EPS)


def _silu(x):
    return x * (1.0 / (1.0 + jnp.exp(-x)))


def _params(semantics):
    return pltpu.CompilerParams(dimension_semantics=semantics, vmem_limit_bytes=VMEM_LIMIT)


def _in_proj_kernel(x_ref, g_ref, w_ref, wdt_ref, main_ref, dt_ref, hn_ref):
    @pl.when(pl.program_id(1) == 0)
    def _():
        x = x_ref[...]
        hn = (x * _rms_scale(x) * g_ref[...]).astype(BF16)
        hn_ref[...] = hn
        dt_ref[...] = _dot(hn, wdt_ref[...])

    main_ref[...] = _dot(hn_ref[...], w_ref[...]).astype(main_ref.dtype)


def _in_proj(x, g, w_main, w_dt, *, tm, tn):
    m, d = x.shape
    n = w_main.shape[1]
    return pl.pallas_call(
        _in_proj_kernel,
        grid=(m // tm, n // tn),
        in_specs=[
            pl.BlockSpec((tm, d), lambda i, j: (i, 0)),
            pl.BlockSpec((1, d), lambda i, j: (0, 0)),
            pl.BlockSpec((d, tn), lambda i, j: (0, j)),
            pl.BlockSpec((d, LANES), lambda i, j: (0, 0)),
        ],
        out_specs=[
            pl.BlockSpec((tm, tn), lambda i, j: (i, j)),
            pl.BlockSpec((tm, LANES), lambda i, j: (i, 0)),
        ],
        out_shape=[
            jax.ShapeDtypeStruct((m, n), BF16),
            jax.ShapeDtypeStruct((m, LANES), F32),
        ],
        scratch_shapes=[pltpu.VMEM((tm, d), BF16)],
        compiler_params=_params(("parallel", "arbitrary")),
        name="in_proj",
    )(x, g, w_main, w_dt)


def _attn_kernel(slopes_ref, q_ref, k_ref, v_ref, o_ref,
                 qf_ref, kf_ref, vf_ref, m_ref, l_ref, acc_ref, bias_ref):
    seq = q_ref.shape[0]
    blk = ATTN_BLOCK
    hp = pl.program_id(1)

    qf_ref[...] = q_ref[...].astype(F32) * (HEAD_DIM ** -0.5)
    kf_ref[...] = k_ref[...].astype(F32)
    vf_ref[...] = v_ref[...].astype(F32)
    m_ref[...] = jnp.full(m_ref.shape, NEG_BIG, F32)
    l_ref[...] = jnp.zeros(l_ref.shape, F32)
    acc_ref[...] = jnp.zeros(acc_ref.shape, F32)

    low = lax.broadcasted_iota(jnp.int32, (blk, LANES), 1) < HEAD_DIM

    row = lax.broadcasted_iota(jnp.int32, (2 * blk, 2 * blk), 0)
    col = lax.broadcasted_iota(jnp.int32, (2 * blk, 2 * blk), 1)
    delta = (row & (blk - 1)) - col + blk
    slope = jnp.where(row < blk, slopes_ref[HEAD_PAIR * hp], slopes_ref[HEAD_PAIR * hp + 1])
    for t, (window, dil) in enumerate(DILATED_BRANCHES):
        valid = (delta >= 0) & (delta <= window // dil)
        bias = jnp.where(valid, -slope * (delta * dil).astype(F32), NEG_BIG)
        bias_ref[2 * t] = bias
        bias_ref[2 * t + 1] = jnp.where(col >= blk, bias, NEG_BIG)

    def pair_rows(x):
        return jnp.where(low, jnp.broadcast_to(x[:blk], (blk, LANES)),
                         jnp.broadcast_to(x[blk:], (blk, LANES)))

    def stack_heads(x):
        return jnp.concatenate([x[:, 0:1], x[:, HEAD_DIM:HEAD_DIM + 1]], axis=0)

    for t, (window, dil) in enumerate(DILATED_BRANCHES):
        nb = seq // dil // blk
        assert nb * dil * blk == seq and nb & (nb - 1) == 0

        def rows_at(start, dil=dil):
            if dil == 1:
                return pl.ds(pl.multiple_of(start, blk), blk)
            return pl.ds(start, blk, stride=dil)

        def body(idx, carry, t=t, dil=dil, nb=nb, rows_at=rows_at):
            r = idx >> (nb.bit_length() - 1)
            n = idx & (nb - 1)
            first = n == 0
            cur = rows_at(r + n * (blk * dil))
            prev = rows_at(r + jnp.maximum(n - 1, 0) * (blk * dil))

            q = qf_ref[cur, :]
            q2 = jnp.concatenate([jnp.where(low, q, 0.0), jnp.where(low, 0.0, q)], axis=0).astype(BF16)
            kk = jnp.concatenate([kf_ref[prev, :], kf_ref[cur, :]], axis=0).astype(BF16)
            vv = jnp.concatenate([vf_ref[prev, :], vf_ref[cur, :]], axis=0).astype(BF16)

            s = _dot_nt(q2, kk) + bias_ref[2 * t + first.astype(jnp.int32)]
            m_old = stack_heads(m_ref[cur, :])
            l_old = stack_heads(l_ref[cur, :])
            m_new = jnp.maximum(m_old, jnp.max(s, axis=-1, keepdims=True))
            alpha = jnp.exp(m_old - m_new)
            p = jnp.exp(s - m_new)
            l_new = alpha * l_old + jnp.sum(p, axis=-1, keepdims=True)
            pv = _dot(p.astype(BF16), vv)

            acc_ref[cur, :] = pair_rows(alpha) * acc_ref[cur, :] + jnp.where(low, pv[:blk], pv[blk:])
            m_ref[cur, :] = pair_rows(m_new)
            l_ref[cur, :] = pair_rows(l_new)
            return carry

        lax.fori_loop(0, dil * nb, body, 0)

    o_ref[...] = (acc_ref[...] / l_ref[...]).astype(o_ref.dtype)


def _attention(slopes, main3):
    b, seq, _ = main3.shape
    n_pairs = ATTN_WIDTH // LANES
    blk = ATTN_BLOCK

    def spec(offset):
        return pl.BlockSpec((None, seq, LANES), lambda i, j: (i, 0, offset + j))

    return pl.pallas_call(
        _attn_kernel,
        grid=(b, n_pairs),
        in_specs=[
            pl.BlockSpec(memory_space=pltpu.SMEM),
            spec(0), spec(n_pairs), spec(2 * n_pairs),
        ],
        out_specs=pl.BlockSpec((None, seq, LANES), lambda i, j: (i, 0, j)),
        out_shape=jax.ShapeDtypeStruct((b, seq, ATTN_WIDTH), BF16),
        scratch_shapes=[pltpu.VMEM((seq, LANES), F32)] * 6
        + [pltpu.VMEM((2 * len(DILATED_BRANCHES), 2 * blk, 2 * blk), F32)],
        compiler_params=_params(("parallel", "parallel")),
        name="attention",
    )(slopes, main3, main3, main3)


def _ssd_kernel(z_ref, xs_ref, bc_ref, dt_ref, cw_ref, cb_ref, dtb_ref, alog_ref, dskip_ref, g_ref,
                y_ref, ext_ref, state_ref):
    tb = xs_ref.shape[0]
    q = SSD_CHUNK
    halo = SUBLANES
    gw = SSD_WIDTH // SSD_GROUPS
    pairs_per_group = gw // LANES

    @pl.when(pl.program_id(1) == 0)
    def _():
        ext_ref[0:halo, :] = jnp.zeros((halo, CONV_CH), F32)
        state_ref[...] = jnp.zeros(state_ref.shape, F32)

    ext_ref[halo:halo + tb, 0:SSD_WIDTH] = xs_ref[...].astype(F32)
    ext_ref[halo:halo + tb, SSD_WIDTH:CONV_CH] = bc_ref[...].astype(F32)

    low = lax.broadcasted_iota(jnp.int32, (q, LANES), 1) < SSD_HEAD_DIM
    ri = lax.broadcasted_iota(jnp.int32, (q, q), 0)
    ci = lax.broadcasted_iota(jnp.int32, (q, q), 1)
    tri = ri >= ci
    tri_bf = jnp.where(tri, 1.0, 0.0).astype(BF16)
    a_neg = -jnp.exp(alog_ref[...])

    def cumsum_rows(v):
        hi = v.astype(BF16)
        r1 = v - hi.astype(F32)
        mid = r1.astype(BF16)
        lo = (r1 - mid.astype(F32)).astype(BF16)
        return _dot(tri_bf, hi) + _dot(tri_bf, mid) + _dot(tri_bf, lo)

    def lane_bcast(v, h):
        return jnp.broadcast_to(v[:, h:h + 1], (q, LANES))

    def chunk(c):
        r0 = c * q
        u = cb_ref[...]
        for k in range(SSD_CONV):
            u = u + cw_ref[k:k + 1, :] * ext_ref[pl.ds(r0 + halo - (SSD_CONV - 1) + k, q), :]
        u = _silu(u)
        x = u[:, :SSD_WIDTH]
        dt_in = dt_ref[pl.ds(r0, q), :] + dtb_ref[...]
        dt = jnp.maximum(dt_in, 0.0) + jnp.log1p(jnp.exp(-jnp.abs(dt_in)))
        acum = cumsum_rows(dt * a_neg)
        acum_t = acum.T
        dt_t = dt.T

        y_parts = []
        for g in range(SSD_GROUPS):
            bg = u[:, SSD_WIDTH + g * SSD_STATE:SSD_WIDTH + (g + 1) * SSD_STATE]
            cg = u[:, SSD_WIDTH + (SSD_GROUPS + g) * SSD_STATE:SSD_WIDTH + (SSD_GROUPS + g + 1) * SSD_STATE]
            cg_bf = cg.astype(BF16)
            cb = _dot_nt(cg_bf, bg.astype(BF16))
            st = state_ref[g]
            y_off = _dot(cg_bf, st.astype(BF16))
            xd_parts, decay_parts = [], []
            for pp in range(pairs_per_group):
                p = g * pairs_per_group + pp
                h0 = HEAD_PAIR * p
                xp = x[:, p * LANES:(p + 1) * LANES]
                cols = [lane_bcast(acum, h0 + e) for e in range(HEAD_PAIR)]
                colc = jnp.where(low, cols[0], cols[1])
                ms = []
                for e in range(HEAD_PAIR):
                    h = h0 + e
                    seg = jnp.exp(jnp.where(tri, cols[e] - acum_t[h:h + 1, :], NEG_BIG))
                    ms.append(cb * seg * dt_t[h:h + 1, :])
                m_cat = jnp.concatenate(ms, axis=1).astype(BF16)
                x_blk = jnp.concatenate([jnp.where(low, xp, 0.0), jnp.where(low, 0.0, xp)],
                                        axis=0).astype(BF16)
                y_diag = _dot(m_cat, x_blk)
                y_o = y_off[:, pp * LANES:(pp + 1) * LANES] * jnp.exp(colc)
                y_parts.append(y_diag + y_o + dskip_ref[:, p * LANES:(p + 1) * LANES] * xp)
                last = colc[q - 1:q, :]
                dtc = jnp.where(low, lane_bcast(dt, h0), lane_bcast(dt, h0 + 1))
                xd_parts.append(xp * (dtc * jnp.exp(last - colc)))
                decay_parts.append(jnp.exp(last))
            xd = jnp.concatenate(xd_parts, axis=1).astype(BF16)
            decay = jnp.concatenate(decay_parts, axis=1)
            state_ref[g] = st * decay + _dot(bg.T.astype(BF16), xd)

        z = z_ref[pl.ds(r0, q), :].astype(F32)
        outs = []
        for g in range(SSD_GROUPS):
            yg = jnp.concatenate(y_parts[g * pairs_per_group:(g + 1) * pairs_per_group], axis=1)
            yg = yg * _silu(z[:, g * gw:(g + 1) * gw])
            outs.append(yg * _rms_scale(yg) * g_ref[:, g * gw:(g + 1) * gw])
        y_ref[pl.ds(r0, q), :] = jnp.concatenate(outs, axis=1).astype(y_ref.dtype)

    for c in range(tb // q):
        chunk(c)
    ext_ref[0:halo, :] = ext_ref[tb:tb + halo, :]


def _ssd(main3, dt3, conv_w, conv_b, dt_bias, a_log, d_skip, norm_g, *, tb):
    b, seq, _ = main3.shape
    z_blk = (3 * ATTN_WIDTH) // SSD_WIDTH
    xs_blk = (3 * ATTN_WIDTH + SSD_WIDTH) // SSD_WIDTH
    bc_blk = (3 * ATTN_WIDTH + 2 * SSD_WIDTH) // BC_WIDTH

    def const(shape):
        return pl.BlockSpec(shape, lambda i, j: (0,) * len(shape))

    return pl.pallas_call(
        _ssd_kernel,
        grid=(b, seq // tb),
        in_specs=[
            pl.BlockSpec((None, tb, SSD_WIDTH), lambda i, j: (i, j, z_blk)),
            pl.BlockSpec((None, tb, SSD_WIDTH), lambda i, j: (i, j, xs_blk)),
            pl.BlockSpec((None, tb, BC_WIDTH), lambda i, j: (i, j, bc_blk)),
            pl.BlockSpec((None, tb, LANES), lambda i, j: (i, j, 0)),
            const((SSD_CONV, CONV_CH)), const((1, CONV_CH)),
            const((1, LANES)), const((1, LANES)),
            const((1, SSD_WIDTH)), const((1, SSD_WIDTH)),
        ],
        out_specs=pl.BlockSpec((None, tb, SSD_WIDTH), lambda i, j: (i, j, 0)),
        out_shape=jax.ShapeDtypeStruct((b, seq, SSD_WIDTH), BF16),
        scratch_shapes=[
            pltpu.VMEM((tb + SUBLANES, CONV_CH), F32),
            pltpu.VMEM((SSD_GROUPS, SSD_STATE, SSD_WIDTH // SSD_GROUPS), F32),
        ],
        compiler_params=_params(("parallel", "arbitrary")),
        name="ssd",
    )(main3, main3, main3, dt3, conv_w, conv_b, dt_bias, a_log, d_skip, norm_g)


def _out_proj_kernel(a_ref, y_ref, x_ref, g_ref, w_ref, o_ref):
    a = a_ref[...].astype(F32)
    an = (a * _rms_scale(a) * g_ref[...]).astype(BF16)
    mixed = _dot(an, w_ref[0:ATTN_WIDTH, :]) + _dot(y_ref[...], w_ref[ATTN_WIDTH:, :])
    o_ref[...] = x_ref[...] + mixed


def _out_proj(attn, y, x, g, w, *, tm):
    m, d = x.shape
    return pl.pallas_call(
        _out_proj_kernel,
        grid=(m // tm,),
        in_specs=[
            pl.BlockSpec((tm, ATTN_WIDTH), lambda i: (i, 0)),
            pl.BlockSpec((tm, SSD_WIDTH), lambda i: (i, 0)),
            pl.BlockSpec((tm, d), lambda i: (i, 0)),
            pl.BlockSpec((1, ATTN_WIDTH), lambda i: (0, 0)),
            pl.BlockSpec(w.shape, lambda i: (0, 0)),
        ],
        out_specs=pl.BlockSpec((tm, d), lambda i: (i, 0)),
        out_shape=jax.ShapeDtypeStruct((m, d), F32),
        compiler_params=_params(("parallel",)),
        name="out_proj",
    )(attn, y, x, g, w)


def _mlp_kernel(x_ref, g_ref, w1_ref, w2_ref, gf_ref, o_ref, hn_ref, *, final_norm):
    f = pl.program_id(1)

    @pl.when(f == 0)
    def _():
        x = x_ref[...]
        hn_ref[...] = (x * _rms_scale(x) * g_ref[...]).astype(BF16)
        o_ref[...] = x

    h = jnp.maximum(_dot(hn_ref[...], w1_ref[...]), 0.0)
    o_ref[...] += _dot((h * h).astype(BF16), w2_ref[...])

    if final_norm:
        @pl.when(f == pl.num_programs(1) - 1)
        def _():
            o = o_ref[...]
            o_ref[...] = o * _rms_scale(o) * gf_ref[...]


def _mlp(x, g, w1, w2, gf, *, tm, tf, final_norm):
    m, d = x.shape
    dff = w1.shape[1]
    return pl.pallas_call(
        functools.partial(_mlp_kernel, final_norm=final_norm),
        grid=(m // tm, dff // tf),
        in_specs=[
            pl.BlockSpec((tm, d), lambda i, j: (i, 0)),
            pl.BlockSpec((1, d), lambda i, j: (0, 0)),
            pl.BlockSpec((d, tf), lambda i, j: (0, j)),
            pl.BlockSpec((tf, d), lambda i, j: (j, 0)),
            pl.BlockSpec((1, d), lambda i, j: (0, 0)),
        ],
        out_specs=pl.BlockSpec((tm, d), lambda i, j: (i, 0)),
        out_shape=jax.ShapeDtypeStruct((m, d), F32),
        scratch_shapes=[pltpu.VMEM((tm, d), BF16)],
        compiler_params=_params(("parallel", "arbitrary")),
        name="mlp",
    )(x, g, w1, w2, gf)


def _alibi_slopes():
    return jnp.asarray(2.0 ** (-8.0 * (np.arange(ATTN_HEADS) + 1) / ATTN_HEADS), dtype=F32)


def _pad_lanes(v):
    return jnp.pad(v, ((0, 0), (0, LANES - v.shape[1])))


def kernel(x, ln1_g, w_in, conv_w, conv_b, dt_bias, a_log, d_skip, attn_norm_g, ssd_norm_g, w_out,
           ln2_g, w_mlp_in, w_mlp_out, final_norm_g):
    b, seq, d = x.shape
    m = b * seq
    slopes = _alibi_slopes()
    xf = x.reshape(m, d)
    for l in range(DEPTH):
        w_main = w_in[l, :, :MAIN_WIDTH].astype(BF16)
        w_dt = _pad_lanes(w_in[l, :, MAIN_WIDTH:]).astype(BF16)
        main, dt_raw = _in_proj(xf, ln1_g[l][None, :], w_main, w_dt, tm=1024, tn=512)
        main3 = main.reshape(b, seq, MAIN_WIDTH)
        attn = _attention(slopes, main3)
        y = _ssd(
            main3, dt_raw.reshape(b, seq, LANES), conv_w[l], conv_b[l][None, :],
            _pad_lanes(dt_bias[l][None, :]), _pad_lanes(a_log[l][None, :]),
            jnp.repeat(d_skip[l], SSD_HEAD_DIM)[None, :], ssd_norm_g[l][None, :], tb=256)
        x1 = _out_proj(attn.reshape(m, ATTN_WIDTH), y.reshape(m, SSD_WIDTH), xf,
                       attn_norm_g[l][None, :], w_out[l].astype(BF16), tm=512)
        xf = _mlp(x1, ln2_g[l][None, :], w_mlp_in[l].astype(BF16), w_mlp_out[l].astype(BF16),
                  final_norm_g[None, :], tm=1024, tf=512, final_norm=(l == DEPTH - 1))
    return xf.reshape(b, seq, d)
```

```python
import functools

import jax
import jax.numpy as jnp
import numpy as np
from jax import lax
from jax.experimental import pallas as pl
from jax.experimental.pallas import tpu as pltpu

D_MODEL = 2048
DEPTH = 2
HEAD_DIM = 64
ATTN_WIDTH = D_MODEL // 2
ATTN_HEADS = ATTN_WIDTH // HEAD_DIM
DILATED_BRANCHES = ((128, 1), (512, 4), (2048, 16))
ATTN_BLOCK = 128
SSD_WIDTH = D_MODEL // 2
SSD_HEAD_DIM = 64
SSD_HEADS = SSD_WIDTH // SSD_HEAD_DIM
SSD_GROUPS = 2
SSD_STATE = 128
SSD_CONV = 4
SSD_CHUNK = 128
BC_WIDTH = 2 * SSD_GROUPS * SSD_STATE
CONV_CH = SSD_WIDTH + BC_WIDTH
MAIN_WIDTH = 3 * ATTN_WIDTH + SSD_WIDTH + CONV_CH
D_FF = 4 * D_MODEL
NORM_EPS = 1e-5

LANES = 128
SUBLANES = 8
HEAD_PAIR = LANES // HEAD_DIM
NEG_BIG = -1e30
VMEM_LIMIT = 56 * 1024 * 1024

F32 = jnp.float32
BF16 = jnp.bfloat16


def _dot(a, b):
    return jnp.dot(a, b, preferred_element_type=F32)


def _dot_nt(a, b):
    return lax.dot_general(a, b, (((1,), (1,)), ((), ())), preferred_element_type=F32)


def _rms_scale(x):
    return lax.rsqrt(jnp.mean(x * x, axis=-1, keepdims=True) + NORM_EPS)


def _silu(x):
    return x * (1.0 / (1.0 + jnp.exp(-x)))


def _params(semantics):
    return pltpu.CompilerParams(dimension_semantics=semantics, vmem_limit_bytes=VMEM_LIMIT)


def _in_proj_kernel(x_ref, g_ref, w_ref, wdt_ref, main_ref, dt_ref, hn_ref):
    @pl.when(pl.program_id(1) == 0)
    def _():
        x = x_ref[...]
        hn = (x * _rms_scale(x) * g_ref[...]).astype(BF16)
        hn_ref[...] = hn
        dt_ref[...] = _dot(hn, wdt_ref[...])

    main_ref[...] = _dot(hn_ref[...], w_ref[...]).astype(main_ref.dtype)


def _in_proj(x, g, w_main, w_dt, *, tm, tn):
    m, d = x.shape
    n = w_main.shape[1]
    return pl.pallas_call(
        _in_proj_kernel,
        grid=(m // tm, n // tn),
        in_specs=[
            pl.BlockSpec((tm, d), lambda i, j: (i, 0)),
            pl.BlockSpec((1, d), lambda i, j: (0, 0)),
            pl.BlockSpec((d, tn), lambda i, j: (0, j)),
            pl.BlockSpec((d, LANES), lambda i, j: (0, 0)),
        ],
        out_specs=[
            pl.BlockSpec((tm, tn), lambda i, j: (i, j)),
            pl.BlockSpec((tm, LANES), lambda i, j: (i, 0)),
        ],
        out_shape=[
            jax.ShapeDtypeStruct((m, n), BF16),
            jax.ShapeDtypeStruct((m, LANES), F32),
        ],
        scratch_shapes=[pltpu.VMEM((tm, d), BF16)],
        compiler_params=_params(("parallel", "arbitrary")),
        name="in_proj",
    )(x, g, w_main, w_dt)


def _attn_kernel(slopes_ref, q_ref, k_ref, v_ref, o_ref,
                 qf_ref, kf_ref, vf_ref, qd_ref, kd_ref, vd_ref, m_ref, l_ref, acc_ref, bias_ref):
    seq = q_ref.shape[0]
    blk = ATTN_BLOCK
    hp = pl.program_id(1)

    qf_ref[...] = q_ref[...].astype(F32)
    kf_ref[...] = k_ref[...].astype(F32)
    vf_ref[...] = v_ref[...].astype(F32)

    low = lax.broadcasted_iota(jnp.int32, (blk, LANES), 1) < HEAD_DIM
    ones_bf = jnp.ones((2 * blk, LANES), BF16)

    row = lax.broadcasted_iota(jnp.int32, (2 * blk, 2 * blk), 0)
    col = lax.broadcasted_iota(jnp.int32, (2 * blk, 2 * blk), 1)
    delta = (row & (blk - 1)) - col + blk
    slope = jnp.where(row < blk, slopes_ref[HEAD_PAIR * hp], slopes_ref[HEAD_PAIR * hp + 1])
    for t, (window, dil) in enumerate(DILATED_BRANCHES):
        valid = (delta >= 0) & (delta <= window // dil)
        bias = jnp.where(valid, -slope * (delta * dil).astype(F32), NEG_BIG)
        bias_ref[2 * t] = bias
        bias_ref[2 * t + 1] = jnp.where(col >= blk, bias, NEG_BIG)

    def pair_rows(x):
        return jnp.where(low, jnp.broadcast_to(x[:blk], (blk, LANES)),
                         jnp.broadcast_to(x[blk:], (blk, LANES)))

    def pair_halves(x):
        return jnp.where(low, x[:blk], x[blk:])

    scale = jnp.asarray(HEAD_DIM ** -0.5, BF16)
    zero_bf = jnp.zeros((), BF16)
    n_blocks = seq // blk

    def block_rows(i):
        return pl.ds(pl.multiple_of(i * blk, blk), blk)

    for t, (window, dil) in enumerate(DILATED_BRANCHES):
        nb = n_blocks // dil
        assert nb * dil * blk == seq and nb & (nb - 1) == 0

        def token_rows(idx, dil=dil, nb=nb):
            if dil == 1:
                return block_rows(idx)
            r = idx >> (nb.bit_length() - 1)
            n = idx & (nb - 1)
            return pl.ds(r + n * (blk * dil), blk, stride=dil)

        if dil == 1:
            q_src, k_src, v_src = q_ref, k_ref, v_ref
        else:
            q_src, k_src, v_src = qd_ref, kd_ref, vd_ref

            def gather(idx, carry, token_rows=token_rows):
                rows = token_rows(idx)
                qd_ref[block_rows(idx), :] = qf_ref[rows, :].astype(BF16)
                kd_ref[block_rows(idx), :] = kf_ref[rows, :].astype(BF16)
                vd_ref[block_rows(idx), :] = vf_ref[rows, :].astype(BF16)
                return carry

            lax.fori_loop(0, n_blocks, gather, 0, unroll=4)

        def body(idx, carry, t=t, nb=nb, token_rows=token_rows, q_src=q_src, k_src=k_src, v_src=v_src):
            first = (idx & (nb - 1)) == 0
            cur = block_rows(idx)
            prev = block_rows(jnp.where(first, idx, idx - 1))

            q = q_src[cur, :] * scale
            q2 = jnp.concatenate([jnp.where(low, q, zero_bf), jnp.where(low, zero_bf, q)], axis=0)
            kk = jnp.concatenate([k_src[prev, :], k_src[cur, :]], axis=0)
            vv = jnp.concatenate([jnp.concatenate([v_src[prev, :], v_src[cur, :]], axis=0), ones_bf], axis=1)

            s = _dot_nt(q2, kk) + bias_ref[2 * t + first.astype(jnp.int32)]
            m = jnp.max(s, axis=-1, keepdims=True)
            p = jnp.exp(s - m)
            pv = _dot(p.astype(BF16), vv)

            out_rows = token_rows(idx)
            acc_ref[t, out_rows, :] = pair_halves(pv[:, :LANES])
            l_ref[t, out_rows, :] = pair_halves(pv[:, LANES:])
            m_ref[t, out_rows, :] = pair_rows(m)
            return carry

        lax.fori_loop(0, n_blocks, body, 0, unroll=8)

    def merge(i, carry):
        rows = pl.ds(pl.multiple_of(i * (2 * blk), 2 * blk), 2 * blk)
        ms = [m_ref[t, rows, :] for t in range(len(DILATED_BRANCHES))]
        m_all = functools.reduce(jnp.maximum, ms)
        ws = [jnp.exp(m_b - m_all) for m_b in ms]
        num = sum(w * acc_ref[t, rows, :] for t, w in enumerate(ws))
        den = sum(w * l_ref[t, rows, :] for t, w in enumerate(ws))
        o_ref[rows, :] = (num / den).astype(o_ref.dtype)
        return carry

    lax.fori_loop(0, seq // (2 * blk), merge, 0)


def _attention(slopes, main3):
    b, seq, _ = main3.shape
    n_pairs = ATTN_WIDTH // LANES
    blk = ATTN_BLOCK

    def spec(offset):
        return pl.BlockSpec((None, seq, LANES), lambda i, j: (i, 0, offset + j))

    return pl.pallas_call(
        _attn_kernel,
        grid=(b, n_pairs),
        in_specs=[
            pl.BlockSpec(memory_space=pltpu.SMEM),
            spec(0), spec(n_pairs), spec(2 * n_pairs),
        ],
        out_specs=pl.BlockSpec((None, seq, LANES), lambda i, j: (i, 0, j)),
        out_shape=jax.ShapeDtypeStruct((b, seq, ATTN_WIDTH), BF16),
        scratch_shapes=[pltpu.VMEM((seq, LANES), F32)] * 3
        + [pltpu.VMEM((seq, LANES), BF16)] * 3
        + [pltpu.VMEM((len(DILATED_BRANCHES), seq, LANES), F32)] * 3
        + [pltpu.VMEM((2 * len(DILATED_BRANCHES), 2 * blk, 2 * blk), F32)],
        compiler_params=_params(("parallel", "parallel")),
        name="attention",
    )(slopes, main3, main3, main3)


def _ssd_kernel(z_ref, xs_ref, bc_ref, dt_ref, cw_ref, cb_ref, dtb_ref, alog_ref, dskip_ref, g_ref,
                y_ref, ext_ref, state_ref):
    tb = xs_ref.shape[0]
    q = SSD_CHUNK
    halo = SUBLANES
    gw = SSD_WIDTH // SSD_GROUPS
    pairs_per_group = gw // LANES

    @pl.when(pl.program_id(1) == 0)
    def _():
        ext_ref[0:halo, :] = jnp.zeros((halo, CONV_CH), F32)
        state_ref[...] = jnp.zeros(state_ref.shape, F32)

    ext_ref[halo:halo + tb, 0:SSD_WIDTH] = xs_ref[...].astype(F32)
    ext_ref[halo:halo + tb, SSD_WIDTH:CONV_CH] = bc_ref[...].astype(F32)

    low = lax.broadcasted_iota(jnp.int32, (q, LANES), 1) < SSD_HEAD_DIM
    ri = lax.broadcasted_iota(jnp.int32, (q, q), 0)
    ci = lax.broadcasted_iota(jnp.int32, (q, q), 1)
    tri = ri >= ci
    tri_bf = jnp.where(tri, 1.0, 0.0).astype(BF16)
    a_neg = -jnp.exp(alog_ref[...])

    def cumsum_rows(v):
        hi = v.astype(BF16)
        r1 = v - hi.astype(F32)
        mid = r1.astype(BF16)
        lo = (r1 - mid.astype(F32)).astype(BF16)
        return _dot(tri_bf, hi) + _dot(tri_bf, mid) + _dot(tri_bf, lo)

    def lane_bcast(v, h):
        return jnp.broadcast_to(v[:, h:h + 1], (q, LANES))

    def chunk(c):
        r0 = c * q
        u = cb_ref[...]
        for k in range(SSD_CONV):
            u = u + cw_ref[k:k + 1, :] * ext_ref[pl.ds(r0 + halo - (SSD_CONV - 1) + k, q), :]
        u = _silu(u)
        x = u[:, :SSD_WIDTH]
        dt_in = dt_ref[pl.ds(r0, q), :] + dtb_ref[...]
        dt = jnp.maximum(dt_in, 0.0) + jnp.log1p(jnp.exp(-jnp.abs(dt_in)))
        acum = cumsum_rows(dt * a_neg)
        acum_t = acum.T
        dt_t = dt.T

        y_parts = []
        for g in range(SSD_GROUPS):
            bg = u[:, SSD_WIDTH + g * SSD_STATE:SSD_WIDTH + (g + 1) * SSD_STATE]
            cg = u[:, SSD_WIDTH + (SSD_GROUPS + g) * SSD_STATE:SSD_WIDTH + (SSD_GROUPS + g + 1) * SSD_STATE]
            cg_bf = cg.astype(BF16)
            cb = _dot_nt(cg_bf, bg.astype(BF16))
            st = state_ref[g]
            y_off = _dot(cg_bf, st.astype(BF16))
            xd_parts, decay_parts = [], []
            for pp in range(pairs_per_group):
                p = g * pairs_per_group + pp
                h0 = HEAD_PAIR * p
                xp = x[:, p * LANES:(p + 1) * LANES]
                cols = [lane_bcast(acum, h0 + e) for e in range(HEAD_PAIR)]
                colc = jnp.where(low, cols[0], cols[1])
                ms = []
                for e in range(HEAD_PAIR):
                    h = h0 + e
                    seg = jnp.exp(jnp.where(tri, cols[e] - acum_t[h:h + 1, :], NEG_BIG))
                    ms.append(cb * seg * dt_t[h:h + 1, :])
                m_cat = jnp.concatenate(ms, axis=1).astype(BF16)
                x_blk = jnp.concatenate([jnp.where(low, xp, 0.0), jnp.where(low, 0.0, xp)],
                                        axis=0).astype(BF16)
                y_diag = _dot(m_cat, x_blk)
                y_o = y_off[:, pp * LANES:(pp + 1) * LANES] * jnp.exp(colc)
                y_parts.append(y_diag + y_o + dskip_ref[:, p * LANES:(p + 1) * LANES] * xp)
                last = colc[q - 1:q, :]
                dtc = jnp.where(low, lane_bcast(dt, h0), lane_bcast(dt, h0 + 1))
                xd_parts.append(xp * (dtc * jnp.exp(last - colc)))
                decay_parts.append(jnp.exp(last))
            xd = jnp.concatenate(xd_parts, axis=1).astype(BF16)
            decay = jnp.concatenate(decay_parts, axis=1)
            state_ref[g] = st * decay + _dot(bg.T.astype(BF16), xd)

        z = z_ref[pl.ds(r0, q), :].astype(F32)
        outs = []
        for g in range(SSD_GROUPS):
            yg = jnp.concatenate(y_parts[g * pairs_per_group:(g + 1) * pairs_per_group], axis=1)
            yg = yg * _silu(z[:, g * gw:(g + 1) * gw])
            outs.append(yg * _rms_scale(yg) * g_ref[:, g * gw:(g + 1) * gw])
        y_ref[pl.ds(r0, q), :] = jnp.concatenate(outs, axis=1).astype(y_ref.dtype)

    for c in range(tb // q):
        chunk(c)
    ext_ref[0:halo, :] = ext_ref[tb:tb + halo, :]


def _ssd(main3, dt3, conv_w, conv_b, dt_bias, a_log, d_skip, norm_g, *, tb):
    b, seq, _ = main3.shape
    z_blk = (3 * ATTN_WIDTH) // SSD_WIDTH
    xs_blk = (3 * ATTN_WIDTH + SSD_WIDTH) // SSD_WIDTH
    bc_blk = (3 * ATTN_WIDTH + 2 * SSD_WIDTH) // BC_WIDTH

    def const(shape):
        return pl.BlockSpec(shape, lambda i, j: (0,) * len(shape))

    return pl.pallas_call(
        _ssd_kernel,
        grid=(b, seq // tb),
        in_specs=[
            pl.BlockSpec((None, tb, SSD_WIDTH), lambda i, j: (i, j, z_blk)),
            pl.BlockSpec((None, tb, SSD_WIDTH), lambda i, j: (i, j, xs_blk)),
            pl.BlockSpec((None, tb, BC_WIDTH), lambda i, j: (i, j, bc_blk)),
            pl.BlockSpec((None, tb, LANES), lambda i, j: (i, j, 0)),
            const((SSD_CONV, CONV_CH)), const((1, CONV_CH)),
            const((1, LANES)), const((1, LANES)),
            const((1, SSD_WIDTH)), const((1, SSD_WIDTH)),
        ],
        out_specs=pl.BlockSpec((None, tb, SSD_WIDTH), lambda i, j: (i, j, 0)),
        out_shape=jax.ShapeDtypeStruct((b, seq, SSD_WIDTH), BF16),
        scratch_shapes=[
            pltpu.VMEM((tb + SUBLANES, CONV_CH), F32),
            pltpu.VMEM((SSD_GROUPS, SSD_STATE, SSD_WIDTH // SSD_GROUPS), F32),
        ],
        compiler_params=_params(("parallel", "arbitrary")),
        name="ssd",
    )(main3, main3, main3, dt3, conv_w, conv_b, dt_bias, a_log, d_skip, norm_g)


def _out_proj_kernel(a_ref, y_ref, x_ref, g_ref, w_ref, o_ref):
    a = a_ref[...].astype(F32)
    an = (a * _rms_scale(a) * g_ref[...]).astype(BF16)
    mixed = _dot(an, w_ref[0:ATTN_WIDTH, :]) + _dot(y_ref[...], w_ref[ATTN_WIDTH:, :])
    o_ref[...] = x_ref[...] + mixed


def _out_proj(attn, y, x, g, w, *, tm):
    m, d = x.shape
    return pl.pallas_call(
        _out_proj_kernel,
        grid=(m // tm,),
        in_specs=[
            pl.BlockSpec((tm, ATTN_WIDTH), lambda i: (i, 0)),
            pl.BlockSpec((tm, SSD_WIDTH), lambda i: (i, 0)),
            pl.BlockSpec((tm, d), lambda i: (i, 0)),
            pl.BlockSpec((1, ATTN_WIDTH), lambda i: (0, 0)),
            pl.BlockSpec(w.shape, lambda i: (0, 0)),
        ],
        out_specs=pl.BlockSpec((tm, d), lambda i: (i, 0)),
        out_shape=jax.ShapeDtypeStruct((m, d), F32),
        compiler_params=_params(("parallel",)),
        name="out_proj",
    )(attn, y, x, g, w)


def _mlp_kernel(x_ref, g_ref, w1_ref, w2_ref, gf_ref, o_ref, hn_ref, *, final_norm):
    f = pl.program_id(1)

    @pl.when(f == 0)
    def _():
        x = x_ref[...]
        hn_ref[...] = (x * _rms_scale(x) * g_ref[...]).astype(BF16)
        o_ref[...] = x

    h = jnp.maximum(_dot(hn_ref[...], w1_ref[...]), 0.0)
    o_ref[...] += _dot((h * h).astype(BF16), w2_ref[...])

    if final_norm:
        @pl.when(f == pl.num_programs(1) - 1)
        def _():
            o = o_ref[...]
            o_ref[...] = o * _rms_scale(o) * gf_ref[...]


def _mlp(x, g, w1, w2, gf, *, tm, tf, final_norm):
    m, d = x.shape
    dff = w1.shape[1]
    return pl.pallas_call(
        functools.partial(_mlp_kernel, final_norm=final_norm),
        grid=(m // tm, dff // tf),
        in_specs=[
            pl.BlockSpec((tm, d), lambda i, j: (i, 0)),
            pl.BlockSpec((1, d), lambda i, j: (0, 0)),
            pl.BlockSpec((d, tf), lambda i, j: (0, j)),
            pl.BlockSpec((tf, d), lambda i, j: (j, 0)),
            pl.BlockSpec((1, d), lambda i, j: (0, 0)),
        ],
        out_specs=pl.BlockSpec((tm, d), lambda i, j: (i, 0)),
        out_shape=jax.ShapeDtypeStruct((m, d), F32),
        scratch_shapes=[pltpu.VMEM((tm, d), BF16)],
        compiler_params=_params(("parallel", "arbitrary")),
        name="mlp",
    )(x, g, w1, w2, gf)


def _alibi_slopes():
    return jnp.asarray(2.0 ** (-8.0 * (np.arange(ATTN_HEADS) + 1) / ATTN_HEADS), dtype=F32)


def _pad_lanes(v):
    return jnp.pad(v, ((0, 0), (0, LANES - v.shape[1])))


def kernel(x, ln1_g, w_in, conv_w, conv_b, dt_bias, a_log, d_skip, attn_norm_g, ssd_norm_g, w_out,
           ln2_g, w_mlp_in, w_mlp_out, final_norm_g):
    b, seq, d = x.shape
    m = b * seq
    slopes = _alibi_slopes()
    xf = x.reshape(m, d)
    for l in range(DEPTH):
        w_main = w_in[l, :, :MAIN_WIDTH].astype(BF16)
        w_dt = _pad_lanes(w_in[l, :, MAIN_WIDTH:]).astype(BF16)
        main, dt_raw = _in_proj(xf, ln1_g[l][None, :], w_main, w_dt, tm=1024, tn=512)
        main3 = main.reshape(b, seq, MAIN_WIDTH)
        attn = _attention(slopes, main3)
        y = _ssd(
            main3, dt_raw.reshape(b, seq, LANES), conv_w[l], conv_b[l][None, :],
            _pad_lanes(dt_bias[l][None, :]), _pad_lanes(a_log[l][None, :]),
            jnp.repeat(d_skip[l], SSD_HEAD_DIM)[None, :], ssd_norm_g[l][None, :], tb=256)
        x1 = _out_proj(attn.reshape(m, ATTN_WIDTH), y.reshape(m, SSD_WIDTH), xf,
                       attn_norm_g[l][None, :], w_out[l].astype(BF16), tm=512)
        xf = _mlp(x1, ln2_g[l][None, :], w_mlp_in[l].astype(BF16), w_mlp_out[l].astype(BF16),
                  final_norm_g[None, :], tm=1024, tf=512, final_norm=(l == DEPTH - 1))
    return xf.reshape(b, seq, d)
```

```python
import functools

import jax
import jax.numpy as jnp
import numpy as np
from jax import lax
from jax.experimental import pallas as pl
from jax.experimental.pallas import tpu as pltpu

D_MODEL = 2048
DEPTH = 2
HEAD_DIM = 64
ATTN_WIDTH = D_MODEL // 2
ATTN_HEADS = ATTN_WIDTH // HEAD_DIM
DILATED_BRANCHES = ((128, 1), (512, 4), (2048, 16))
ATTN_BLOCK = 128
SSD_WIDTH = D_MODEL // 2
SSD_HEAD_DIM = 64
SSD_HEADS = SSD_WIDTH // SSD_HEAD_DIM
SSD_GROUPS = 2
SSD_STATE = 128
SSD_CONV = 4
SSD_CHUNK = 128
BC_WIDTH = 2 * SSD_GROUPS * SSD_STATE
CONV_CH = SSD_WIDTH + BC_WIDTH
MAIN_WIDTH = 3 * ATTN_WIDTH + SSD_WIDTH + CONV_CH
D_FF = 4 * D_MODEL
NORM_EPS = 1e-5

LANES = 128
SUBLANES = 8
HEAD_PAIR = LANES // HEAD_DIM
NEG_BIG = -1e30
LOG2_E = 1.4426950408889634
VMEM_LIMIT = 56 * 1024 * 1024

F32 = jnp.float32
BF16 = jnp.bfloat16


def _dot(a, b):
    return jnp.dot(a, b, preferred_element_type=F32)


def _dot_nt(a, b):
    return lax.dot_general(a, b, (((1,), (1,)), ((), ())), preferred_element_type=F32)


def _rms_scale(x):
    return lax.rsqrt(jnp.mean(x * x, axis=-1, keepdims=True) + NORM_EPS)


def _silu(x):
    return x * (1.0 / (1.0 + jnp.exp(-x)))


def _params(semantics):
    return pltpu.CompilerParams(dimension_semantics=semantics, vmem_limit_bytes=VMEM_LIMIT)


def _in_proj_kernel(x_ref, g_ref, w_ref, wdt_ref, main_ref, dt_ref, hn_ref, *, tn):
    x = x_ref[...]
    hn_ref[...] = (x * _rms_scale(x) * g_ref[...]).astype(BF16)
    dt_ref[...] = _dot(hn_ref[...], wdt_ref[...])
    for j in range(w_ref.shape[1] // tn):
        cols = slice(j * tn, (j + 1) * tn)
        main_ref[:, cols] = _dot(hn_ref[...], w_ref[:, cols]).astype(main_ref.dtype)


def _resident(shape):
    return pl.BlockSpec(shape, lambda i: (0,) * len(shape), pipeline_mode=pl.Buffered(1))


def _in_proj(x, g, w_main, w_dt, *, tm, tn):
    m, d = x.shape
    n = w_main.shape[1]
    assert m % tm == 0 and n % tn == 0
    return pl.pallas_call(
        functools.partial(_in_proj_kernel, tn=tn),
        grid=(m // tm,),
        in_specs=[
            pl.BlockSpec((tm, d), lambda i: (i, 0)),
            _resident((1, d)),
            _resident((d, n)),
            _resident((d, LANES)),
        ],
        out_specs=[
            pl.BlockSpec((tm, n), lambda i: (i, 0)),
            pl.BlockSpec((tm, LANES), lambda i: (i, 0)),
        ],
        out_shape=[
            jax.ShapeDtypeStruct((m, n), BF16),
            jax.ShapeDtypeStruct((m, LANES), F32),
        ],
        scratch_shapes=[pltpu.VMEM((tm, d), BF16)],
        compiler_params=_params(("parallel",)),
        name="in_proj",
    )(x, g, w_main, w_dt)


def _attn_kernel(slopes_ref, q_ref, k_ref, v_ref, o_ref,
                 qf_ref, kf_ref, vf_ref, qm_ref, km_ref, vm_ref, of_ref, qd_ref, kd_ref, vd_ref,
                 m_ref, l_ref, acc_ref, bias_ref):
    seq = q_ref.shape[0]
    blk = ATTN_BLOCK
    hp = pl.program_id(1)

    qf_ref[...] = q_ref[...].astype(F32)
    kf_ref[...] = k_ref[...].astype(F32)
    vf_ref[...] = v_ref[...].astype(F32)

    low = lax.broadcasted_iota(jnp.int32, (blk, LANES), 1) < HEAD_DIM
    ones_bf = jnp.ones((2 * blk, LANES), BF16)

    row = lax.broadcasted_iota(jnp.int32, (2 * blk, 2 * blk), 0)
    col = lax.broadcasted_iota(jnp.int32, (2 * blk, 2 * blk), 1)
    delta = (row & (blk - 1)) - col + blk
    slope = jnp.where(row < blk, slopes_ref[HEAD_PAIR * hp], slopes_ref[HEAD_PAIR * hp + 1])
    for t, (window, dil) in enumerate(DILATED_BRANCHES):
        valid = (delta >= 0) & (delta <= window // dil)
        bias = jnp.where(valid, -slope * (delta * dil).astype(F32), NEG_BIG)
        bias_ref[2 * t] = bias
        bias_ref[2 * t + 1] = jnp.where(col >= blk, bias, NEG_BIG)

    def pair_rows(x):
        return jnp.where(low, jnp.broadcast_to(x[:blk], (blk, LANES)),
                         jnp.broadcast_to(x[blk:], (blk, LANES)))

    def pair_halves(x):
        return jnp.where(low, x[:blk], x[blk:])

    scale = jnp.asarray(HEAD_DIM ** -0.5, BF16)
    zero_bf = jnp.zeros((), BF16)
    n_blocks = seq // blk

    def block_rows(i):
        return pl.ds(pl.multiple_of(i * blk, blk), blk)

    (_, d_lo), (_, d_mid), (_, d_hi) = DILATED_BRANCHES
    ratio = d_hi // d_mid
    assert d_lo == 1 and d_hi == d_mid * ratio and seq % (blk * d_hi) == 0
    mid_len = seq // d_mid

    def split_block(idx, dil):
        nb = n_blocks // dil
        assert nb & (nb - 1) == 0
        return idx >> (nb.bit_length() - 1), idx & (nb - 1)

    def mid_rows(idx):
        r, n = split_block(idx, d_mid)
        return pl.ds(r + n * (blk * d_mid), blk, stride=d_mid)

    def hi_rows_in_mid(idx):
        r, n = split_block(idx, d_hi)
        r_mid, sub = r & (d_mid - 1), r >> (d_mid.bit_length() - 1)
        return pl.ds(r_mid * mid_len + n * (blk * ratio) + sub, blk, stride=ratio)

    def gather_mid(idx, carry):
        rows = mid_rows(idx)
        for f_ref, mid_ref, d_ref in ((qf_ref, qm_ref, qd_ref), (kf_ref, km_ref, kd_ref), (vf_ref, vm_ref, vd_ref)):
            x = f_ref[rows, :]
            mid_ref[block_rows(idx), :] = x
            d_ref[block_rows(idx), :] = x.astype(BF16)
        return carry

    def gather_hi(idx, carry):
        rows = hi_rows_in_mid(idx)
        for mid_ref, d_ref in ((qm_ref, qd_ref), (km_ref, kd_ref), (vm_ref, vd_ref)):
            d_ref[block_rows(idx), :] = mid_ref[rows, :].astype(BF16)
        return carry

    branch_io = (
        ((q_ref, k_ref, v_ref), block_rows, None),
        ((qd_ref, kd_ref, vd_ref), block_rows, gather_mid),
        ((qd_ref, kd_ref, vd_ref), hi_rows_in_mid, gather_hi),
    )
    for t, (window, dil) in enumerate(DILATED_BRANCHES):
        nb = n_blocks // dil
        (q_src, k_src, v_src), out_rows_of, gather = branch_io[t]
        if gather is not None:
            lax.fori_loop(0, n_blocks, gather, 0, unroll=4)

        def body(idx, carry, t=t, nb=nb, out_rows_of=out_rows_of, q_src=q_src, k_src=k_src, v_src=v_src):
            first = (idx & (nb - 1)) == 0
            cur = block_rows(idx)
            prev = block_rows(jnp.where(first, idx, idx - 1))

            q = q_src[cur, :] * scale
            q2 = jnp.concatenate([jnp.where(low, q, zero_bf), jnp.where(low, zero_bf, q)], axis=0)
            kk = jnp.concatenate([k_src[prev, :], k_src[cur, :]], axis=0)
            vv = jnp.concatenate([jnp.concatenate([v_src[prev, :], v_src[cur, :]], axis=0), ones_bf], axis=1)

            s = _dot_nt(q2, kk) + bias_ref[2 * t + first.astype(jnp.int32)]
            m = jnp.max(s, axis=-1, keepdims=True)
            p = jnp.exp(s - m)
            pv = _dot(p.astype(BF16), vv)

            out_rows = out_rows_of(idx)
            acc_ref[t, out_rows, :] = pair_halves(pv[:, :LANES])
            l_ref[t, out_rows, :] = pair_halves(pv[:, LANES:])
            m_ref[t, out_rows, :] = pair_rows(m)
            return carry

        lax.fori_loop(0, n_blocks, body, 0, unroll=8)

    def merge(idx, carry):
        tok = mid_rows(idx)
        rows = (tok, block_rows(idx), block_rows(idx))
        ms = [m_ref[t, rows[t], :] for t in range(len(DILATED_BRANCHES))]
        m_all = functools.reduce(jnp.maximum, ms)
        ws = [jnp.exp(m_b - m_all) for m_b in ms]
        num = sum(w * acc_ref[t, rows[t], :] for t, w in enumerate(ws))
        den = sum(w * l_ref[t, rows[t], :] for t, w in enumerate(ws))
        of_ref[tok, :] = num / den
        return carry

    lax.fori_loop(0, n_blocks, merge, 0, unroll=2)
    o_ref[...] = of_ref[...].astype(o_ref.dtype)


def _attention(slopes, main3):
    b, seq, _ = main3.shape
    n_pairs = ATTN_WIDTH // LANES
    blk = ATTN_BLOCK

    def spec(offset):
        return pl.BlockSpec((None, seq, LANES), lambda i, j: (i, 0, offset + j))

    return pl.pallas_call(
        _attn_kernel,
        grid=(b, n_pairs),
        in_specs=[
            pl.BlockSpec(memory_space=pltpu.SMEM),
            spec(0), spec(n_pairs), spec(2 * n_pairs),
        ],
        out_specs=pl.BlockSpec((None, seq, LANES), lambda i, j: (i, 0, j)),
        out_shape=jax.ShapeDtypeStruct((b, seq, ATTN_WIDTH), BF16),
        scratch_shapes=[pltpu.VMEM((seq, LANES), F32)] * 7
        + [pltpu.VMEM((seq, LANES), BF16)] * 3
        + [pltpu.VMEM((len(DILATED_BRANCHES), seq, LANES), F32)] * 3
        + [pltpu.VMEM((2 * len(DILATED_BRANCHES), 2 * blk, 2 * blk), F32)],
        compiler_params=_params(("parallel", "parallel")),
        name="attention",
    )(slopes, main3, main3, main3)


def _ssd_kernel(z_ref, xs_ref, bc_ref, dt_ref, cw_ref, cb_ref, dtb_ref, alog_ref, dskip_ref, g_ref,
                y_ref, carry_ref, state_ref):
    tb = xs_ref.shape[0]
    q = SSD_CHUNK
    gw = SSD_WIDTH // SSD_GROUPS
    pairs_per_group = gw // LANES
    conv_cols = 512

    step = pl.program_id(1)
    slot = step & 1

    @pl.when(step == 0)
    def _():
        carry_ref[0] = jnp.zeros((q, CONV_CH), BF16)
        state_ref[...] = jnp.zeros(state_ref.shape, F32)

    carry_ref[1 - slot, :, 0:SSD_WIDTH] = xs_ref[tb - q:tb, :]
    carry_ref[1 - slot, :, SSD_WIDTH:CONV_CH] = bc_ref[tb - q:tb, :]

    low = lax.broadcasted_iota(jnp.int32, (q, LANES), 1) < SSD_HEAD_DIM
    ri = lax.broadcasted_iota(jnp.int32, (q, q), 0)
    ci = lax.broadcasted_iota(jnp.int32, (q, q), 1)
    tri = ri >= ci
    tri_bf = jnp.where(tri, 1.0, 0.0).astype(BF16)
    a_neg = -jnp.exp(alog_ref[...])

    sr = lax.broadcasted_iota(jnp.int32, (SSD_CONV * q, q), 0)
    sc = lax.broadcasted_iota(jnp.int32, (SSD_CONV * q, q), 1)
    src_row = (sr & (q - 1)) + (sr >> (q.bit_length() - 1)) - (SSD_CONV - 1)
    shift_cur = jnp.where(sc == src_row, 1.0, 0.0).astype(BF16)
    shift_prev = jnp.where(sc == src_row + q, 1.0, 0.0).astype(BF16)

    def cumsum_rows(v):
        hi = v.astype(BF16)
        r1 = v - hi.astype(F32)
        mid = r1.astype(BF16)
        lo = (r1 - mid.astype(F32)).astype(BF16)
        return _dot(tri_bf, hi) + _dot(tri_bf, mid) + _dot(tri_bf, lo)

    def lane_bcast(v, h):
        return jnp.broadcast_to(v[:, h:h + 1], (q, LANES))

    def chunk(c):
        r0 = c * q
        u_parts = []
        for j in range(CONV_CH // conv_cols):
            cols = slice(j * conv_cols, (j + 1) * conv_cols)
            src_ref, off = (xs_ref, 0) if j * conv_cols < SSD_WIDTH else (bc_ref, SSD_WIDTH)
            src_cols = slice(j * conv_cols - off, (j + 1) * conv_cols - off)
            prev = carry_ref[slot, :, cols] if c == 0 else src_ref[r0 - q:r0, src_cols]
            taps = _dot(shift_prev, prev) + _dot(shift_cur, src_ref[r0:r0 + q, src_cols])
            uj = cb_ref[:, cols]
            for k in range(SSD_CONV):
                uj = uj + cw_ref[k:k + 1, cols] * taps[k * q:(k + 1) * q]
            u_parts.append(_silu(uj))
        u = jnp.concatenate(u_parts, axis=1)
        x = u[:, :SSD_WIDTH]
        dt_in = dt_ref[pl.ds(r0, q), :] + dtb_ref[...]
        dt = jnp.maximum(dt_in, 0.0) + jnp.log(1.0 + jnp.exp(-jnp.abs(dt_in)))
        acum = cumsum_rows(dt * a_neg) * LOG2_E
        acum_t = acum.T
        dt_t = dt.T

        y_parts = []
        for g in range(SSD_GROUPS):
            bg = u[:, SSD_WIDTH + g * SSD_STATE:SSD_WIDTH + (g + 1) * SSD_STATE]
            cg = u[:, SSD_WIDTH + (SSD_GROUPS + g) * SSD_STATE:SSD_WIDTH + (SSD_GROUPS + g + 1) * SSD_STATE]
            cg_bf = cg.astype(BF16)
            cb = _dot_nt(cg_bf, bg.astype(BF16))
            st = state_ref[g]
            y_off = _dot(cg_bf, st.astype(BF16))
            xd_parts, decay_parts = [], []
            for pp in range(pairs_per_group):
                p = g * pairs_per_group + pp
                h0 = HEAD_PAIR * p
                xp = x[:, p * LANES:(p + 1) * LANES]
                cols = [lane_bcast(acum, h0 + e) for e in range(HEAD_PAIR)]
                colc = jnp.where(low, cols[0], cols[1])
                ms = []
                for e in range(HEAD_PAIR):
                    h = h0 + e
                    seg = jnp.exp2(jnp.where(tri, cols[e] - acum_t[h:h + 1, :], NEG_BIG))
                    ms.append(cb * seg * dt_t[h:h + 1, :])
                m_cat = jnp.concatenate(ms, axis=1).astype(BF16)
                x_blk = jnp.concatenate([jnp.where(low, xp, 0.0), jnp.where(low, 0.0, xp)],
                                        axis=0).astype(BF16)
                y_diag = _dot(m_cat, x_blk)
                y_o = y_off[:, pp * LANES:(pp + 1) * LANES] * jnp.exp2(colc)
                y_parts.append(y_diag + y_o + dskip_ref[:, p * LANES:(p + 1) * LANES] * xp)
                last = colc[q - 1:q, :]
                dtc = jnp.where(low, lane_bcast(dt, h0), lane_bcast(dt, h0 + 1))
                xd_parts.append(xp * (dtc * jnp.exp2(last - colc)))
                decay_parts.append(jnp.exp2(last))
            xd = jnp.concatenate(xd_parts, axis=1).astype(BF16)
            decay = jnp.concatenate(decay_parts, axis=1)
            state_ref[g] = st * decay + _dot(bg.T.astype(BF16), xd)

        z = z_ref[pl.ds(r0, q), :].astype(F32)
        outs = []
        for g in range(SSD_GROUPS):
            yg = jnp.concatenate(y_parts[g * pairs_per_group:(g + 1) * pairs_per_group], axis=1)
            yg = yg * _silu(z[:, g * gw:(g + 1) * gw])
            outs.append(yg * _rms_scale(yg) * g_ref[:, g * gw:(g + 1) * gw])
        y_ref[pl.ds(r0, q), :] = jnp.concatenate(outs, axis=1).astype(y_ref.dtype)

    for c in range(tb // q):
        chunk(c)


def _ssd(main3, dt3, conv_w, conv_b, dt_bias, a_log, d_skip, norm_g, *, tb):
    b, seq, _ = main3.shape
    z_blk = (3 * ATTN_WIDTH) // SSD_WIDTH
    xs_blk = (3 * ATTN_WIDTH + SSD_WIDTH) // SSD_WIDTH
    bc_blk = (3 * ATTN_WIDTH + 2 * SSD_WIDTH) // BC_WIDTH

    def const(shape):
        return pl.BlockSpec(shape, lambda i, j: (0,) * len(shape))

    return pl.pallas_call(
        _ssd_kernel,
        grid=(b, seq // tb),
        in_specs=[
            pl.BlockSpec((None, tb, SSD_WIDTH), lambda i, j: (i, j, z_blk)),
            pl.BlockSpec((None, tb, SSD_WIDTH), lambda i, j: (i, j, xs_blk)),
            pl.BlockSpec((None, tb, BC_WIDTH), lambda i, j: (i, j, bc_blk)),
            pl.BlockSpec((None, tb, LANES), lambda i, j: (i, j, 0)),
            const((SSD_CONV, CONV_CH)), const((1, CONV_CH)),
            const((1, LANES)), const((1, LANES)),
            const((1, SSD_WIDTH)), const((1, SSD_WIDTH)),
        ],
        out_specs=pl.BlockSpec((None, tb, SSD_WIDTH), lambda i, j: (i, j, 0)),
        out_shape=jax.ShapeDtypeStruct((b, seq, SSD_WIDTH), BF16),
        scratch_shapes=[
            pltpu.VMEM((2, SSD_CHUNK, CONV_CH), BF16),
            pltpu.VMEM((SSD_GROUPS, SSD_STATE, SSD_WIDTH // SSD_GROUPS), F32),
        ],
        compiler_params=_params(("parallel", "arbitrary")),
        name="ssd",
    )(main3, main3, main3, dt3, conv_w, conv_b, dt_bias, a_log, d_skip, norm_g)


def _out_proj_kernel(a_ref, y_ref, x_ref, g_ref, w_ref, o_ref):
    a = a_ref[...].astype(F32)
    an = (a * _rms_scale(a) * g_ref[...]).astype(BF16)
    mixed = _dot(an, w_ref[0:ATTN_WIDTH, :]) + _dot(y_ref[...], w_ref[ATTN_WIDTH:, :])
    o_ref[...] = x_ref[...] + mixed


def _out_proj(attn, y, x, g, w, *, tm):
    m, d = x.shape
    return pl.pallas_call(
        _out_proj_kernel,
        grid=(m // tm,),
        in_specs=[
            pl.BlockSpec((tm, ATTN_WIDTH), lambda i: (i, 0)),
            pl.BlockSpec((tm, SSD_WIDTH), lambda i: (i, 0)),
            pl.BlockSpec((tm, d), lambda i: (i, 0)),
            _resident((1, ATTN_WIDTH)),
            _resident(w.shape),
        ],
        out_specs=pl.BlockSpec((tm, d), lambda i: (i, 0)),
        out_shape=jax.ShapeDtypeStruct((m, d), F32),
        compiler_params=_params(("parallel",)),
        name="out_proj",
    )(attn, y, x, g, w)


def _mlp_kernel(x_ref, g_ref, w1_ref, w2_ref, gf_ref, o_ref, hn_ref, *, final_norm):
    f = pl.program_id(1)

    @pl.when(f == 0)
    def _():
        x = x_ref[...]
        hn_ref[...] = (x * _rms_scale(x) * g_ref[...]).astype(BF16)
        o_ref[...] = x

    h = jnp.maximum(_dot(hn_ref[...], w1_ref[...]), 0.0)
    o_ref[...] += _dot((h * h).astype(BF16), w2_ref[...])

    if final_norm:
        @pl.when(f == pl.num_programs(1) - 1)
        def _():
            o = o_ref[...]
            o_ref[...] = o * _rms_scale(o) * gf_ref[...]


def _mlp(x, g, w1, w2, gf, *, tm, tf, final_norm):
    m, d = x.shape
    dff = w1.shape[1]
    return pl.pallas_call(
        functools.partial(_mlp_kernel, final_norm=final_norm),
        grid=(m // tm, dff // tf),
        in_specs=[
            pl.BlockSpec((tm, d), lambda i, j: (i, 0)),
            pl.BlockSpec((1, d), lambda i, j: (0, 0)),
            pl.BlockSpec((d, tf), lambda i, j: (0, j)),
            pl.BlockSpec((tf, d), lambda i, j: (j, 0)),
            pl.BlockSpec((1, d), lambda i, j: (0, 0)),
        ],
        out_specs=pl.BlockSpec((tm, d), lambda i, j: (i, 0)),
        out_shape=jax.ShapeDtypeStruct((m, d), F32),
        scratch_shapes=[pltpu.VMEM((tm, d), BF16)],
        compiler_params=_params(("parallel", "arbitrary")),
        name="mlp",
    )(x, g, w1, w2, gf)


def _alibi_slopes():
    return jnp.asarray(2.0 ** (-8.0 * (np.arange(ATTN_HEADS) + 1) / ATTN_HEADS), dtype=F32)


def _pad_lanes(v):
    return jnp.pad(v, ((0, 0), (0, LANES - v.shape[1])))


def kernel(x, ln1_g, w_in, conv_w, conv_b, dt_bias, a_log, d_skip, attn_norm_g, ssd_norm_g, w_out,
           ln2_g, w_mlp_in, w_mlp_out, final_norm_g):
    b, seq, d = x.shape
    m = b * seq
    slopes = _alibi_slopes()
    xf = x.reshape(m, d)
    for l in range(DEPTH):
        w_main = w_in[l, :, :MAIN_WIDTH].astype(BF16)
        w_dt = _pad_lanes(w_in[l, :, MAIN_WIDTH:]).astype(BF16)
        main, dt_raw = _in_proj(xf, ln1_g[l][None, :], w_main, w_dt, tm=512, tn=512)
        main3 = main.reshape(b, seq, MAIN_WIDTH)
        attn = _attention(slopes, main3)
        y = _ssd(
            main3, dt_raw.reshape(b, seq, LANES), conv_w[l], conv_b[l][None, :],
            _pad_lanes(dt_bias[l][None, :]), _pad_lanes(a_log[l][None, :]),
            jnp.repeat(d_skip[l], SSD_HEAD_DIM)[None, :], ssd_norm_g[l][None, :], tb=256)
        x1 = _out_proj(attn.reshape(m, ATTN_WIDTH), y.reshape(m, SSD_WIDTH), xf,
                       attn_norm_g[l][None, :], w_out[l].astype(BF16), tm=512)
        xf = _mlp(x1, ln2_g[l][None, :], w_mlp_in[l].astype(BF16), w_mlp_out[l].astype(BF16),
                  final_norm_g[None, :], tm=1024, tf=512, final_norm=(l == DEPTH - 1))
    return xf.reshape(b, seq, d)
```

```python
import functools

import jax
import jax.numpy as jnp
import numpy as np
from jax import lax
from jax.experimental import pallas as pl
from jax.experimental.pallas import tpu as pltpu

D_MODEL = 2048
DEPTH = 2
HEAD_DIM = 64
ATTN_WIDTH = D_MODEL // 2
ATTN_HEADS = ATTN_WIDTH // HEAD_DIM
DILATED_BRANCHES = ((128, 1), (512, 4), (2048, 16))
ATTN_BLOCK = 128
SSD_WIDTH = D_MODEL // 2
SSD_HEAD_DIM = 64
SSD_HEADS = SSD_WIDTH // SSD_HEAD_DIM
SSD_GROUPS = 2
SSD_STATE = 128
SSD_CONV = 4
SSD_CHUNK = 128
BC_WIDTH = 2 * SSD_GROUPS * SSD_STATE
CONV_CH = SSD_WIDTH + BC_WIDTH
MAIN_WIDTH = 3 * ATTN_WIDTH + SSD_WIDTH + CONV_CH
D_FF = 4 * D_MODEL
NORM_EPS = 1e-5

LANES = 128
SUBLANES = 8
HEAD_PAIR = LANES // HEAD_DIM
NEG_BIG = -1e30
LOG2_E = 1.4426950408889634
VMEM_LIMIT = 56 * 1024 * 1024

F32 = jnp.float32
BF16 = jnp.bfloat16


def _dot(a, b):
    return jnp.dot(a, b, preferred_element_type=F32)


def _dot_nt(a, b):
    return lax.dot_general(a, b, (((1,), (1,)), ((), ())), preferred_element_type=F32)


def _rms_scale(x):
    return lax.rsqrt(jnp.mean(x * x, axis=-1, keepdims=True) + NORM_EPS)


def _silu(x):
    return x * (1.0 / (1.0 + jnp.exp2(x * (-LOG2_E))))


def _params(semantics):
    return pltpu.CompilerParams(dimension_semantics=semantics, vmem_limit_bytes=VMEM_LIMIT)


def _in_proj_kernel(x_ref, g_ref, w_ref, wdt_ref, main_ref, dt_ref, hn_ref, *, tn):
    x = x_ref[...]
    hn_ref[...] = (x * _rms_scale(x) * g_ref[...]).astype(BF16)
    dt_ref[...] = _dot(hn_ref[...], wdt_ref[...])
    for j in range(main_ref.shape[1] // tn):
        cols = slice(j * tn, (j + 1) * tn)
        main_ref[:, cols] = _dot(hn_ref[...], w_ref[:, cols]).astype(main_ref.dtype)


def _resident(shape, layer=None):
    if layer is None:
        return pl.BlockSpec(shape, lambda i: (0,) * len(shape), pipeline_mode=pl.Buffered(1))
    return pl.BlockSpec((None,) + tuple(shape), lambda i: (layer,) + (0,) * len(shape),
                        pipeline_mode=pl.Buffered(1))


def _in_proj(x, g, w_all, w_dt, layer, *, tm, tn):
    m, d = x.shape
    n = MAIN_WIDTH
    assert m % tm == 0 and n % tn == 0
    return pl.pallas_call(
        functools.partial(_in_proj_kernel, tn=tn),
        grid=(m // tm,),
        in_specs=[
            pl.BlockSpec((tm, d), lambda i: (i, 0)),
            _resident((1, d)),
            _resident(w_all.shape[1:], layer),
            _resident((d, LANES), layer),
        ],
        out_specs=[
            pl.BlockSpec((tm, n), lambda i: (i, 0)),
            pl.BlockSpec((tm, LANES), lambda i: (i, 0)),
        ],
        out_shape=[
            jax.ShapeDtypeStruct((m, n), BF16),
            jax.ShapeDtypeStruct((m, LANES), F32),
        ],
        scratch_shapes=[pltpu.VMEM((tm, d), BF16)],
        compiler_params=_params(("parallel",)),
        name="in_proj",
    )(x, g, w_all, w_dt)


def _attn_kernel(slopes_ref, q_ref, k_ref, v_ref, o_ref,
                 qf_ref, kf_ref, vf_ref, qm_ref, km_ref, vm_ref, of_ref, qd_ref, kd_ref, vd_ref,
                 m_ref, l_ref, acc_ref, bias_ref):
    seq = q_ref.shape[0]
    blk = ATTN_BLOCK
    hp = pl.program_id(1)

    qf_ref[...] = q_ref[...].astype(F32)
    kf_ref[...] = k_ref[...].astype(F32)
    vf_ref[...] = v_ref[...].astype(F32)

    low = lax.broadcasted_iota(jnp.int32, (blk, LANES), 1) < HEAD_DIM
    ones_bf = jnp.ones((2 * blk, LANES), BF16)

    row = lax.broadcasted_iota(jnp.int32, (2 * blk, 2 * blk), 0)
    col = lax.broadcasted_iota(jnp.int32, (2 * blk, 2 * blk), 1)
    delta = (row & (blk - 1)) - col + blk
    slope = jnp.where(row < blk, slopes_ref[HEAD_PAIR * hp], slopes_ref[HEAD_PAIR * hp + 1])
    for t, (window, dil) in enumerate(DILATED_BRANCHES):
        valid = (delta >= 0) & (delta <= window // dil)
        bias = jnp.where(valid, -slope * (delta * dil).astype(F32), NEG_BIG)
        bias_ref[2 * t] = bias
        bias_ref[2 * t + 1] = jnp.where(col >= blk, bias, NEG_BIG)

    def pair_rows(x):
        return jnp.where(low, jnp.broadcast_to(x[:blk], (blk, LANES)),
                         jnp.broadcast_to(x[blk:], (blk, LANES)))

    def pair_halves(x):
        return jnp.where(low, x[:blk], x[blk:])

    scale = jnp.asarray(HEAD_DIM ** -0.5, BF16)
    zero_bf = jnp.zeros((), BF16)
    n_blocks = seq // blk

    def block_rows(i):
        start = i * blk
        return pl.ds(start if isinstance(start, int) else pl.multiple_of(start, blk), blk)

    (_, d_lo), (_, d_mid), (_, d_hi) = DILATED_BRANCHES
    ratio = d_hi // d_mid
    assert d_lo == 1 and d_hi == d_mid * ratio and seq % (blk * d_hi) == 0
    mid_len = seq // d_mid

    def split_block(idx, dil):
        nb = n_blocks // dil
        assert nb & (nb - 1) == 0
        return idx >> (nb.bit_length() - 1), idx & (nb - 1)

    def mid_rows(idx):
        r, n = split_block(idx, d_mid)
        return pl.ds(r + n * (blk * d_mid), blk, stride=d_mid)

    def hi_rows_in_mid(idx):
        r, n = split_block(idx, d_hi)
        r_mid, sub = r & (d_mid - 1), r >> (d_mid.bit_length() - 1)
        return pl.ds(r_mid * mid_len + n * (blk * ratio) + sub, blk, stride=ratio)

    def gather_mid(idx, carry):
        rows = mid_rows(idx)
        for f_ref, mid_ref, d_ref in ((qf_ref, qm_ref, qd_ref), (kf_ref, km_ref, kd_ref), (vf_ref, vm_ref, vd_ref)):
            x = f_ref[rows, :]
            mid_ref[block_rows(idx), :] = x
            d_ref[block_rows(idx), :] = x.astype(BF16)
        return carry

    def gather_hi(idx, carry):
        rows = hi_rows_in_mid(idx)
        for mid_ref, d_ref in ((qm_ref, qd_ref), (km_ref, kd_ref), (vm_ref, vd_ref)):
            d_ref[block_rows(idx), :] = mid_ref[rows, :].astype(BF16)
        return carry

    branch_io = (
        ((q_ref, k_ref, v_ref), block_rows, None),
        ((qd_ref, kd_ref, vd_ref), block_rows, gather_mid),
        ((qd_ref, kd_ref, vd_ref), hi_rows_in_mid, gather_hi),
    )
    for t, (window, dil) in enumerate(DILATED_BRANCHES):
        nb = n_blocks // dil
        (q_src, k_src, v_src), out_rows_of, gather = branch_io[t]
        if gather is not None:
            lax.fori_loop(0, n_blocks, gather, 0, unroll=4)

        def body(idx, carry, t=t, nb=nb, out_rows_of=out_rows_of, q_src=q_src, k_src=k_src, v_src=v_src):
            first = (idx & (nb - 1)) == 0
            cur = block_rows(idx)
            prev = block_rows(jnp.where(first, idx, idx - 1))

            q = q_src[cur, :] * scale
            q2 = jnp.concatenate([jnp.where(low, q, zero_bf), jnp.where(low, zero_bf, q)], axis=0)
            kk = jnp.concatenate([k_src[prev, :], k_src[cur, :]], axis=0)
            vv = jnp.concatenate([jnp.concatenate([v_src[prev, :], v_src[cur, :]], axis=0), ones_bf], axis=1)

            s = _dot_nt(q2, kk) + bias_ref[2 * t + jnp.where(first, 1, 0)]
            m = jnp.max(s, axis=-1, keepdims=True)
            p = jnp.exp(s - m)
            pv = _dot(p.astype(BF16), vv)

            out_rows = out_rows_of(idx)
            acc_ref[t, out_rows, :] = pair_halves(pv[:, :LANES])
            l_ref[t, out_rows, :] = pair_halves(pv[:, LANES:])
            m_ref[t, out_rows, :] = pair_rows(m)
            return carry

        lax.fori_loop(0, n_blocks, body, 0, unroll=32)

    def merge(idx, carry):
        tok = mid_rows(idx)
        rows = (tok, block_rows(idx), block_rows(idx))
        ms = [m_ref[t, rows[t], :] for t in range(len(DILATED_BRANCHES))]
        m_all = functools.reduce(jnp.maximum, ms)
        ws = [jnp.exp(m_b - m_all) for m_b in ms]
        num = sum(w * acc_ref[t, rows[t], :] for t, w in enumerate(ws))
        den = sum(w * l_ref[t, rows[t], :] for t, w in enumerate(ws))
        of_ref[tok, :] = num / den
        return carry

    lax.fori_loop(0, n_blocks, merge, 0, unroll=2)
    o_ref[...] = of_ref[...].astype(o_ref.dtype)


def _attention(slopes, main3):
    b, seq, _ = main3.shape
    n_pairs = ATTN_WIDTH // LANES
    blk = ATTN_BLOCK

    def spec(offset):
        return pl.BlockSpec((None, seq, LANES), lambda i, j: (i, 0, offset + j))

    return pl.pallas_call(
        _attn_kernel,
        grid=(b, n_pairs),
        in_specs=[
            pl.BlockSpec(memory_space=pltpu.SMEM),
            spec(0), spec(n_pairs), spec(2 * n_pairs),
        ],
        out_specs=pl.BlockSpec((None, seq, LANES), lambda i, j: (i, 0, j)),
        out_shape=jax.ShapeDtypeStruct((b, seq, ATTN_WIDTH), BF16),
        scratch_shapes=[pltpu.VMEM((seq, LANES), F32)] * 7
        + [pltpu.VMEM((seq, LANES), BF16)] * 3
        + [pltpu.VMEM((len(DILATED_BRANCHES), seq, LANES), F32)] * 3
        + [pltpu.VMEM((2 * len(DILATED_BRANCHES), 2 * blk, 2 * blk), F32)],
        compiler_params=_params(("parallel", "parallel")),
        name="attention",
    )(slopes, main3, main3, main3)


def _ssd_kernel(z_ref, xs_ref, bc_ref, dt_ref, cw_ref, cb_ref, dtb_ref, alog_ref, dskip_ref, g_ref,
                y_ref, carry_ref, state_ref):
    tb = xs_ref.shape[0]
    q = SSD_CHUNK
    gw = SSD_WIDTH // SSD_GROUPS
    pairs_per_group = gw // LANES
    conv_cols = 512

    step = pl.program_id(1)
    slot = step & 1

    @pl.when(step == 0)
    def _():
        carry_ref[0] = jnp.zeros((q, CONV_CH), BF16)
        state_ref[...] = jnp.zeros(state_ref.shape, F32)

    carry_ref[1 - slot, :, 0:SSD_WIDTH] = xs_ref[tb - q:tb, :]
    carry_ref[1 - slot, :, SSD_WIDTH:CONV_CH] = bc_ref[tb - q:tb, :]

    low = lax.broadcasted_iota(jnp.int32, (q, LANES), 1) < SSD_HEAD_DIM
    ri = lax.broadcasted_iota(jnp.int32, (q, q), 0)
    ci = lax.broadcasted_iota(jnp.int32, (q, q), 1)
    tri = ri >= ci
    tri_bf = jnp.where(tri, 1.0, 0.0).astype(BF16)
    a_neg = -jnp.exp(alog_ref[...])

    sr = lax.broadcasted_iota(jnp.int32, (SSD_CONV * q, 2 * q), 0)
    sc = lax.broadcasted_iota(jnp.int32, (SSD_CONV * q, 2 * q), 1)
    src_row = (sr & (q - 1)) + (sr >> (q.bit_length() - 1)) + (q - (SSD_CONV - 1))
    shift = jnp.where(sc == src_row, 1.0, 0.0).astype(BF16)

    def cumsum_rows(v):
        hi = v.astype(BF16)
        r1 = v - hi.astype(F32)
        mid = r1.astype(BF16)
        lo = (r1 - mid.astype(F32)).astype(BF16)
        return _dot(tri_bf, hi) + _dot(tri_bf, mid) + _dot(tri_bf, lo)

    def lane_bcast(v, h):
        return jnp.broadcast_to(v[:, h:h + 1], (q, LANES))

    def chunk(c):
        r0 = c * q
        u_parts = []
        for j in range(CONV_CH // conv_cols):
            cols = slice(j * conv_cols, (j + 1) * conv_cols)
            src_ref, off = (xs_ref, 0) if j * conv_cols < SSD_WIDTH else (bc_ref, SSD_WIDTH)
            src_cols = slice(j * conv_cols - off, (j + 1) * conv_cols - off)
            prev = carry_ref[slot, :, cols] if c == 0 else src_ref[r0 - q:r0, src_cols]
            window = jnp.concatenate([prev, src_ref[r0:r0 + q, src_cols]], axis=0)
            taps = _dot(shift, window)
            uj = cb_ref[:, cols]
            for k in range(SSD_CONV):
                uj = uj + cw_ref[k:k + 1, cols] * taps[k * q:(k + 1) * q]
            u_parts.append(_silu(uj))
        u = jnp.concatenate(u_parts, axis=1)
        x = u[:, :SSD_WIDTH]
        dt_in = dt_ref[pl.ds(r0, q), :] + dtb_ref[...]
        dt = jnp.maximum(dt_in, 0.0) + jnp.log(1.0 + jnp.exp(-jnp.abs(dt_in)))
        acum = cumsum_rows(dt * a_neg) * LOG2_E
        acum_t = acum.T
        dt_t = dt.T

        y_parts = []
        for g in range(SSD_GROUPS):
            bg = u[:, SSD_WIDTH + g * SSD_STATE:SSD_WIDTH + (g + 1) * SSD_STATE]
            cg = u[:, SSD_WIDTH + (SSD_GROUPS + g) * SSD_STATE:SSD_WIDTH + (SSD_GROUPS + g + 1) * SSD_STATE]
            cg_bf = cg.astype(BF16)
            cb = _dot_nt(cg_bf, bg.astype(BF16))
            st = state_ref[g]
            y_off = _dot(cg_bf, st.astype(BF16))
            xd_parts, decay_parts = [], []
            for pp in range(pairs_per_group):
                p = g * pairs_per_group + pp
                h0 = HEAD_PAIR * p
                xp = x[:, p * LANES:(p + 1) * LANES]
                cols = [lane_bcast(acum, h0 + e) for e in range(HEAD_PAIR)]
                colc = jnp.where(low, cols[0], cols[1])
                ms = []
                for e in range(HEAD_PAIR):
                    h = h0 + e
                    seg = jnp.exp2(jnp.where(tri, cols[e] - acum_t[h:h + 1, :], NEG_BIG))
                    ms.append(cb * seg * dt_t[h:h + 1, :])
                m_cat = jnp.concatenate(ms, axis=1).astype(BF16)
                x_blk = jnp.concatenate([jnp.where(low, xp, 0.0), jnp.where(low, 0.0, xp)],
                                        axis=0).astype(BF16)
                y_diag = _dot(m_cat, x_blk)
                y_o = y_off[:, pp * LANES:(pp + 1) * LANES] * jnp.exp2(colc)
                y_parts.append(y_diag + y_o + dskip_ref[:, p * LANES:(p + 1) * LANES] * xp)
                last = colc[q - 1:q, :]
                dtc = jnp.where(low, lane_bcast(dt, h0), lane_bcast(dt, h0 + 1))
                xd_parts.append(xp * (dtc * jnp.exp2(last - colc)))
                decay_parts.append(jnp.exp2(last))
            xd = jnp.concatenate(xd_parts, axis=1).astype(BF16)
            decay = jnp.concatenate(decay_parts, axis=1)
            state_ref[g] = st * decay + _dot(bg.T.astype(BF16), xd)

        z = z_ref[pl.ds(r0, q), :].astype(F32)
        outs = []
        for g in range(SSD_GROUPS):
            yg = jnp.concatenate(y_parts[g * pairs_per_group:(g + 1) * pairs_per_group], axis=1)
            yg = yg * _silu(z[:, g * gw:(g + 1) * gw])
            outs.append(yg * _rms_scale(yg) * g_ref[:, g * gw:(g + 1) * gw])
        y_ref[pl.ds(r0, q), :] = jnp.concatenate(outs, axis=1).astype(y_ref.dtype)

    for c in range(tb // q):
        chunk(c)


def _ssd(main3, dt3, conv_w, conv_b, dt_bias, a_log, d_skip, norm_g, *, tb):
    b, seq, _ = main3.shape
    z_blk = (3 * ATTN_WIDTH) // SSD_WIDTH
    xs_blk = (3 * ATTN_WIDTH + SSD_WIDTH) // SSD_WIDTH
    bc_blk = (3 * ATTN_WIDTH + 2 * SSD_WIDTH) // BC_WIDTH

    def const(shape):
        return pl.BlockSpec(shape, lambda i, j: (0,) * len(shape))

    return pl.pallas_call(
        _ssd_kernel,
        grid=(b, seq // tb),
        in_specs=[
            pl.BlockSpec((None, tb, SSD_WIDTH), lambda i, j: (i, j, z_blk)),
            pl.BlockSpec((None, tb, SSD_WIDTH), lambda i, j: (i, j, xs_blk)),
            pl.BlockSpec((None, tb, BC_WIDTH), lambda i, j: (i, j, bc_blk)),
            pl.BlockSpec((None, tb, LANES), lambda i, j: (i, j, 0)),
            const((SSD_CONV, CONV_CH)), const((1, CONV_CH)),
            const((1, LANES)), const((1, LANES)),
            const((1, SSD_WIDTH)), const((1, SSD_WIDTH)),
        ],
        out_specs=pl.BlockSpec((None, tb, SSD_WIDTH), lambda i, j: (i, j, 0)),
        out_shape=jax.ShapeDtypeStruct((b, seq, SSD_WIDTH), BF16),
        scratch_shapes=[
            pltpu.VMEM((2, SSD_CHUNK, CONV_CH), BF16),
            pltpu.VMEM((SSD_GROUPS, SSD_STATE, SSD_WIDTH // SSD_GROUPS), F32),
        ],
        compiler_params=_params(("parallel", "arbitrary")),
        name="ssd",
    )(main3, main3, main3, dt3, conv_w, conv_b, dt_bias, a_log, d_skip, norm_g)


def _out_proj_kernel(a_ref, y_ref, x_ref, g_ref, w_ref, o_ref):
    a = a_ref[...].astype(F32)
    an = (a * _rms_scale(a) * g_ref[...]).astype(BF16)
    mixed = _dot(an, w_ref[0:ATTN_WIDTH, :]) + _dot(y_ref[...], w_ref[ATTN_WIDTH:, :])
    o_ref[...] = x_ref[...] + mixed


def _out_proj(attn, y, x, g, w_all, layer, *, tm):
    m, d = x.shape
    return pl.pallas_call(
        _out_proj_kernel,
        grid=(m // tm,),
        in_specs=[
            pl.BlockSpec((tm, ATTN_WIDTH), lambda i: (i, 0)),
            pl.BlockSpec((tm, SSD_WIDTH), lambda i: (i, 0)),
            pl.BlockSpec((tm, d), lambda i: (i, 0)),
            _resident((1, ATTN_WIDTH)),
            _resident(w_all.shape[1:], layer),
        ],
        out_specs=pl.BlockSpec((tm, d), lambda i: (i, 0)),
        out_shape=jax.ShapeDtypeStruct((m, d), F32),
        compiler_params=_params(("parallel",)),
        name="out_proj",
    )(attn, y, x, g, w_all)


def _mlp_kernel(x_ref, g_ref, w1_ref, w2_ref, gf_ref, o_ref, hn_ref, *, final_norm):
    f = pl.program_id(1)

    @pl.when(f == 0)
    def _():
        x = x_ref[...]
        hn_ref[...] = (x * _rms_scale(x) * g_ref[...]).astype(BF16)
        o_ref[...] = x

    h = jnp.maximum(_dot(hn_ref[...], w1_ref[...]), 0.0)
    o_ref[...] += _dot((h * h).astype(BF16), w2_ref[...])

    if final_norm:
        @pl.when(f == pl.num_programs(1) - 1)
        def _():
            o = o_ref[...]
            o_ref[...] = o * _rms_scale(o) * gf_ref[...]


def _mlp(x, g, w1_all, w2_all, layer, gf, *, tm, tf, final_norm):
    m, d = x.shape
    dff = w1_all.shape[2]
    return pl.pallas_call(
        functools.partial(_mlp_kernel, final_norm=final_norm),
        grid=(m // tm, dff // tf),
        in_specs=[
            pl.BlockSpec((tm, d), lambda i, j: (i, 0)),
            pl.BlockSpec((1, d), lambda i, j: (0, 0)),
            pl.BlockSpec((None, d, tf), lambda i, j: (layer, 0, j)),
            pl.BlockSpec((None, tf, d), lambda i, j: (layer, j, 0)),
            pl.BlockSpec((1, d), lambda i, j: (0, 0)),
        ],
        out_specs=pl.BlockSpec((tm, d), lambda i, j: (i, 0)),
        out_shape=jax.ShapeDtypeStruct((m, d), F32),
        scratch_shapes=[pltpu.VMEM((tm, d), BF16)],
        compiler_params=_params(("parallel", "arbitrary")),
        name="mlp",
    )(x, g, w1_all, w2_all, gf)


def _alibi_slopes():
    return jnp.asarray(2.0 ** (-8.0 * (np.arange(ATTN_HEADS) + 1) / ATTN_HEADS), dtype=F32)


def _pad_lanes(v):
    return jnp.pad(v, ((0, 0), (0, LANES - v.shape[1])))


def kernel(x, ln1_g, w_in, conv_w, conv_b, dt_bias, a_log, d_skip, attn_norm_g, ssd_norm_g, w_out,
           ln2_g, w_mlp_in, w_mlp_out, final_norm_g):
    b, seq, d = x.shape
    m = b * seq
    slopes = _alibi_slopes()
    xf = x.reshape(m, d)
    w_in_bf = w_in.astype(BF16)
    w_dt_bf = jnp.pad(w_in[:, :, MAIN_WIDTH:], ((0, 0), (0, 0), (0, LANES - (w_in.shape[2] - MAIN_WIDTH)))).astype(BF16)
    w_out_bf = w_out.astype(BF16)
    w_mlp_in_bf = w_mlp_in.astype(BF16)
    w_mlp_out_bf = w_mlp_out.astype(BF16)
    for l in range(DEPTH):
        main, dt_raw = _in_proj(xf, ln1_g[l][None, :], w_in_bf, w_dt_bf, l, tm=512, tn=512)
        main3 = main.reshape(b, seq, MAIN_WIDTH)
        attn = _attention(slopes, main3)
        y = _ssd(
            main3, dt_raw.reshape(b, seq, LANES), conv_w[l], conv_b[l][None, :],
            _pad_lanes(dt_bias[l][None, :]), _pad_lanes(a_log[l][None, :]),
            jnp.repeat(d_skip[l], SSD_HEAD_DIM)[None, :], ssd_norm_g[l][None, :], tb=256)
        x1 = _out_proj(attn.reshape(m, ATTN_WIDTH), y.reshape(m, SSD_WIDTH), xf,
                       attn_norm_g[l][None, :], w_out_bf, l, tm=512)
        xf = _mlp(x1, ln2_g[l][None, :], w_mlp_in_bf, w_mlp_out_bf, l,
                  final_norm_g[None, :], tm=1024, tf=512, final_norm=(l == DEPTH - 1))
    return xf.reshape(b, seq, d)
```

```python
import functools

import jax
import jax.numpy as jnp
import numpy as np
from jax import lax
from jax.experimental import pallas as pl
from jax.experimental.pallas import tpu as pltpu

D_MODEL = 2048
DEPTH = 2
HEAD_DIM = 64
ATTN_WIDTH = D_MODEL // 2
ATTN_HEADS = ATTN_WIDTH // HEAD_DIM
DILATED_BRANCHES = ((128, 1), (512, 4), (2048, 16))
ATTN_BLOCK = 128
SSD_WIDTH = D_MODEL // 2
SSD_HEAD_DIM = 64
SSD_HEADS = SSD_WIDTH // SSD_HEAD_DIM
SSD_GROUPS = 2
SSD_STATE = 128
SSD_CONV = 4
SSD_CHUNK = 128
BC_WIDTH = 2 * SSD_GROUPS * SSD_STATE
CONV_CH = SSD_WIDTH + BC_WIDTH
MAIN_WIDTH = 3 * ATTN_WIDTH + SSD_WIDTH + CONV_CH
D_FF = 4 * D_MODEL
NORM_EPS = 1e-5

LANES = 128
SUBLANES = 8
HEAD_PAIR = LANES // HEAD_DIM
NEG_BIG = -1e30
LOG2_E = 1.4426950408889634
Q_PRESCALE = HEAD_DIM ** -0.5 * LOG2_E
VMEM_LIMIT = 56 * 1024 * 1024

F32 = jnp.float32
BF16 = jnp.bfloat16


def _dot(a, b):
    return jnp.dot(a, b, preferred_element_type=F32)


def _dot_nt(a, b):
    return lax.dot_general(a, b, (((1,), (1,)), ((), ())), preferred_element_type=F32)


def _rms_scale(x):
    return lax.rsqrt(jnp.mean(x * x, axis=-1, keepdims=True) + NORM_EPS)


def _silu(x):
    return x * (1.0 / (1.0 + jnp.exp2(x * (-LOG2_E))))


def _params(semantics):
    return pltpu.CompilerParams(dimension_semantics=semantics, vmem_limit_bytes=VMEM_LIMIT)


def _in_proj_kernel(x_ref, g_ref, w_ref, wdt_ref, main_ref, dt_ref, hn_ref, *, tn):
    x = x_ref[...]
    hn_ref[...] = (x * _rms_scale(x) * g_ref[...]).astype(BF16)
    dt_ref[...] = _dot(hn_ref[...], wdt_ref[...])
    for j in range(main_ref.shape[1] // tn):
        cols = slice(j * tn, (j + 1) * tn)
        acc = _dot(hn_ref[...], w_ref[:, cols])
        if (j + 1) * tn <= ATTN_WIDTH:
            acc = acc * Q_PRESCALE
        main_ref[:, cols] = acc.astype(main_ref.dtype)


def _resident(shape, layer=None):
    if layer is None:
        return pl.BlockSpec(shape, lambda i: (0,) * len(shape), pipeline_mode=pl.Buffered(1))
    return pl.BlockSpec((None,) + tuple(shape), lambda i: (layer,) + (0,) * len(shape),
                        pipeline_mode=pl.Buffered(1))


def _in_proj(x, g, w_all, w_dt, layer, *, tm, tn):
    m, d = x.shape
    n = MAIN_WIDTH
    assert m % tm == 0 and n % tn == 0
    return pl.pallas_call(
        functools.partial(_in_proj_kernel, tn=tn),
        grid=(m // tm,),
        in_specs=[
            pl.BlockSpec((tm, d), lambda i: (i, 0)),
            _resident((1, d)),
            _resident(w_all.shape[1:], layer),
            _resident((d, LANES), layer),
        ],
        out_specs=[
            pl.BlockSpec((tm, n), lambda i: (i, 0)),
            pl.BlockSpec((tm, LANES), lambda i: (i, 0)),
        ],
        out_shape=[
            jax.ShapeDtypeStruct((m, n), BF16),
            jax.ShapeDtypeStruct((m, LANES), F32),
        ],
        scratch_shapes=[pltpu.VMEM((tm, d), BF16)],
        compiler_params=_params(("parallel",)),
        name="in_proj",
    )(x, g, w_all, w_dt)


def _attn_kernel(slopes_ref, q_ref, k_ref, v_ref, w1_ref, w2_ref, o_ref, w1o_ref, w2o_ref,
                 qf_ref, kf_ref, vf_ref, qm_ref, km_ref, vm_ref, of_ref, qd_ref, kd_ref, vd_ref,
                 m_ref, l_ref, acc_ref, bias_ref):
    seq = q_ref.shape[0]
    blk = ATTN_BLOCK
    hp = pl.program_id(1)

    w1o_ref[...] = w1_ref[...].astype(BF16)
    w2o_ref[...] = w2_ref[...].astype(BF16)

    qf_ref[...] = q_ref[...].astype(F32)
    kf_ref[...] = k_ref[...].astype(F32)
    vf_ref[...] = v_ref[...].astype(F32)

    low = lax.broadcasted_iota(jnp.int32, (blk, LANES), 1) < HEAD_DIM
    ones_bf = jnp.ones((2 * blk, LANES), BF16)

    row = lax.broadcasted_iota(jnp.int32, (2 * blk, 2 * blk), 0)
    col = lax.broadcasted_iota(jnp.int32, (2 * blk, 2 * blk), 1)
    delta = (row & (blk - 1)) - col + blk
    slope = jnp.where(row < blk, slopes_ref[HEAD_PAIR * hp], slopes_ref[HEAD_PAIR * hp + 1])
    for t, (window, dil) in enumerate(DILATED_BRANCHES):
        valid = (delta >= 0) & (delta <= window // dil)
        bias = jnp.where(valid, -slope * (delta * dil).astype(F32) * LOG2_E, NEG_BIG)
        bias_ref[2 * t] = bias
        bias_ref[2 * t + 1] = jnp.where(col >= blk, bias, NEG_BIG)

    def pair_rows(x):
        return jnp.where(low, jnp.broadcast_to(x[:blk], (blk, LANES)),
                         jnp.broadcast_to(x[blk:], (blk, LANES)))

    def pair_halves(x):
        return jnp.where(low, x[:blk], x[blk:])

    zero_bf = jnp.zeros((), BF16)
    n_blocks = seq // blk

    def block_rows(i):
        start = i * blk
        return pl.ds(start if isinstance(start, int) else pl.multiple_of(start, blk), blk)

    (_, d_lo), (_, d_mid), (_, d_hi) = DILATED_BRANCHES
    ratio = d_hi // d_mid
    assert d_lo == 1 and d_hi == d_mid * ratio and seq % (blk * d_hi) == 0
    mid_len = seq // d_mid

    def split_block(idx, dil):
        nb = n_blocks // dil
        assert nb & (nb - 1) == 0
        return idx >> (nb.bit_length() - 1), idx & (nb - 1)

    def mid_rows(idx):
        r, n = split_block(idx, d_mid)
        return pl.ds(r + n * (blk * d_mid), blk, stride=d_mid)

    def hi_rows_in_mid(idx):
        r, n = split_block(idx, d_hi)
        r_mid, sub = r & (d_mid - 1), r >> (d_mid.bit_length() - 1)
        return pl.ds(r_mid * mid_len + n * (blk * ratio) + sub, blk, stride=ratio)

    def gather_mid(idx, carry):
        rows = mid_rows(idx)
        for f_ref, mid_ref, d_ref in ((qf_ref, qm_ref, qd_ref), (kf_ref, km_ref, kd_ref), (vf_ref, vm_ref, vd_ref)):
            x = f_ref[rows, :]
            mid_ref[block_rows(idx), :] = x
            d_ref[block_rows(idx), :] = x.astype(BF16)
        return carry

    def gather_hi(idx, carry):
        rows = hi_rows_in_mid(idx)
        for mid_ref, d_ref in ((qm_ref, qd_ref), (km_ref, kd_ref), (vm_ref, vd_ref)):
            d_ref[block_rows(idx), :] = mid_ref[rows, :].astype(BF16)
        return carry

    branch_io = (
        ((q_ref, k_ref, v_ref), block_rows, None),
        ((qd_ref, kd_ref, vd_ref), block_rows, gather_mid),
        ((qd_ref, kd_ref, vd_ref), hi_rows_in_mid, gather_hi),
    )
    for t, (window, dil) in enumerate(DILATED_BRANCHES):
        nb = n_blocks // dil
        (q_src, k_src, v_src), out_rows_of, gather = branch_io[t]
        if gather is not None:
            lax.fori_loop(0, n_blocks, gather, 0, unroll=4)

        def body(idx, carry, t=t, nb=nb, out_rows_of=out_rows_of, q_src=q_src, k_src=k_src, v_src=v_src):
            first = (idx & (nb - 1)) == 0
            cur = block_rows(idx)
            prev = block_rows(jnp.where(first, idx, idx - 1))

            q = q_src[cur, :]
            q2 = jnp.concatenate([jnp.where(low, q, zero_bf), jnp.where(low, zero_bf, q)], axis=0)
            kk = jnp.concatenate([k_src[prev, :], k_src[cur, :]], axis=0)
            vv = jnp.concatenate([jnp.concatenate([v_src[prev, :], v_src[cur, :]], axis=0), ones_bf], axis=1)

            s = _dot_nt(q2, kk) + bias_ref[2 * t + jnp.where(first, 1, 0)]
            m = jnp.max(s, axis=-1, keepdims=True)
            p = jnp.exp2(s - m)
            pv = _dot(p.astype(BF16), vv)

            out_rows = out_rows_of(idx)
            acc_ref[t, out_rows, :] = pair_halves(pv[:, :LANES])
            l_ref[t, out_rows, :] = pair_halves(pv[:, LANES:])
            m_ref[t, out_rows, :] = pair_rows(m)
            return carry

        lax.fori_loop(0, n_blocks, body, 0, unroll=32)

    def merge(idx, carry):
        tok = mid_rows(idx)
        rows = (tok, block_rows(idx), block_rows(idx))
        ms = [m_ref[t, rows[t], :] for t in range(len(DILATED_BRANCHES))]
        m_all = functools.reduce(jnp.maximum, ms)
        ws = [jnp.exp2(m_b - m_all) for m_b in ms]
        num = sum(w * acc_ref[t, rows[t], :] for t, w in enumerate(ws))
        den = sum(w * l_ref[t, rows[t], :] for t, w in enumerate(ws))
        of_ref[tok, :] = num / den
        return carry

    lax.fori_loop(0, n_blocks, merge, 0, unroll=2)
    o_ref[...] = of_ref[...].astype(o_ref.dtype)


def _attention(slopes, main3, w1_all, w2_all, layer):
    b, seq, _ = main3.shape
    n_pairs = ATTN_WIDTH // LANES
    blk = ATTN_BLOCK
    n_steps = b * n_pairs
    w1_shape, w2_shape = w1_all.shape[1:], w2_all.shape[1:]
    r1, r2 = w1_shape[0] // n_steps, w2_shape[0] // n_steps
    assert r1 * n_steps == w1_shape[0] and r2 * n_steps == w2_shape[0]

    def spec(offset):
        return pl.BlockSpec((None, seq, LANES), lambda i, j: (i, 0, offset + j))

    return pl.pallas_call(
        _attn_kernel,
        grid=(b, n_pairs),
        in_specs=[
            pl.BlockSpec(memory_space=pltpu.SMEM),
            spec(0), spec(n_pairs), spec(2 * n_pairs),
            pl.BlockSpec((None, r1, w1_shape[1]), lambda i, j: (layer, i * n_pairs + j, 0)),
            pl.BlockSpec((None, r2, w2_shape[1]), lambda i, j: (layer, i * n_pairs + j, 0)),
        ],
        out_specs=[
            pl.BlockSpec((None, seq, LANES), lambda i, j: (i, 0, j)),
            pl.BlockSpec((r1, w1_shape[1]), lambda i, j: (i * n_pairs + j, 0)),
            pl.BlockSpec((r2, w2_shape[1]), lambda i, j: (i * n_pairs + j, 0)),
        ],
        out_shape=[
            jax.ShapeDtypeStruct((b, seq, ATTN_WIDTH), BF16),
            jax.ShapeDtypeStruct(w1_shape, BF16),
            jax.ShapeDtypeStruct(w2_shape, BF16),
        ],
        scratch_shapes=[pltpu.VMEM((seq, LANES), F32)] * 7
        + [pltpu.VMEM((seq, LANES), BF16)] * 3
        + [pltpu.VMEM((len(DILATED_BRANCHES), seq, LANES), F32)] * 3
        + [pltpu.VMEM((2 * len(DILATED_BRANCHES), 2 * blk, 2 * blk), F32)],
        compiler_params=_params(("parallel", "parallel")),
        name="attention",
    )(slopes, main3, main3, main3, w1_all, w2_all)


def _ssd_kernel(z_ref, xs_ref, bc_ref, dt_ref, cw_ref, cb_ref, dtb_ref, alog_ref, dskip_ref, g_ref,
                y_ref, carry_ref, state_ref):
    tb = xs_ref.shape[0]
    q = SSD_CHUNK
    gw = SSD_WIDTH // SSD_GROUPS
    pairs_per_group = gw // LANES
    conv_cols = 512

    step = pl.program_id(1)
    slot = step & 1

    @pl.when(step == 0)
    def _():
        carry_ref[0] = jnp.zeros((q, CONV_CH), BF16)
        state_ref[...] = jnp.zeros(state_ref.shape, F32)

    carry_ref[1 - slot, :, 0:SSD_WIDTH] = xs_ref[tb - q:tb, :]
    carry_ref[1 - slot, :, SSD_WIDTH:CONV_CH] = bc_ref[tb - q:tb, :]

    low = lax.broadcasted_iota(jnp.int32, (q, LANES), 1) < SSD_HEAD_DIM
    ri = lax.broadcasted_iota(jnp.int32, (q, q), 0)
    ci = lax.broadcasted_iota(jnp.int32, (q, q), 1)
    tri = ri >= ci
    tri_bf = jnp.where(tri, 1.0, 0.0).astype(BF16)
    a_neg = -jnp.exp(alog_ref[...])

    sr = lax.broadcasted_iota(jnp.int32, (SSD_CONV * q, 2 * q), 0)
    sc = lax.broadcasted_iota(jnp.int32, (SSD_CONV * q, 2 * q), 1)
    src_row = (sr & (q - 1)) + (sr >> (q.bit_length() - 1)) + (q - (SSD_CONV - 1))
    shift = jnp.where(sc == src_row, 1.0, 0.0).astype(BF16)

    def cumsum_rows(v):
        hi = v.astype(BF16)
        r1 = v - hi.astype(F32)
        mid = r1.astype(BF16)
        lo = (r1 - mid.astype(F32)).astype(BF16)
        return _dot(tri_bf, hi) + _dot(tri_bf, mid) + _dot(tri_bf, lo)

    def lane_bcast(v, h):
        return jnp.broadcast_to(v[:, h:h + 1], (q, LANES))

    def chunk(c):
        r0 = c * q
        u_parts = []
        for j in range(CONV_CH // conv_cols):
            cols = slice(j * conv_cols, (j + 1) * conv_cols)
            src_ref, off = (xs_ref, 0) if j * conv_cols < SSD_WIDTH else (bc_ref, SSD_WIDTH)
            src_cols = slice(j * conv_cols - off, (j + 1) * conv_cols - off)
            prev = carry_ref[slot, :, cols] if c == 0 else src_ref[r0 - q:r0, src_cols]
            window = jnp.concatenate([prev, src_ref[r0:r0 + q, src_cols]], axis=0)
            taps = _dot(shift, window)
            uj = cb_ref[:, cols]
            for k in range(SSD_CONV):
                uj = uj + cw_ref[k:k + 1, cols] * taps[k * q:(k + 1) * q]
            u_parts.append(_silu(uj))
        u = jnp.concatenate(u_parts, axis=1)
        x = u[:, :SSD_WIDTH]
        dt_in = dt_ref[pl.ds(r0, q), :] + dtb_ref[...]
        dt = jnp.maximum(dt_in, 0.0) + jnp.log(1.0 + jnp.exp(-jnp.abs(dt_in)))
        acum = cumsum_rows(dt * a_neg) * LOG2_E
        acum_t = acum.T

        y_parts = []
        for g in range(SSD_GROUPS):
            bg = u[:, SSD_WIDTH + g * SSD_STATE:SSD_WIDTH + (g + 1) * SSD_STATE]
            cg = u[:, SSD_WIDTH + (SSD_GROUPS + g) * SSD_STATE:SSD_WIDTH + (SSD_GROUPS + g + 1) * SSD_STATE]
            cg_bf = cg.astype(BF16)
            cb = _dot_nt(cg_bf, bg.astype(BF16))
            st = state_ref[g]
            y_off = _dot(cg_bf, st.astype(BF16))
            xd_parts, decay_parts = [], []
            for pp in range(pairs_per_group):
                p = g * pairs_per_group + pp
                h0 = HEAD_PAIR * p
                xp = x[:, p * LANES:(p + 1) * LANES]
                cols = [lane_bcast(acum, h0 + e) for e in range(HEAD_PAIR)]
                colc = jnp.where(low, cols[0], cols[1])
                ms = []
                for e in range(HEAD_PAIR):
                    h = h0 + e
                    seg = jnp.exp2(jnp.where(tri, cols[e] - acum_t[h:h + 1, :], NEG_BIG))
                    ms.append(cb * seg)
                m_cat = jnp.concatenate(ms, axis=1).astype(BF16)
                dtc = jnp.where(low, lane_bcast(dt, h0), lane_bcast(dt, h0 + 1))
                xdt = xp * dtc
                x_blk = jnp.concatenate([jnp.where(low, xdt, 0.0), jnp.where(low, 0.0, xdt)],
                                        axis=0).astype(BF16)
                y_diag = _dot(m_cat, x_blk)
                y_o = y_off[:, pp * LANES:(pp + 1) * LANES] * jnp.exp2(colc)
                y_parts.append(y_diag + y_o + dskip_ref[:, p * LANES:(p + 1) * LANES] * xp)
                last = colc[q - 1:q, :]
                xd_parts.append(xdt * jnp.exp2(last - colc))
                decay_parts.append(jnp.exp2(last))
            xd = jnp.concatenate(xd_parts, axis=1).astype(BF16)
            decay = jnp.concatenate(decay_parts, axis=1)
            state_ref[g] = st * decay + _dot(bg.T.astype(BF16), xd)

        z = z_ref[pl.ds(r0, q), :].astype(F32)
        outs = []
        for g in range(SSD_GROUPS):
            yg = jnp.concatenate(y_parts[g * pairs_per_group:(g + 1) * pairs_per_group], axis=1)
            yg = yg * _silu(z[:, g * gw:(g + 1) * gw])
            outs.append(yg * _rms_scale(yg) * g_ref[:, g * gw:(g + 1) * gw])
        y_ref[pl.ds(r0, q), :] = jnp.concatenate(outs, axis=1).astype(y_ref.dtype)

    for c in range(tb // q):
        chunk(c)


def _ssd(main3, dt3, conv_w, conv_b, dt_bias, a_log, d_skip, norm_g, *, tb):
    b, seq, _ = main3.shape
    z_blk = (3 * ATTN_WIDTH) // SSD_WIDTH
    xs_blk = (3 * ATTN_WIDTH + SSD_WIDTH) // SSD_WIDTH
    bc_blk = (3 * ATTN_WIDTH + 2 * SSD_WIDTH) // BC_WIDTH

    def const(shape):
        return pl.BlockSpec(shape, lambda i, j: (0,) * len(shape))

    return pl.pallas_call(
        _ssd_kernel,
        grid=(b, seq // tb),
        in_specs=[
            pl.BlockSpec((None, tb, SSD_WIDTH), lambda i, j: (i, j, z_blk)),
            pl.BlockSpec((None, tb, SSD_WIDTH), lambda i, j: (i, j, xs_blk)),
            pl.BlockSpec((None, tb, BC_WIDTH), lambda i, j: (i, j, bc_blk)),
            pl.BlockSpec((None, tb, LANES), lambda i, j: (i, j, 0)),
            const((SSD_CONV, CONV_CH)), const((1, CONV_CH)),
            const((1, LANES)), const((1, LANES)),
            const((1, SSD_WIDTH)), const((1, SSD_WIDTH)),
        ],
        out_specs=pl.BlockSpec((None, tb, SSD_WIDTH), lambda i, j: (i, j, 0)),
        out_shape=jax.ShapeDtypeStruct((b, seq, SSD_WIDTH), BF16),
        scratch_shapes=[
            pltpu.VMEM((2, SSD_CHUNK, CONV_CH), BF16),
            pltpu.VMEM((SSD_GROUPS, SSD_STATE, SSD_WIDTH // SSD_GROUPS), F32),
        ],
        compiler_params=_params(("parallel", "arbitrary")),
        name="ssd",
    )(main3, main3, main3, dt3, conv_w, conv_b, dt_bias, a_log, d_skip, norm_g)


def _out_proj_kernel(a_ref, y_ref, x_ref, g_ref, w_ref, o_ref):
    a = a_ref[...].astype(F32)
    an = (a * _rms_scale(a) * g_ref[...]).astype(BF16)
    mixed = _dot(an, w_ref[0:ATTN_WIDTH, :]) + _dot(y_ref[...], w_ref[ATTN_WIDTH:, :])
    o_ref[...] = x_ref[...] + mixed


def _out_proj(attn, y, x, g, w_all, layer, *, tm):
    m, d = x.shape
    return pl.pallas_call(
        _out_proj_kernel,
        grid=(m // tm,),
        in_specs=[
            pl.BlockSpec((tm, ATTN_WIDTH), lambda i: (i, 0)),
            pl.BlockSpec((tm, SSD_WIDTH), lambda i: (i, 0)),
            pl.BlockSpec((tm, d), lambda i: (i, 0)),
            _resident((1, ATTN_WIDTH)),
            _resident(w_all.shape[1:], layer),
        ],
        out_specs=pl.BlockSpec((tm, d), lambda i: (i, 0)),
        out_shape=jax.ShapeDtypeStruct((m, d), F32),
        compiler_params=_params(("parallel",)),
        name="out_proj",
    )(attn, y, x, g, w_all)


def _mlp_kernel(x_ref, g_ref, w1_ref, w2_ref, gf_ref, o_ref, hn_ref, *, final_norm):
    f = pl.program_id(1)

    @pl.when(f == 0)
    def _():
        x = x_ref[...]
        hn_ref[...] = (x * _rms_scale(x) * g_ref[...]).astype(BF16)
        o_ref[...] = x

    h = jnp.maximum(_dot(hn_ref[...], w1_ref[...]), 0.0)
    o_ref[...] += _dot((h * h).astype(BF16), w2_ref[...])

    if final_norm:
        @pl.when(f == pl.num_programs(1) - 1)
        def _():
            o = o_ref[...]
            o_ref[...] = o * _rms_scale(o) * gf_ref[...]


def _mlp(x, g, w1, w2, gf, *, tm, tf, final_norm):
    m, d = x.shape
    dff = w1.shape[1]
    return pl.pallas_call(
        functools.partial(_mlp_kernel, final_norm=final_norm),
        grid=(m // tm, dff // tf),
        in_specs=[
            pl.BlockSpec((tm, d), lambda i, j: (i, 0), pipeline_mode=pl.Buffered(1)),
            pl.BlockSpec((1, d), lambda i, j: (0, 0)),
            pl.BlockSpec((d, tf), lambda i, j: (0, j)),
            pl.BlockSpec((tf, d), lambda i, j: (j, 0)),
            pl.BlockSpec((1, d), lambda i, j: (0, 0)),
        ],
        out_specs=pl.BlockSpec((tm, d), lambda i, j: (i, 0)),
        out_shape=jax.ShapeDtypeStruct((m, d), F32),
        scratch_shapes=[pltpu.VMEM((tm, d), BF16)],
        compiler_params=_params(("parallel", "arbitrary")),
        name="mlp",
    )(x, g, w1, w2, gf)


def _alibi_slopes():
    return jnp.asarray(2.0 ** (-8.0 * (np.arange(ATTN_HEADS) + 1) / ATTN_HEADS), dtype=F32)


def _pad_lanes(v):
    return jnp.pad(v, ((0, 0), (0, LANES - v.shape[1])))


def kernel(x, ln1_g, w_in, conv_w, conv_b, dt_bias, a_log, d_skip, attn_norm_g, ssd_norm_g, w_out,
           ln2_g, w_mlp_in, w_mlp_out, final_norm_g):
    b, seq, d = x.shape
    m = b * seq
    slopes = _alibi_slopes()
    xf = x.reshape(m, d)
    w_in_bf = w_in.astype(BF16)
    w_dt_bf = jnp.pad(w_in[:, :, MAIN_WIDTH:], ((0, 0), (0, 0), (0, LANES - (w_in.shape[2] - MAIN_WIDTH)))).astype(BF16)
    w_out_bf = w_out.astype(BF16)
    for l in range(DEPTH):
        main, dt_raw = _in_proj(xf, ln1_g[l][None, :], w_in_bf, w_dt_bf, l, tm=512, tn=512)
        main3 = main.reshape(b, seq, MAIN_WIDTH)
        attn, w1_bf, w2_bf = _attention(slopes, main3, w_mlp_in, w_mlp_out, l)
        y = _ssd(
            main3, dt_raw.reshape(b, seq, LANES), conv_w[l], conv_b[l][None, :],
            _pad_lanes(dt_bias[l][None, :]), _pad_lanes(a_log[l][None, :]),
            jnp.repeat(d_skip[l], SSD_HEAD_DIM)[None, :], ssd_norm_g[l][None, :], tb=512)
        x1 = _out_proj(attn.reshape(m, ATTN_WIDTH), y.reshape(m, SSD_WIDTH), xf,
                       attn_norm_g[l][None, :], w_out_bf, l, tm=512)
        xf = _mlp(x1, ln2_g[l][None, :], w1_bf, w2_bf,
                  final_norm_g[None, :], tm=1024, tf=1024, final_norm=(l == DEPTH - 1))
    return xf.reshape(b, seq, d)
```

```python
import functools

import jax
import jax.numpy as jnp
import numpy as np
from jax import lax
from jax.experimental import pallas as pl
from jax.experimental.pallas import tpu as pltpu

D_MODEL = 2048
DEPTH = 2
HEAD_DIM = 64
ATTN_WIDTH = D_MODEL // 2
ATTN_HEADS = ATTN_WIDTH // HEAD_DIM
DILATED_BRANCHES = ((128, 1), (512, 4), (2048, 16))
ATTN_BLOCK = 128
SSD_WIDTH = D_MODEL // 2
SSD_HEAD_DIM = 64
SSD_HEADS = SSD_WIDTH // SSD_HEAD_DIM
SSD_GROUPS = 2
SSD_STATE = 128
SSD_CONV = 4
SSD_CHUNK = 128
BC_WIDTH = 2 * SSD_GROUPS * SSD_STATE
CONV_CH = SSD_WIDTH + BC_WIDTH
MAIN_WIDTH = 3 * ATTN_WIDTH + SSD_WIDTH + CONV_CH
D_FF = 4 * D_MODEL
NORM_EPS = 1e-5

LANES = 128
SUBLANES = 8
HEAD_PAIR = LANES // HEAD_DIM
NEG_BIG = -1e30
LOG2_E = 1.4426950408889634
Q_PRESCALE = HEAD_DIM ** -0.5 * LOG2_E
VMEM_LIMIT = 56 * 1024 * 1024

F32 = jnp.float32
BF16 = jnp.bfloat16


def _dot(a, b):
    return jnp.dot(a, b, preferred_element_type=F32)


def _dot_nt(a, b):
    return lax.dot_general(a, b, (((1,), (1,)), ((), ())), preferred_element_type=F32)


def _rms_scale(x):
    return lax.rsqrt(jnp.mean(x * x, axis=-1, keepdims=True) + NORM_EPS)


def _silu(x):
    return x * (1.0 / (1.0 + jnp.exp2(x * (-LOG2_E))))


def _params(semantics):
    return pltpu.CompilerParams(dimension_semantics=semantics, vmem_limit_bytes=VMEM_LIMIT)


def _in_proj_kernel(x_ref, g_ref, w_ref, wdt_ref, main_ref, dt_ref, hn_ref, *, tn):
    x = x_ref[...]
    hn_ref[...] = (x * _rms_scale(x) * g_ref[...]).astype(BF16)
    dt_ref[...] = _dot(hn_ref[...], wdt_ref[...])
    for j in range(main_ref.shape[1] // tn):
        cols = slice(j * tn, (j + 1) * tn)
        acc = _dot(hn_ref[...], w_ref[:, cols])
        if (j + 1) * tn <= ATTN_WIDTH:
            acc = acc * Q_PRESCALE
        main_ref[:, cols] = acc.astype(main_ref.dtype)


def _resident(shape, layer=None):
    if layer is None:
        return pl.BlockSpec(shape, lambda i: (0,) * len(shape), pipeline_mode=pl.Buffered(1))
    return pl.BlockSpec((None,) + tuple(shape), lambda i: (layer,) + (0,) * len(shape),
                        pipeline_mode=pl.Buffered(1))


def _in_proj(x, g, w_all, w_dt, layer, *, tm, tn):
    m, d = x.shape
    n = MAIN_WIDTH
    assert m % tm == 0 and n % tn == 0
    return pl.pallas_call(
        functools.partial(_in_proj_kernel, tn=tn),
        grid=(m // tm,),
        in_specs=[
            pl.BlockSpec((tm, d), lambda i: (i, 0)),
            _resident((1, d)),
            _resident(w_all.shape[1:], layer),
            _resident((d, LANES), layer),
        ],
        out_specs=[
            pl.BlockSpec((tm, n), lambda i: (i, 0)),
            pl.BlockSpec((tm, LANES), lambda i: (i, 0)),
        ],
        out_shape=[
            jax.ShapeDtypeStruct((m, n), BF16),
            jax.ShapeDtypeStruct((m, LANES), F32),
        ],
        scratch_shapes=[pltpu.VMEM((tm, d), BF16)],
        compiler_params=_params(("parallel",)),
        name="in_proj",
    )(x, g, w_all, w_dt)


def _attn_kernel(slopes_ref, q_ref, k_ref, v_ref, w1_ref, w2_ref, o_ref, w1o_ref, w2o_ref,
                 qf_ref, kf_ref, vf_ref, qm_ref, km_ref, vm_ref, of_ref, qd_ref, kd_ref, vd_ref,
                 m_ref, l_ref, acc_ref, bias_ref):
    seq = q_ref.shape[0]
    blk = ATTN_BLOCK
    hp = pl.program_id(1)

    w1o_ref[...] = w1_ref[...].astype(BF16)
    w2o_ref[...] = w2_ref[...].astype(BF16)

    qf_ref[...] = q_ref[...].astype(F32)
    kf_ref[...] = k_ref[...].astype(F32)
    vf_ref[...] = v_ref[...].astype(F32)

    low = lax.broadcasted_iota(jnp.int32, (blk, LANES), 1) < HEAD_DIM
    ones_bf = jnp.ones((2 * blk, LANES), BF16)

    row = lax.broadcasted_iota(jnp.int32, (2 * blk, 2 * blk), 0)
    col = lax.broadcasted_iota(jnp.int32, (2 * blk, 2 * blk), 1)
    delta = (row & (blk - 1)) - col + blk
    slope = jnp.where(row < blk, slopes_ref[HEAD_PAIR * hp], slopes_ref[HEAD_PAIR * hp + 1])
    for t, (window, dil) in enumerate(DILATED_BRANCHES):
        valid = (delta >= 0) & (delta <= window // dil)
        bias = jnp.where(valid, -slope * (delta * dil).astype(F32) * LOG2_E, NEG_BIG)
        bias_ref[2 * t] = bias
        bias_ref[2 * t + 1] = jnp.where(col >= blk, bias, NEG_BIG)

    def pair_rows(x):
        return jnp.where(low, jnp.broadcast_to(x[:blk], (blk, LANES)),
                         jnp.broadcast_to(x[blk:], (blk, LANES)))

    def pair_halves(x):
        return jnp.where(low, x[:blk], x[blk:])

    zero_bf = jnp.zeros((), BF16)
    n_blocks = seq // blk

    def block_rows(i):
        start = i * blk
        return pl.ds(start if isinstance(start, int) else pl.multiple_of(start, blk), blk)

    (_, d_lo), (_, d_mid), (_, d_hi) = DILATED_BRANCHES
    ratio = d_hi // d_mid
    assert d_lo == 1 and d_hi == d_mid * ratio and seq % (blk * d_hi) == 0
    mid_len = seq // d_mid

    def split_block(idx, dil):
        nb = n_blocks // dil
        assert nb & (nb - 1) == 0
        return idx >> (nb.bit_length() - 1), idx & (nb - 1)

    def mid_rows(idx):
        r, n = split_block(idx, d_mid)
        return pl.ds(r + n * (blk * d_mid), blk, stride=d_mid)

    def hi_rows_in_mid(idx):
        r, n = split_block(idx, d_hi)
        r_mid, sub = r & (d_mid - 1), r >> (d_mid.bit_length() - 1)
        return pl.ds(r_mid * mid_len + n * (blk * ratio) + sub, blk, stride=ratio)

    def gather_mid(idx, carry):
        rows = mid_rows(idx)
        for f_ref, mid_ref, d_ref in ((qf_ref, qm_ref, qd_ref), (kf_ref, km_ref, kd_ref), (vf_ref, vm_ref, vd_ref)):
            x = f_ref[rows, :]
            mid_ref[block_rows(idx), :] = x
            d_ref[block_rows(idx), :] = x.astype(BF16)
        return carry

    def gather_hi(idx, carry):
        rows = hi_rows_in_mid(idx)
        for mid_ref, d_ref in ((qm_ref, qd_ref), (km_ref, kd_ref), (vm_ref, vd_ref)):
            d_ref[block_rows(idx), :] = mid_ref[rows, :].astype(BF16)
        return carry

    branch_io = (
        ((q_ref, k_ref, v_ref), block_rows, None),
        ((qd_ref, kd_ref, vd_ref), block_rows, gather_mid),
        ((qd_ref, kd_ref, vd_ref), hi_rows_in_mid, gather_hi),
    )
    for t, (window, dil) in enumerate(DILATED_BRANCHES):
        nb = n_blocks // dil
        (q_src, k_src, v_src), out_rows_of, gather = branch_io[t]
        if gather is not None:
            lax.fori_loop(0, n_blocks, gather, 0, unroll=4)

        def body(idx, carry, t=t, nb=nb, out_rows_of=out_rows_of, q_src=q_src, k_src=k_src, v_src=v_src):
            first = (idx & (nb - 1)) == 0
            cur = block_rows(idx)
            prev = block_rows(jnp.where(first, idx, idx - 1))

            q = q_src[cur, :]
            q2 = jnp.concatenate([jnp.where(low, q, zero_bf), jnp.where(low, zero_bf, q)], axis=0)
            kk = jnp.concatenate([k_src[prev, :], k_src[cur, :]], axis=0)
            vv = jnp.concatenate([jnp.concatenate([v_src[prev, :], v_src[cur, :]], axis=0), ones_bf], axis=1)

            s = _dot_nt(q2, kk) + bias_ref[2 * t + jnp.where(first, 1, 0)]
            m = jnp.max(s, axis=-1, keepdims=True)
            p = jnp.exp2(s - m)
            pv = _dot(p.astype(BF16), vv)

            out_rows = out_rows_of(idx)
            acc_ref[t, out_rows, :] = pair_halves(pv[:, :LANES])
            l_ref[t, out_rows, :] = pair_halves(pv[:, LANES:])
            m_ref[t, out_rows, :] = pair_rows(m)
            return carry

        lax.fori_loop(0, n_blocks, body, 0, unroll=32)

    def merge(idx, carry):
        tok = mid_rows(idx)
        rows = (tok, block_rows(idx), block_rows(idx))
        ms = [m_ref[t, rows[t], :] for t in range(len(DILATED_BRANCHES))]
        m_all = functools.reduce(jnp.maximum, ms)
        ws = [jnp.exp2(m_b - m_all) for m_b in ms]
        num = sum(w * acc_ref[t, rows[t], :] for t, w in enumerate(ws))
        den = sum(w * l_ref[t, rows[t], :] for t, w in enumerate(ws))
        of_ref[tok, :] = num / den
        return carry

    lax.fori_loop(0, n_blocks, merge, 0, unroll=2)
    o_ref[...] = of_ref[...].astype(o_ref.dtype)


def _attention(slopes, main3, w1_all, w2_all, layer):
    b, seq, _ = main3.shape
    n_pairs = ATTN_WIDTH // LANES
    blk = ATTN_BLOCK
    n_steps = b * n_pairs
    w1_shape, w2_shape = w1_all.shape[1:], w2_all.shape[1:]
    r1, r2 = w1_shape[0] // n_steps, w2_shape[0] // n_steps
    assert r1 * n_steps == w1_shape[0] and r2 * n_steps == w2_shape[0]

    def spec(offset):
        return pl.BlockSpec((None, seq, LANES), lambda i, j: (i, 0, offset + j))

    return pl.pallas_call(
        _attn_kernel,
        grid=(b, n_pairs),
        in_specs=[
            pl.BlockSpec(memory_space=pltpu.SMEM),
            spec(0), spec(n_pairs), spec(2 * n_pairs),
            pl.BlockSpec((None, r1, w1_shape[1]), lambda i, j: (layer, i * n_pairs + j, 0)),
            pl.BlockSpec((None, r2, w2_shape[1]), lambda i, j: (layer, i * n_pairs + j, 0)),
        ],
        out_specs=[
            pl.BlockSpec((None, seq, LANES), lambda i, j: (i, 0, j)),
            pl.BlockSpec((r1, w1_shape[1]), lambda i, j: (i * n_pairs + j, 0)),
            pl.BlockSpec((r2, w2_shape[1]), lambda i, j: (i * n_pairs + j, 0)),
        ],
        out_shape=[
            jax.ShapeDtypeStruct((b, seq, ATTN_WIDTH), BF16),
            jax.ShapeDtypeStruct(w1_shape, BF16),
            jax.ShapeDtypeStruct(w2_shape, BF16),
        ],
        scratch_shapes=[pltpu.VMEM((seq, LANES), F32)] * 7
        + [pltpu.VMEM((seq, LANES), BF16)] * 3
        + [pltpu.VMEM((len(DILATED_BRANCHES), seq, LANES), F32)] * 3
        + [pltpu.VMEM((2 * len(DILATED_BRANCHES), 2 * blk, 2 * blk), F32)],
        compiler_params=_params(("parallel", "parallel")),
        name="attention",
    )(slopes, main3, main3, main3, w1_all, w2_all)


def _ssd_kernel(z_ref, xs_ref, bc_ref, dt_ref, cw_ref, cb_ref, dtb_ref, alog_ref, dskip_ref, g_ref,
                y_ref, carry_ref, state_ref):
    tb = xs_ref.shape[0]
    q = SSD_CHUNK
    gw = SSD_WIDTH // SSD_GROUPS
    pairs_per_group = gw // LANES
    conv_cols = 512

    step = pl.program_id(1)
    slot = step & 1

    @pl.when(step == 0)
    def _():
        carry_ref[0] = jnp.zeros((q, CONV_CH), BF16)
        state_ref[...] = jnp.zeros(state_ref.shape, F32)

    carry_ref[1 - slot, :, 0:SSD_WIDTH] = xs_ref[tb - q:tb, :]
    carry_ref[1 - slot, :, SSD_WIDTH:CONV_CH] = bc_ref[tb - q:tb, :]

    low = lax.broadcasted_iota(jnp.int32, (q, LANES), 1) < SSD_HEAD_DIM
    ri = lax.broadcasted_iota(jnp.int32, (q, q), 0)
    ci = lax.broadcasted_iota(jnp.int32, (q, q), 1)
    tri = ri >= ci
    tri_bf = jnp.where(tri, 1.0, 0.0).astype(BF16)
    a_neg = -jnp.exp(alog_ref[...])

    sr = lax.broadcasted_iota(jnp.int32, (SSD_CONV * q, 2 * q), 0)
    sc = lax.broadcasted_iota(jnp.int32, (SSD_CONV * q, 2 * q), 1)
    src_row = (sr & (q - 1)) + (sr >> (q.bit_length() - 1)) + (q - (SSD_CONV - 1))
    shift = jnp.where(sc == src_row, 1.0, 0.0).astype(BF16)

    def cumsum_rows(v):
        hi = v.astype(BF16)
        r1 = v - hi.astype(F32)
        mid = r1.astype(BF16)
        lo = (r1 - mid.astype(F32)).astype(BF16)
        return _dot(tri_bf, hi) + _dot(tri_bf, mid) + _dot(tri_bf, lo)

    def lane_bcast(v, h):
        return jnp.broadcast_to(v[:, h:h + 1], (q, LANES))

    def chunk(c):
        r0 = c * q
        u_parts = []
        for j in range(CONV_CH // conv_cols):
            cols = slice(j * conv_cols, (j + 1) * conv_cols)
            src_ref, off = (xs_ref, 0) if j * conv_cols < SSD_WIDTH else (bc_ref, SSD_WIDTH)
            src_cols = slice(j * conv_cols - off, (j + 1) * conv_cols - off)
            prev = carry_ref[slot, :, cols] if c == 0 else src_ref[r0 - q:r0, src_cols]
            window = jnp.concatenate([prev, src_ref[r0:r0 + q, src_cols]], axis=0)
            taps = _dot(shift, window)
            uj = cb_ref[:, cols]
            for k in range(SSD_CONV):
                uj = uj + cw_ref[k:k + 1, cols] * taps[k * q:(k + 1) * q]
            u_parts.append(_silu(uj))
        u = jnp.concatenate(u_parts, axis=1)
        x = u[:, :SSD_WIDTH]
        dt_in = dt_ref[pl.ds(r0, q), :] + dtb_ref[...]
        dt = jnp.maximum(dt_in, 0.0) + jnp.log(1.0 + jnp.exp(-jnp.abs(dt_in)))
        acum = cumsum_rows(dt * a_neg) * LOG2_E
        acum_t = acum.T

        y_parts = []
        for g in range(SSD_GROUPS):
            bg = u[:, SSD_WIDTH + g * SSD_STATE:SSD_WIDTH + (g + 1) * SSD_STATE]
            cg = u[:, SSD_WIDTH + (SSD_GROUPS + g) * SSD_STATE:SSD_WIDTH + (SSD_GROUPS + g + 1) * SSD_STATE]
            cg_bf = cg.astype(BF16)
            cb = _dot_nt(cg_bf, bg.astype(BF16))
            st = state_ref[g]
            y_off = _dot(cg_bf, st.astype(BF16))
            xd_parts, decay_parts = [], []
            for pp in range(pairs_per_group):
                p = g * pairs_per_group + pp
                h0 = HEAD_PAIR * p
                xp = x[:, p * LANES:(p + 1) * LANES]
                cols = [lane_bcast(acum, h0 + e) for e in range(HEAD_PAIR)]
                colc = jnp.where(low, cols[0], cols[1])
                ms = []
                for e in range(HEAD_PAIR):
                    h = h0 + e
                    seg = jnp.exp2(jnp.where(tri, cols[e] - acum_t[h:h + 1, :], NEG_BIG))
                    ms.append(cb * seg)
                m_cat = jnp.concatenate(ms, axis=1).astype(BF16)
                dtc = jnp.where(low, lane_bcast(dt, h0), lane_bcast(dt, h0 + 1))
                xdt = xp * dtc
                x_blk = jnp.concatenate([jnp.where(low, xdt, 0.0), jnp.where(low, 0.0, xdt)],
                                        axis=0).astype(BF16)
                y_diag = _dot(m_cat, x_blk)
                y_o = y_off[:, pp * LANES:(pp + 1) * LANES] * jnp.exp2(colc)
                y_parts.append(y_diag + y_o + dskip_ref[:, p * LANES:(p + 1) * LANES] * xp)
                last = colc[q - 1:q, :]
                xd_parts.append(xdt * jnp.exp2(last - colc))
                decay_parts.append(jnp.exp2(last))
            xd = jnp.concatenate(xd_parts, axis=1).astype(BF16)
            decay = jnp.concatenate(decay_parts, axis=1)
            state_ref[g] = st * decay + _dot(bg.T.astype(BF16), xd)

        z = z_ref[pl.ds(r0, q), :].astype(F32)
        outs = []
        for g in range(SSD_GROUPS):
            yg = jnp.concatenate(y_parts[g * pairs_per_group:(g + 1) * pairs_per_group], axis=1)
            yg = yg * _silu(z[:, g * gw:(g + 1) * gw])
            outs.append(yg * _rms_scale(yg) * g_ref[:, g * gw:(g + 1) * gw])
        y_ref[pl.ds(r0, q), :] = jnp.concatenate(outs, axis=1).astype(y_ref.dtype)

    for c in range(tb // q):
        chunk(c)


def _ssd(main3, dt3, conv_w, conv_b, dt_bias, a_log, d_skip, norm_g, *, tb):
    b, seq, _ = main3.shape
    z_blk = (3 * ATTN_WIDTH) // SSD_WIDTH
    xs_blk = (3 * ATTN_WIDTH + SSD_WIDTH) // SSD_WIDTH
    bc_blk = (3 * ATTN_WIDTH + 2 * SSD_WIDTH) // BC_WIDTH

    def const(shape):
        return pl.BlockSpec(shape, lambda i, j: (0,) * len(shape))

    return pl.pallas_call(
        _ssd_kernel,
        grid=(b, seq // tb),
        in_specs=[
            pl.BlockSpec((None, tb, SSD_WIDTH), lambda i, j: (i, j, z_blk)),
            pl.BlockSpec((None, tb, SSD_WIDTH), lambda i, j: (i, j, xs_blk)),
            pl.BlockSpec((None, tb, BC_WIDTH), lambda i, j: (i, j, bc_blk)),
            pl.BlockSpec((None, tb, LANES), lambda i, j: (i, j, 0)),
            const((SSD_CONV, CONV_CH)), const((1, CONV_CH)),
            const((1, LANES)), const((1, LANES)),
            const((1, SSD_WIDTH)), const((1, SSD_WIDTH)),
        ],
        out_specs=pl.BlockSpec((None, tb, SSD_WIDTH), lambda i, j: (i, j, 0)),
        out_shape=jax.ShapeDtypeStruct((b, seq, SSD_WIDTH), BF16),
        scratch_shapes=[
            pltpu.VMEM((2, SSD_CHUNK, CONV_CH), BF16),
            pltpu.VMEM((SSD_GROUPS, SSD_STATE, SSD_WIDTH // SSD_GROUPS), F32),
        ],
        compiler_params=_params(("parallel", "arbitrary")),
        name="ssd",
    )(main3, main3, main3, dt3, conv_w, conv_b, dt_bias, a_log, d_skip, norm_g)


def _out_proj_kernel(a_ref, y_ref, x_ref, g_ref, w_ref, o_ref):
    a = a_ref[...].astype(F32)
    an = (a * _rms_scale(a) * g_ref[...]).astype(BF16)
    mixed = _dot(an, w_ref[0:ATTN_WIDTH, :]) + _dot(y_ref[...], w_ref[ATTN_WIDTH:, :])
    o_ref[...] = x_ref[...] + mixed


def _out_proj(attn, y, x, g, w_all, layer, *, tm):
    m, d = x.shape
    return pl.pallas_call(
        _out_proj_kernel,
        grid=(m // tm,),
        in_specs=[
            pl.BlockSpec((tm, ATTN_WIDTH), lambda i: (i, 0)),
            pl.BlockSpec((tm, SSD_WIDTH), lambda i: (i, 0)),
            pl.BlockSpec((tm, d), lambda i: (i, 0)),
            _resident((1, ATTN_WIDTH)),
            _resident(w_all.shape[1:], layer),
        ],
        out_specs=pl.BlockSpec((tm, d), lambda i: (i, 0)),
        out_shape=jax.ShapeDtypeStruct((m, d), F32),
        compiler_params=_params(("parallel",)),
        name="out_proj",
    )(attn, y, x, g, w_all)


def _mlp_kernel(x_ref, g_ref, w1_ref, w2_ref, gf_ref, o_ref, hn_ref, *, final_norm):
    f = pl.program_id(1)

    @pl.when(f == 0)
    def _():
        x = x_ref[...]
        hn_ref[...] = (x * _rms_scale(x) * g_ref[...]).astype(BF16)
        o_ref[...] = x

    h = jnp.maximum(_dot(hn_ref[...], w1_ref[...]), 0.0)
    o_ref[...] += _dot((h * h).astype(BF16), w2_ref[...])

    if final_norm:
        @pl.when(f == pl.num_programs(1) - 1)
        def _():
            o = o_ref[...]
            o_ref[...] = o * _rms_scale(o) * gf_ref[...]


def _mlp(x, g, w1, w2, gf, *, tm, tf, final_norm):
    m, d = x.shape
    dff = w1.shape[1]
    return pl.pallas_call(
        functools.partial(_mlp_kernel, final_norm=final_norm),
        grid=(m // tm, dff // tf),
        in_specs=[
            pl.BlockSpec((tm, d), lambda i, j: (i, 0)),
            pl.BlockSpec((1, d), lambda i, j: (0, 0)),
            pl.BlockSpec((d, tf), lambda i, j: (0, j)),
            pl.BlockSpec((tf, d), lambda i, j: (j, 0)),
            pl.BlockSpec((1, d), lambda i, j: (0, 0)),
        ],
        out_specs=pl.BlockSpec((tm, d), lambda i, j: (i, 0)),
        out_shape=jax.ShapeDtypeStruct((m, d), F32),
        scratch_shapes=[pltpu.VMEM((tm, d), BF16)],
        compiler_params=_params(("parallel", "arbitrary")),
        name="mlp",
    )(x, g, w1, w2, gf)


def _alibi_slopes():
    return jnp.asarray(2.0 ** (-8.0 * (np.arange(ATTN_HEADS) + 1) / ATTN_HEADS), dtype=F32)


def _pad_lanes(v):
    return jnp.pad(v, ((0, 0), (0, LANES - v.shape[1])))


def kernel(x, ln1_g, w_in, conv_w, conv_b, dt_bias, a_log, d_skip, attn_norm_g, ssd_norm_g, w_out,
           ln2_g, w_mlp_in, w_mlp_out, final_norm_g):
    b, seq, d = x.shape
    m = b * seq
    slopes = _alibi_slopes()
    xf = x.reshape(m, d)
    w_in_bf = w_in.astype(BF16)
    w_dt_bf = jnp.pad(w_in[:, :, MAIN_WIDTH:], ((0, 0), (0, 0), (0, LANES - (w_in.shape[2] - MAIN_WIDTH)))).astype(BF16)
    w_out_bf = w_out.astype(BF16)
    for l in range(DEPTH):
        main, dt_raw = _in_proj(xf, ln1_g[l][None, :], w_in_bf, w_dt_bf, l, tm=512, tn=512)
        main3 = main.reshape(b, seq, MAIN_WIDTH)
        attn, w1_bf, w2_bf = _attention(slopes, main3, w_mlp_in, w_mlp_out, l)
        y = _ssd(
            main3, dt_raw.reshape(b, seq, LANES), conv_w[l], conv_b[l][None, :],
            _pad_lanes(dt_bias[l][None, :]), _pad_lanes(a_log[l][None, :]),
            jnp.repeat(d_skip[l], SSD_HEAD_DIM)[None, :], ssd_norm_g[l][None, :], tb=512)
        x1 = _out_proj(attn.reshape(m, ATTN_WIDTH), y.reshape(m, SSD_WIDTH), xf,
                       attn_norm_g[l][None, :], w_out_bf, l, tm=512)
        xf = _mlp(x1, ln2_g[l][None, :], w1_bf, w2_bf,
                  final_norm_g[None, :], tm=1024, tf=512, final_norm=(l == DEPTH - 1))
    return xf.reshape(b, seq, d)
```

```python
import functools

import jax
import jax.numpy as jnp
import numpy as np
from jax import lax
from jax.experimental import pallas as pl
from jax.experimental.pallas import tpu as pltpu

D_MODEL = 2048
DEPTH = 2
HEAD_DIM = 64
ATTN_WIDTH = D_MODEL // 2
ATTN_HEADS = ATTN_WIDTH // HEAD_DIM
DILATED_BRANCHES = ((128, 1), (512, 4), (2048, 16))
ATTN_BLOCK = 128
BLOCKS_PER_GROUP = 16
SSD_WIDTH = D_MODEL // 2
SSD_HEAD_DIM = 64
SSD_HEADS = SSD_WIDTH // SSD_HEAD_DIM
SSD_GROUPS = 2
SSD_STATE = 128
SSD_CONV = 4
SSD_CHUNK = 128
BC_WIDTH = 2 * SSD_GROUPS * SSD_STATE
CONV_CH = SSD_WIDTH + BC_WIDTH
MAIN_WIDTH = 3 * ATTN_WIDTH + SSD_WIDTH + CONV_CH
D_FF = 4 * D_MODEL
NORM_EPS = 1e-5

LANES = 128
SUBLANES = 8
HEAD_PAIR = LANES // HEAD_DIM
NEG_BIG = -1e30
LOG2_E = 1.4426950408889634
Q_PRESCALE = HEAD_DIM ** -0.5 * LOG2_E
VMEM_LIMIT = 56 * 1024 * 1024

F32 = jnp.float32
BF16 = jnp.bfloat16


def _dot(a, b):
    return jnp.dot(a, b, preferred_element_type=F32)


def _dot_nt(a, b):
    return lax.dot_general(a, b, (((1,), (1,)), ((), ())), preferred_element_type=F32)


def _rms_scale(x):
    return lax.rsqrt(jnp.mean(x * x, axis=-1, keepdims=True) + NORM_EPS)


def _silu(x):
    return x * (1.0 / (1.0 + jnp.exp2(x * (-LOG2_E))))


def _params(semantics):
    return pltpu.CompilerParams(dimension_semantics=semantics, vmem_limit_bytes=VMEM_LIMIT)


def _in_proj_kernel(x_ref, g_ref, w_ref, wdt_ref, main_ref, dt_ref, hn_ref, *, tn):
    x = x_ref[...]
    hn_ref[...] = (x * _rms_scale(x) * g_ref[...]).astype(BF16)
    dt_ref[...] = _dot(hn_ref[...], wdt_ref[...])
    for j in range(main_ref.shape[1] // tn):
        cols = slice(j * tn, (j + 1) * tn)
        acc = _dot(hn_ref[...], w_ref[:, cols])
        if (j + 1) * tn <= ATTN_WIDTH:
            acc = acc * Q_PRESCALE
        main_ref[:, cols] = acc.astype(main_ref.dtype)


def _resident(shape, layer=None):
    if layer is None:
        return pl.BlockSpec(shape, lambda i: (0,) * len(shape), pipeline_mode=pl.Buffered(1))
    return pl.BlockSpec((None,) + tuple(shape), lambda i: (layer,) + (0,) * len(shape),
                        pipeline_mode=pl.Buffered(1))


def _in_proj(x, g, w_all, w_dt, layer, *, tm, tn):
    m, d = x.shape
    n = MAIN_WIDTH
    assert m % tm == 0 and n % tn == 0
    return pl.pallas_call(
        functools.partial(_in_proj_kernel, tn=tn),
        grid=(m // tm,),
        in_specs=[
            pl.BlockSpec((tm, d), lambda i: (i, 0)),
            _resident((1, d)),
            _resident(w_all.shape[1:], layer),
            _resident((d, LANES), layer),
        ],
        out_specs=[
            pl.BlockSpec((tm, n), lambda i: (i, 0)),
            pl.BlockSpec((tm, LANES), lambda i: (i, 0)),
        ],
        out_shape=[
            jax.ShapeDtypeStruct((m, n), BF16),
            jax.ShapeDtypeStruct((m, LANES), F32),
        ],
        scratch_shapes=[pltpu.VMEM((tm, d), BF16)],
        compiler_params=_params(("parallel",)),
        name="in_proj",
    )(x, g, w_all, w_dt)


def _attn_kernel(slopes_ref, q_ref, k_ref, v_ref, w1_ref, w2_ref, o_ref, w1o_ref, w2o_ref,
                 qf_ref, kf_ref, vf_ref, qm_ref, km_ref, vm_ref, of_ref,
                 qd_ref, kd_ref, vd_ref, qh_ref, kh_ref, vh_ref, st0_ref, st1_ref, bias_ref):
    seq = q_ref.shape[0]
    blk = ATTN_BLOCK
    hp = pl.program_id(1)

    w1o_ref[...] = w1_ref[...].astype(BF16)
    w2o_ref[...] = w2_ref[...].astype(BF16)

    qf_ref[...] = q_ref[...].astype(F32)
    kf_ref[...] = k_ref[...].astype(F32)
    vf_ref[...] = v_ref[...].astype(F32)

    low = lax.broadcasted_iota(jnp.int32, (blk, LANES), 1) < HEAD_DIM
    ones_bf = jnp.ones((2 * blk, LANES), BF16)

    row = lax.broadcasted_iota(jnp.int32, (2 * blk, 2 * blk), 0)
    col = lax.broadcasted_iota(jnp.int32, (2 * blk, 2 * blk), 1)
    delta = (row & (blk - 1)) - col + blk
    slope = jnp.where(row < blk, slopes_ref[HEAD_PAIR * hp], slopes_ref[HEAD_PAIR * hp + 1])
    for t, (window, dil) in enumerate(DILATED_BRANCHES):
        valid = (delta >= 0) & (delta <= window // dil)
        bias = jnp.where(valid, -slope * (delta * dil).astype(F32) * LOG2_E, NEG_BIG)
        bias_ref[2 * t] = bias
        bias_ref[2 * t + 1] = jnp.where(col >= blk, bias, NEG_BIG)

    def pair_rows(x):
        return jnp.where(low, jnp.broadcast_to(x[:blk], (blk, LANES)),
                         jnp.broadcast_to(x[blk:], (blk, LANES)))

    def pair_halves(x):
        return jnp.where(low, x[:blk], x[blk:])

    zero_bf = jnp.zeros((), BF16)
    n_blocks = seq // blk

    def block_rows(i):
        start = i * blk
        return pl.ds(start if isinstance(start, int) else pl.multiple_of(start, blk), blk)

    (_, d_lo), (_, d_mid), (_, d_hi) = DILATED_BRANCHES
    ratio = d_hi // d_mid
    assert d_lo == 1 and d_hi == d_mid * ratio and seq % (blk * d_hi) == 0
    mid_len = seq // d_mid

    def log2(n):
        assert n & (n - 1) == 0
        return n.bit_length() - 1

    def split_block(idx, dil):
        nb = n_blocks // dil
        return idx >> log2(nb), idx & (nb - 1)

    def token_rows(idx, dil):
        r, n = split_block(idx, dil)
        return pl.ds(r + n * (blk * dil), blk, stride=dil) if dil > 1 else block_rows(idx)

    def hi_rows_in_mid(idx):
        r, n = split_block(idx, d_hi)
        return pl.ds((r & (d_mid - 1)) * mid_len + n * (blk * ratio) + (r >> log2(d_mid)), blk, stride=ratio)

    def gather_mid(idx):
        rows = token_rows(idx, d_mid)
        for f_ref, mid_ref, d_ref in ((qf_ref, qm_ref, qd_ref), (kf_ref, km_ref, kd_ref), (vf_ref, vm_ref, vd_ref)):
            x = f_ref[rows, :]
            mid_ref[block_rows(idx), :] = x
            d_ref[block_rows(idx), :] = x.astype(BF16)

    def gather_hi(idx):
        rows = hi_rows_in_mid(idx)
        for mid_ref, h_ref in ((qm_ref, qh_ref), (km_ref, kh_ref), (vm_ref, vh_ref)):
            h_ref[block_rows(idx), :] = mid_ref[rows, :].astype(BF16)

    def block_softmax(t, idx, q_src, k_src, v_src):
        first = split_block(idx, DILATED_BRANCHES[t][1])[1] == 0
        cur = block_rows(idx)
        prev = block_rows(idx if first else idx - 1)
        q = q_src[cur, :]
        q2 = jnp.concatenate([jnp.where(low, q, zero_bf), jnp.where(low, zero_bf, q)], axis=0)
        kk = jnp.concatenate([k_src[prev, :], k_src[cur, :]], axis=0)
        vv = jnp.concatenate([jnp.concatenate([v_src[prev, :], v_src[cur, :]], axis=0), ones_bf], axis=1)
        s = _dot_nt(q2, kk) + bias_ref[2 * t + int(first)]
        m = jnp.max(s, axis=-1, keepdims=True)
        p = jnp.exp2(s - m)
        pv = _dot(p.astype(BF16), vv)
        return pair_rows(m), pair_halves(pv[:, LANES:]), pair_halves(pv[:, :LANES])

    def combine(a, b):
        (ma, la, acca), (mb, lb, accb) = a, b
        m = jnp.maximum(ma, mb)
        wa, wb = jnp.exp2(ma - m), jnp.exp2(mb - m)
        return m, wa * la + wb * lb, wa * acca + wb * accb

    def load_state(st_ref, rows):
        return tuple(st_ref[i, rows, :] for i in range(3))

    def store_state(st_ref, rows, state):
        for i, x in enumerate(state):
            st_ref[i, rows, :] = x

    def branch0(idx):
        store_state(st0_ref, block_rows(idx), block_softmax(0, idx, q_ref, k_ref, v_ref))
        gather_mid(idx)

    def branch1(idx):
        mine = block_softmax(1, idx, qd_ref, kd_ref, vd_ref)
        store_state(st1_ref, block_rows(idx), combine(load_state(st0_ref, token_rows(idx, d_mid)), mine))
        gather_hi(idx)

    def branch2(idx):
        mine = block_softmax(2, idx, qh_ref, kh_ref, vh_ref)
        _, l_all, acc_all = combine(load_state(st1_ref, hi_rows_in_mid(idx)), mine)
        of_ref[token_rows(idx, d_hi), :] = acc_all / l_all

    guards = (pl.program_id(0) >= 0, pl.program_id(1) >= 0, pl.program_id(0) + pl.program_id(1) >= 0)
    for branch, guard in zip((branch0, branch1, branch2), guards):
        @pl.when(guard)
        def _(branch=branch):
            for idx in range(n_blocks):
                branch(idx)
    o_ref[...] = of_ref[...].astype(o_ref.dtype)


def _attention(slopes, main3, w1_all, w2_all, layer):
    b, seq, _ = main3.shape
    n_pairs = ATTN_WIDTH // LANES
    blk = ATTN_BLOCK
    n_steps = b * n_pairs
    w1_shape, w2_shape = w1_all.shape[1:], w2_all.shape[1:]
    r1, r2 = w1_shape[0] // n_steps, w2_shape[0] // n_steps
    assert r1 * n_steps == w1_shape[0] and r2 * n_steps == w2_shape[0]

    def spec(offset):
        return pl.BlockSpec((None, seq, LANES), lambda i, j: (i, 0, offset + j))

    return pl.pallas_call(
        _attn_kernel,
        grid=(b, n_pairs),
        in_specs=[
            pl.BlockSpec(memory_space=pltpu.SMEM),
            spec(0), spec(n_pairs), spec(2 * n_pairs),
            pl.BlockSpec((None, r1, w1_shape[1]), lambda i, j: (layer, i * n_pairs + j, 0)),
            pl.BlockSpec((None, r2, w2_shape[1]), lambda i, j: (layer, i * n_pairs + j, 0)),
        ],
        out_specs=[
            pl.BlockSpec((None, seq, LANES), lambda i, j: (i, 0, j)),
            pl.BlockSpec((r1, w1_shape[1]), lambda i, j: (i * n_pairs + j, 0)),
            pl.BlockSpec((r2, w2_shape[1]), lambda i, j: (i * n_pairs + j, 0)),
        ],
        out_shape=[
            jax.ShapeDtypeStruct((b, seq, ATTN_WIDTH), BF16),
            jax.ShapeDtypeStruct(w1_shape, BF16),
            jax.ShapeDtypeStruct(w2_shape, BF16),
        ],
        scratch_shapes=[pltpu.VMEM((seq, LANES), F32)] * 7
        + [pltpu.VMEM((seq, LANES), BF16)] * 6
        + [pltpu.VMEM((3, seq, LANES), F32)] * 2
        + [pltpu.VMEM((2 * len(DILATED_BRANCHES), 2 * blk, 2 * blk), F32)],
        compiler_params=_params(("parallel", "parallel")),
        name="attention",
    )(slopes, main3, main3, main3, w1_all, w2_all)


def _ssd_kernel(z_ref, xs_ref, bc_ref, dt_ref, cw_ref, cb_ref, dtb_ref, alog_ref, dskip_ref, g_ref,
                y_ref, carry_ref, state_ref):
    tb = xs_ref.shape[0]
    q = SSD_CHUNK
    gw = SSD_WIDTH // SSD_GROUPS
    pairs_per_group = gw // LANES
    conv_cols = 512

    step = pl.program_id(1)
    slot = step & 1

    @pl.when(step == 0)
    def _():
        carry_ref[0] = jnp.zeros((q, CONV_CH), BF16)
        state_ref[...] = jnp.zeros(state_ref.shape, F32)

    carry_ref[1 - slot, :, 0:SSD_WIDTH] = xs_ref[tb - q:tb, :]
    carry_ref[1 - slot, :, SSD_WIDTH:CONV_CH] = bc_ref[tb - q:tb, :]

    low = lax.broadcasted_iota(jnp.int32, (q, LANES), 1) < SSD_HEAD_DIM
    ri = lax.broadcasted_iota(jnp.int32, (q, q), 0)
    ci = lax.broadcasted_iota(jnp.int32, (q, q), 1)
    tri = ri >= ci
    tri_bf = jnp.where(tri, 1.0, 0.0).astype(BF16)
    a_neg = -jnp.exp(alog_ref[...])

    sr = lax.broadcasted_iota(jnp.int32, (SSD_CONV * q, 2 * q), 0)
    sc = lax.broadcasted_iota(jnp.int32, (SSD_CONV * q, 2 * q), 1)
    src_row = (sr & (q - 1)) + (sr >> (q.bit_length() - 1)) + (q - (SSD_CONV - 1))
    shift = jnp.where(sc == src_row, 1.0, 0.0).astype(BF16)

    def cumsum_rows(v):
        hi = v.astype(BF16)
        r1 = v - hi.astype(F32)
        mid = r1.astype(BF16)
        lo = (r1 - mid.astype(F32)).astype(BF16)
        return _dot(tri_bf, hi) + _dot(tri_bf, mid) + _dot(tri_bf, lo)

    def lane_bcast(v, h):
        return jnp.broadcast_to(v[:, h:h + 1], (q, LANES))

    def chunk(c):
        r0 = c * q
        u_parts = []
        for j in range(CONV_CH // conv_cols):
            cols = slice(j * conv_cols, (j + 1) * conv_cols)
            src_ref, off = (xs_ref, 0) if j * conv_cols < SSD_WIDTH else (bc_ref, SSD_WIDTH)
            src_cols = slice(j * conv_cols - off, (j + 1) * conv_cols - off)
            prev = carry_ref[slot, :, cols] if c == 0 else src_ref[r0 - q:r0, src_cols]
            window = jnp.concatenate([prev, src_ref[r0:r0 + q, src_cols]], axis=0)
            taps = _dot(shift, window)
            uj = cb_ref[:, cols]
            for k in range(SSD_CONV):
                uj = uj + cw_ref[k:k + 1, cols] * taps[k * q:(k + 1) * q]
            u_parts.append(_silu(uj))
        u = jnp.concatenate(u_parts, axis=1)
        x = u[:, :SSD_WIDTH]
        dt_in = dt_ref[pl.ds(r0, q), :] + dtb_ref[...]
        dt = jnp.maximum(dt_in, 0.0) + jnp.log(1.0 + jnp.exp(-jnp.abs(dt_in)))
        acum = cumsum_rows(dt * a_neg) * LOG2_E
        acum_t = acum.T

        y_parts = []
        for g in range(SSD_GROUPS):
            bg = u[:, SSD_WIDTH + g * SSD_STATE:SSD_WIDTH + (g + 1) * SSD_STATE]
            cg = u[:, SSD_WIDTH + (SSD_GROUPS + g) * SSD_STATE:SSD_WIDTH + (SSD_GROUPS + g + 1) * SSD_STATE]
            cg_bf = cg.astype(BF16)
            cb = _dot_nt(cg_bf, bg.astype(BF16))
            st = state_ref[g]
            y_off = _dot(cg_bf, st.astype(BF16))
            xd_parts, decay_parts = [], []
            for pp in range(pairs_per_group):
                p = g * pairs_per_group + pp
                h0 = HEAD_PAIR * p
                xp = x[:, p * LANES:(p + 1) * LANES]
                cols = [lane_bcast(acum, h0 + e) for e in range(HEAD_PAIR)]
                colc = jnp.where(low, cols[0], cols[1])
                ms = []
                for e in range(HEAD_PAIR):
                    h = h0 + e
                    seg = jnp.exp2(jnp.where(tri, cols[e] - acum_t[h:h + 1, :], NEG_BIG))
                    ms.append(cb * seg)
                m_cat = jnp.concatenate(ms, axis=1).astype(BF16)
                dtc = jnp.where(low, lane_bcast(dt, h0), lane_bcast(dt, h0 + 1))
                xdt = xp * dtc
                x_blk = jnp.concatenate([jnp.where(low, xdt, 0.0), jnp.where(low, 0.0, xdt)],
                                        axis=0).astype(BF16)
                y_diag = _dot(m_cat, x_blk)
                y_o = y_off[:, pp * LANES:(pp + 1) * LANES] * jnp.exp2(colc)
                y_parts.append(y_diag + y_o + dskip_ref[:, p * LANES:(p + 1) * LANES] * xp)
                last = colc[q - 1:q, :]
                xd_parts.append(xdt * jnp.exp2(last - colc))
                decay_parts.append(jnp.exp2(last))
            xd = jnp.concatenate(xd_parts, axis=1).astype(BF16)
            decay = jnp.concatenate(decay_parts, axis=1)
            state_ref[g] = st * decay + _dot(bg.T.astype(BF16), xd)

        z = z_ref[pl.ds(r0, q), :].astype(F32)
        outs = []
        for g in range(SSD_GROUPS):
            yg = jnp.concatenate(y_parts[g * pairs_per_group:(g + 1) * pairs_per_group], axis=1)
            yg = yg * _silu(z[:, g * gw:(g + 1) * gw])
            outs.append(yg * _rms_scale(yg) * g_ref[:, g * gw:(g + 1) * gw])
        y_ref[pl.ds(r0, q), :] = jnp.concatenate(outs, axis=1).astype(y_ref.dtype)

    for c in range(tb // q):
        chunk(c)


def _ssd(main3, dt3, conv_w, conv_b, dt_bias, a_log, d_skip, norm_g, *, tb):
    b, seq, _ = main3.shape
    z_blk = (3 * ATTN_WIDTH) // SSD_WIDTH
    xs_blk = (3 * ATTN_WIDTH + SSD_WIDTH) // SSD_WIDTH
    bc_blk = (3 * ATTN_WIDTH + 2 * SSD_WIDTH) // BC_WIDTH

    def const(shape):
        return pl.BlockSpec(shape, lambda i, j: (0,) * len(shape))

    return pl.pallas_call(
        _ssd_kernel,
        grid=(b, seq // tb),
        in_specs=[
            pl.BlockSpec((None, tb, SSD_WIDTH), lambda i, j: (i, j, z_blk)),
            pl.BlockSpec((None, tb, SSD_WIDTH), lambda i, j: (i, j, xs_blk)),
            pl.BlockSpec((None, tb, BC_WIDTH), lambda i, j: (i, j, bc_blk)),
            pl.BlockSpec((None, tb, LANES), lambda i, j: (i, j, 0)),
            const((SSD_CONV, CONV_CH)), const((1, CONV_CH)),
            const((1, LANES)), const((1, LANES)),
            const((1, SSD_WIDTH)), const((1, SSD_WIDTH)),
        ],
        out_specs=pl.BlockSpec((None, tb, SSD_WIDTH), lambda i, j: (i, j, 0)),
        out_shape=jax.ShapeDtypeStruct((b, seq, SSD_WIDTH), BF16),
        scratch_shapes=[
            pltpu.VMEM((2, SSD_CHUNK, CONV_CH), BF16),
            pltpu.VMEM((SSD_GROUPS, SSD_STATE, SSD_WIDTH // SSD_GROUPS), F32),
        ],
        compiler_params=_params(("parallel", "arbitrary")),
        name="ssd",
    )(main3, main3, main3, dt3, conv_w, conv_b, dt_bias, a_log, d_skip, norm_g)


def _out_proj_kernel(a_ref, y_ref, x_ref, g_ref, w_ref, o_ref):
    a = a_ref[...].astype(F32)
    an = (a * _rms_scale(a) * g_ref[...]).astype(BF16)
    mixed = _dot(an, w_ref[0:ATTN_WIDTH, :]) + _dot(y_ref[...], w_ref[ATTN_WIDTH:, :])
    o_ref[...] = x_ref[...] + mixed


def _out_proj(attn, y, x, g, w_all, layer, *, tm):
    m, d = x.shape
    return pl.pallas_call(
        _out_proj_kernel,
        grid=(m // tm,),
        in_specs=[
            pl.BlockSpec((tm, ATTN_WIDTH), lambda i: (i, 0)),
            pl.BlockSpec((tm, SSD_WIDTH), lambda i: (i, 0)),
            pl.BlockSpec((tm, d), lambda i: (i, 0)),
            _resident((1, ATTN_WIDTH)),
            _resident(w_all.shape[1:], layer),
        ],
        out_specs=pl.BlockSpec((tm, d), lambda i: (i, 0)),
        out_shape=jax.ShapeDtypeStruct((m, d), F32),
        compiler_params=_params(("parallel",)),
        name="out_proj",
    )(attn, y, x, g, w_all)


def _mlp_kernel(x_ref, g_ref, w1_ref, w2_ref, gf_ref, o_ref, hn_ref, *, final_norm):
    f = pl.program_id(1)

    @pl.when(f == 0)
    def _():
        x = x_ref[...]
        hn_ref[...] = (x * _rms_scale(x) * g_ref[...]).astype(BF16)
        o_ref[...] = x

    h = jnp.maximum(_dot(hn_ref[...], w1_ref[...]), 0.0)
    o_ref[...] += _dot((h * h).astype(BF16), w2_ref[...])

    if final_norm:
        @pl.when(f == pl.num_programs(1) - 1)
        def _():
            o = o_ref[...]
            o_ref[...] = o * _rms_scale(o) * gf_ref[...]


def _mlp(x, g, w1, w2, gf, *, tm, tf, final_norm):
    m, d = x.shape
    dff = w1.shape[1]
    return pl.pallas_call(
        functools.partial(_mlp_kernel, final_norm=final_norm),
        grid=(m // tm, dff // tf),
        in_specs=[
            pl.BlockSpec((tm, d), lambda i, j: (i, 0)),
            pl.BlockSpec((1, d), lambda i, j: (0, 0)),
            pl.BlockSpec((d, tf), lambda i, j: (0, j)),
            pl.BlockSpec((tf, d), lambda i, j: (j, 0)),
            pl.BlockSpec((1, d), lambda i, j: (0, 0)),
        ],
        out_specs=pl.BlockSpec((tm, d), lambda i, j: (i, 0)),
        out_shape=jax.ShapeDtypeStruct((m, d), F32),
        scratch_shapes=[pltpu.VMEM((tm, d), BF16)],
        compiler_params=_params(("parallel", "arbitrary")),
        name="mlp",
    )(x, g, w1, w2, gf)


def _alibi_slopes():
    return jnp.asarray(2.0 ** (-8.0 * (np.arange(ATTN_HEADS) + 1) / ATTN_HEADS), dtype=F32)


def _pad_lanes(v):
    return jnp.pad(v, ((0, 0), (0, LANES - v.shape[1])))


def kernel(x, ln1_g, w_in, conv_w, conv_b, dt_bias, a_log, d_skip, attn_norm_g, ssd_norm_g, w_out,
           ln2_g, w_mlp_in, w_mlp_out, final_norm_g):
    b, seq, d = x.shape
    m = b * seq
    slopes = _alibi_slopes()
    xf = x.reshape(m, d)
    w_in_bf = w_in.astype(BF16)
    w_dt_bf = jnp.pad(w_in[:, :, MAIN_WIDTH:], ((0, 0), (0, 0), (0, LANES - (w_in.shape[2] - MAIN_WIDTH)))).astype(BF16)
    w_out_bf = w_out.astype(BF16)
    for l in range(DEPTH):
        main, dt_raw = _in_proj(xf, ln1_g[l][None, :], w_in_bf, w_dt_bf, l, tm=512, tn=512)
        main3 = main.reshape(b, seq, MAIN_WIDTH)
        attn, w1_bf, w2_bf = _attention(slopes, main3, w_mlp_in, w_mlp_out, l)
        y = _ssd(
            main3, dt_raw.reshape(b, seq, LANES), conv_w[l], conv_b[l][None, :],
            _pad_lanes(dt_bias[l][None, :]), _pad_lanes(a_log[l][None, :]),
            jnp.repeat(d_skip[l], SSD_HEAD_DIM)[None, :], ssd_norm_g[l][None, :], tb=512)
        x1 = _out_proj(attn.reshape(m, ATTN_WIDTH), y.reshape(m, SSD_WIDTH), xf,
                       attn_norm_g[l][None, :], w_out_bf, l, tm=512)
        xf = _mlp(x1, ln2_g[l][None, :], w1_bf, w2_bf,
                  final_norm_g[None, :], tm=1024, tf=512, final_norm=(l == DEPTH - 1))
    return xf.reshape(b, seq, d)
```

```python
import functools

import jax
import jax.numpy as jnp
import numpy as np
from jax import lax
from jax.experimental import pallas as pl
from jax.experimental.pallas import tpu as pltpu

D_MODEL = 2048
DEPTH = 2
HEAD_DIM = 64
ATTN_WIDTH = D_MODEL // 2
ATTN_HEADS = ATTN_WIDTH // HEAD_DIM
DILATED_BRANCHES = ((128, 1), (512, 4), (2048, 16))
ATTN_BLOCK = 128
SSD_WIDTH = D_MODEL // 2
SSD_HEAD_DIM = 64
SSD_HEADS = SSD_WIDTH // SSD_HEAD_DIM
SSD_GROUPS = 2
SSD_STATE = 128
SSD_CONV = 4
SSD_CHUNK = 128
BC_WIDTH = 2 * SSD_GROUPS * SSD_STATE
CONV_CH = SSD_WIDTH + BC_WIDTH
MAIN_WIDTH = 3 * ATTN_WIDTH + SSD_WIDTH + CONV_CH
D_FF = 4 * D_MODEL
NORM_EPS = 1e-5

LANES = 128
HEAD_PAIR = LANES // HEAD_DIM
NEG_BIG = -1e30
LOG2_E = 1.4426950408889634
Q_PRESCALE = HEAD_DIM ** -0.5 * LOG2_E
VMEM_LIMIT = 56 * 1024 * 1024

IN_PROJ_ROWS, IN_PROJ_COLS = 512, 512
OUT_PROJ_ROWS = 512
MLP_ROWS, MLP_HIDDEN = 1024, 512
SSD_TOKENS = 512

F32 = jnp.float32
BF16 = jnp.bfloat16


def _dot(a, b):
    return jnp.dot(a, b, preferred_element_type=F32)


def _dot_nt(a, b):
    return lax.dot_general(a, b, (((1,), (1,)), ((), ())), preferred_element_type=F32)


def _rms_scale(x):
    return lax.rsqrt(jnp.mean(x * x, axis=-1, keepdims=True) + NORM_EPS)


def _silu(x):
    return x * (1.0 / (1.0 + jnp.exp2(x * (-LOG2_E))))


def _params(semantics):
    return pltpu.CompilerParams(dimension_semantics=semantics, vmem_limit_bytes=VMEM_LIMIT)


def _in_proj_kernel(x_ref, g_ref, w_ref, wdt_ref, main_ref, dt_ref, hn_ref, *, tn):
    x = x_ref[...]
    hn_ref[...] = (x * _rms_scale(x) * g_ref[...]).astype(BF16)
    dt_ref[...] = _dot(hn_ref[...], wdt_ref[...])
    for j in range(main_ref.shape[1] // tn):
        cols = slice(j * tn, (j + 1) * tn)
        acc = _dot(hn_ref[...], w_ref[:, cols])
        if (j + 1) * tn <= ATTN_WIDTH:
            acc = acc * Q_PRESCALE
        main_ref[:, cols] = acc.astype(main_ref.dtype)


def _resident(shape, layer=None):
    if layer is None:
        return pl.BlockSpec(shape, lambda i: (0,) * len(shape), pipeline_mode=pl.Buffered(1))
    return pl.BlockSpec((None,) + tuple(shape), lambda i: (layer,) + (0,) * len(shape),
                        pipeline_mode=pl.Buffered(1))


def _in_proj(x, g, w_all, w_dt, layer, *, tm, tn):
    m, d = x.shape
    n = MAIN_WIDTH
    assert m % tm == 0 and n % tn == 0
    return pl.pallas_call(
        functools.partial(_in_proj_kernel, tn=tn),
        grid=(m // tm,),
        in_specs=[
            pl.BlockSpec((tm, d), lambda i: (i, 0)),
            _resident((1, d)),
            _resident(w_all.shape[1:], layer),
            _resident((d, LANES), layer),
        ],
        out_specs=[
            pl.BlockSpec((tm, n), lambda i: (i, 0)),
            pl.BlockSpec((tm, LANES), lambda i: (i, 0)),
        ],
        out_shape=[
            jax.ShapeDtypeStruct((m, n), BF16),
            jax.ShapeDtypeStruct((m, LANES), F32),
        ],
        scratch_shapes=[pltpu.VMEM((tm, d), BF16)],
        compiler_params=_params(("parallel",)),
        name="in_proj",
    )(x, g, w_all, w_dt)


def _attn_kernel(slopes_ref, q_ref, k_ref, v_ref, w1_ref, w2_ref, o_ref, w1o_ref, w2o_ref,
                 qf_ref, kf_ref, vf_ref, qm_ref, km_ref, vm_ref, of_ref,
                 qd_ref, kd_ref, vd_ref, qh_ref, kh_ref, vh_ref, st0_ref, st1_ref, bias_ref):
    seq = q_ref.shape[0]
    blk = ATTN_BLOCK
    hp = pl.program_id(1)

    w1o_ref[...] = w1_ref[...].astype(BF16)
    w2o_ref[...] = w2_ref[...].astype(BF16)

    qf_ref[...] = q_ref[...].astype(F32)
    kf_ref[...] = k_ref[...].astype(F32)
    vf_ref[...] = v_ref[...].astype(F32)

    low = lax.broadcasted_iota(jnp.int32, (blk, LANES), 1) < HEAD_DIM
    ones_bf = jnp.ones((2 * blk, LANES), BF16)

    row = lax.broadcasted_iota(jnp.int32, (2 * blk, 2 * blk), 0)
    col = lax.broadcasted_iota(jnp.int32, (2 * blk, 2 * blk), 1)
    delta = (row & (blk - 1)) - col + blk
    slope = jnp.where(row < blk, slopes_ref[HEAD_PAIR * hp], slopes_ref[HEAD_PAIR * hp + 1])
    for t, (window, dil) in enumerate(DILATED_BRANCHES):
        valid = (delta >= 0) & (delta <= window // dil)
        bias = jnp.where(valid, -slope * (delta * dil).astype(F32) * LOG2_E, NEG_BIG)
        bias_ref[2 * t] = bias
        bias_ref[2 * t + 1] = jnp.where(col >= blk, bias, NEG_BIG)

    def pair_rows(x):
        return jnp.where(low, jnp.broadcast_to(x[:blk], (blk, LANES)),
                         jnp.broadcast_to(x[blk:], (blk, LANES)))

    def pair_halves(x):
        return jnp.where(low, x[:blk], x[blk:])

    zero_bf = jnp.zeros((), BF16)
    n_blocks = seq // blk

    def block_rows(i):
        start = i * blk
        return pl.ds(start if isinstance(start, int) else pl.multiple_of(start, blk), blk)

    (_, d_lo), (_, d_mid), (_, d_hi) = DILATED_BRANCHES
    ratio = d_hi // d_mid
    assert d_lo == 1 and d_hi == d_mid * ratio and seq % (blk * d_hi) == 0
    mid_len = seq // d_mid

    def log2(n):
        assert n & (n - 1) == 0
        return n.bit_length() - 1

    def split_block(idx, dil):
        nb = n_blocks // dil
        return idx >> log2(nb), idx & (nb - 1)

    def token_rows(idx, dil):
        r, n = split_block(idx, dil)
        return pl.ds(r + n * (blk * dil), blk, stride=dil) if dil > 1 else block_rows(idx)

    def hi_rows_in_mid(idx):
        r, n = split_block(idx, d_hi)
        return pl.ds((r & (d_mid - 1)) * mid_len + n * (blk * ratio) + (r >> log2(d_mid)), blk, stride=ratio)

    def gather_mid(idx):
        rows = token_rows(idx, d_mid)
        for f_ref, mid_ref, d_ref in ((qf_ref, qm_ref, qd_ref), (kf_ref, km_ref, kd_ref), (vf_ref, vm_ref, vd_ref)):
            x = f_ref[rows, :]
            mid_ref[block_rows(idx), :] = x
            d_ref[block_rows(idx), :] = x.astype(BF16)

    def gather_hi(idx):
        rows = hi_rows_in_mid(idx)
        for mid_ref, h_ref in ((qm_ref, qh_ref), (km_ref, kh_ref), (vm_ref, vh_ref)):
            h_ref[block_rows(idx), :] = mid_ref[rows, :].astype(BF16)

    def block_softmax(t, idx, q_src, k_src, v_src):
        first = split_block(idx, DILATED_BRANCHES[t][1])[1] == 0
        cur = block_rows(idx)
        prev = block_rows(idx if first else idx - 1)
        q = q_src[cur, :]
        q2 = jnp.concatenate([jnp.where(low, q, zero_bf), jnp.where(low, zero_bf, q)], axis=0)
        kk = jnp.concatenate([k_src[prev, :], k_src[cur, :]], axis=0)
        vv = jnp.concatenate([jnp.concatenate([v_src[prev, :], v_src[cur, :]], axis=0), ones_bf], axis=1)
        s = _dot_nt(q2, kk) + bias_ref[2 * t + int(first)]
        m = jnp.max(s, axis=-1, keepdims=True)
        p = jnp.exp2(s - m)
        pv = _dot(p.astype(BF16), vv)
        return pair_rows(m), pair_halves(pv[:, LANES:]), pair_halves(pv[:, :LANES])

    def combine(a, b):
        (ma, la, acca), (mb, lb, accb) = a, b
        m = jnp.maximum(ma, mb)
        wa, wb = jnp.exp2(ma - m), jnp.exp2(mb - m)
        return m, wa * la + wb * lb, wa * acca + wb * accb

    def load_state(st_ref, rows):
        return tuple(st_ref[i, rows, :] for i in range(3))

    def store_state(st_ref, rows, state):
        for i, x in enumerate(state):
            st_ref[i, rows, :] = x

    def branch0(idx):
        store_state(st0_ref, block_rows(idx), block_softmax(0, idx, q_ref, k_ref, v_ref))
        gather_mid(idx)

    def branch1(idx):
        mine = block_softmax(1, idx, qd_ref, kd_ref, vd_ref)
        store_state(st1_ref, block_rows(idx), combine(load_state(st0_ref, token_rows(idx, d_mid)), mine))
        gather_hi(idx)

    def branch2(idx):
        mine = block_softmax(2, idx, qh_ref, kh_ref, vh_ref)
        _, l_all, acc_all = combine(load_state(st1_ref, hi_rows_in_mid(idx)), mine)
        of_ref[token_rows(idx, d_hi), :] = acc_all / l_all

    guards = (pl.program_id(0) >= 0, pl.program_id(1) >= 0, pl.program_id(0) + pl.program_id(1) >= 0)
    for branch, guard in zip((branch0, branch1, branch2), guards):
        @pl.when(guard)
        def _(branch=branch):
            for idx in range(n_blocks):
                branch(idx)
    o_ref[...] = of_ref[...].astype(o_ref.dtype)


def _attention(slopes, main3, w1_all, w2_all, layer):
    b, seq, _ = main3.shape
    n_pairs = ATTN_WIDTH // LANES
    blk = ATTN_BLOCK
    n_steps = b * n_pairs
    w1_shape, w2_shape = w1_all.shape[1:], w2_all.shape[1:]
    r1, r2 = w1_shape[0] // n_steps, w2_shape[0] // n_steps
    assert r1 * n_steps == w1_shape[0] and r2 * n_steps == w2_shape[0]

    def spec(offset):
        return pl.BlockSpec((None, seq, LANES), lambda i, j: (i, 0, offset + j))

    return pl.pallas_call(
        _attn_kernel,
        grid=(b, n_pairs),
        in_specs=[
            pl.BlockSpec(memory_space=pltpu.SMEM),
            spec(0), spec(n_pairs), spec(2 * n_pairs),
            pl.BlockSpec((None, r1, w1_shape[1]), lambda i, j: (layer, i * n_pairs + j, 0)),
            pl.BlockSpec((None, r2, w2_shape[1]), lambda i, j: (layer, i * n_pairs + j, 0)),
        ],
        out_specs=[
            pl.BlockSpec((None, seq, LANES), lambda i, j: (i, 0, j)),
            pl.BlockSpec((r1, w1_shape[1]), lambda i, j: (i * n_pairs + j, 0)),
            pl.BlockSpec((r2, w2_shape[1]), lambda i, j: (i * n_pairs + j, 0)),
        ],
        out_shape=[
            jax.ShapeDtypeStruct((b, seq, ATTN_WIDTH), BF16),
            jax.ShapeDtypeStruct(w1_shape, BF16),
            jax.ShapeDtypeStruct(w2_shape, BF16),
        ],
        scratch_shapes=[pltpu.VMEM((seq, LANES), F32)] * 7
        + [pltpu.VMEM((seq, LANES), BF16)] * 6
        + [pltpu.VMEM((3, seq, LANES), F32)] * 2
        + [pltpu.VMEM((2 * len(DILATED_BRANCHES), 2 * blk, 2 * blk), F32)],
        compiler_params=_params(("parallel", "parallel")),
        name="attention",
    )(slopes, main3, main3, main3, w1_all, w2_all)


def _ssd_kernel(z_ref, xs_ref, bc_ref, dt_ref, cw_ref, cb_ref, dtb_ref, alog_ref, dskip_ref, g_ref,
                y_ref, carry_ref, state_ref):
    tb = xs_ref.shape[0]
    q = SSD_CHUNK
    gw = SSD_WIDTH // SSD_GROUPS
    pairs_per_group = gw // LANES
    conv_cols = 512

    step = pl.program_id(1)
    slot = step & 1

    @pl.when(step == 0)
    def _():
        carry_ref[0] = jnp.zeros((q, CONV_CH), BF16)
        state_ref[...] = jnp.zeros(state_ref.shape, F32)

    carry_ref[1 - slot, :, 0:SSD_WIDTH] = xs_ref[tb - q:tb, :]
    carry_ref[1 - slot, :, SSD_WIDTH:CONV_CH] = bc_ref[tb - q:tb, :]

    low = lax.broadcasted_iota(jnp.int32, (q, LANES), 1) < SSD_HEAD_DIM
    ri = lax.broadcasted_iota(jnp.int32, (q, q), 0)
    ci = lax.broadcasted_iota(jnp.int32, (q, q), 1)
    tri = ri >= ci
    tri_bf = jnp.where(tri, 1.0, 0.0).astype(BF16)
    a_neg = -jnp.exp(alog_ref[...])

    sr = lax.broadcasted_iota(jnp.int32, (SSD_CONV * q, 2 * q), 0)
    sc = lax.broadcasted_iota(jnp.int32, (SSD_CONV * q, 2 * q), 1)
    src_row = (sr & (q - 1)) + (sr >> (q.bit_length() - 1)) + (q - (SSD_CONV - 1))
    shift = jnp.where(sc == src_row, 1.0, 0.0).astype(BF16)

    def cumsum_rows(v):
        hi = v.astype(BF16)
        r1 = v - hi.astype(F32)
        mid = r1.astype(BF16)
        lo = (r1 - mid.astype(F32)).astype(BF16)
        return _dot(tri_bf, hi) + _dot(tri_bf, mid) + _dot(tri_bf, lo)

    def lane_bcast(v, h):
        return jnp.broadcast_to(v[:, h:h + 1], (q, LANES))

    def chunk(c):
        r0 = c * q
        u_parts = []
        for j in range(CONV_CH // conv_cols):
            cols = slice(j * conv_cols, (j + 1) * conv_cols)
            src_ref, off = (xs_ref, 0) if j * conv_cols < SSD_WIDTH else (bc_ref, SSD_WIDTH)
            src_cols = slice(j * conv_cols - off, (j + 1) * conv_cols - off)
            prev = carry_ref[slot, :, cols] if c == 0 else src_ref[r0 - q:r0, src_cols]
            window = jnp.concatenate([prev, src_ref[r0:r0 + q, src_cols]], axis=0)
            taps = _dot(shift, window)
            uj = cb_ref[:, cols]
            for k in range(SSD_CONV):
                uj = uj + cw_ref[k:k + 1, cols] * taps[k * q:(k + 1) * q]
            u_parts.append(_silu(uj))
        u = jnp.concatenate(u_parts, axis=1)
        x = u[:, :SSD_WIDTH]
        dt_in = dt_ref[pl.ds(r0, q), :] + dtb_ref[...]
        dt = jnp.maximum(dt_in, 0.0) + jnp.log(1.0 + jnp.exp(-jnp.abs(dt_in)))
        acum = cumsum_rows(dt * a_neg) * LOG2_E
        acum_t = acum.T

        y_parts = []
        for g in range(SSD_GROUPS):
            bg = u[:, SSD_WIDTH + g * SSD_STATE:SSD_WIDTH + (g + 1) * SSD_STATE]
            cg = u[:, SSD_WIDTH + (SSD_GROUPS + g) * SSD_STATE:SSD_WIDTH + (SSD_GROUPS + g + 1) * SSD_STATE]
            cg_bf = cg.astype(BF16)
            cb = _dot_nt(cg_bf, bg.astype(BF16))
            st = state_ref[g]
            y_off = _dot(cg_bf, st.astype(BF16))
            xd_parts, decay_parts = [], []
            for pp in range(pairs_per_group):
                p = g * pairs_per_group + pp
                h0 = HEAD_PAIR * p
                xp = x[:, p * LANES:(p + 1) * LANES]
                cols = [lane_bcast(acum, h0 + e) for e in range(HEAD_PAIR)]
                colc = jnp.where(low, cols[0], cols[1])
                ms = []
                for e in range(HEAD_PAIR):
                    h = h0 + e
                    seg = jnp.exp2(jnp.where(tri, cols[e] - acum_t[h:h + 1, :], NEG_BIG))
                    ms.append(cb * seg)
                m_cat = jnp.concatenate(ms, axis=1).astype(BF16)
                dtc = jnp.where(low, lane_bcast(dt, h0), lane_bcast(dt, h0 + 1))
                xdt = xp * dtc
                x_blk = jnp.concatenate([jnp.where(low, xdt, 0.0), jnp.where(low, 0.0, xdt)],
                                        axis=0).astype(BF16)
                y_diag = _dot(m_cat, x_blk)
                y_o = y_off[:, pp * LANES:(pp + 1) * LANES] * jnp.exp2(colc)
                y_parts.append(y_diag + y_o + dskip_ref[:, p * LANES:(p + 1) * LANES] * xp)
                last = colc[q - 1:q, :]
                xd_parts.append(xdt * jnp.exp2(last - colc))
                decay_parts.append(jnp.exp2(last))
            xd = jnp.concatenate(xd_parts, axis=1).astype(BF16)
            decay = jnp.concatenate(decay_parts, axis=1)
            state_ref[g] = st * decay + _dot(bg.T.astype(BF16), xd)

        z = z_ref[pl.ds(r0, q), :].astype(F32)
        outs = []
        for g in range(SSD_GROUPS):
            yg = jnp.concatenate(y_parts[g * pairs_per_group:(g + 1) * pairs_per_group], axis=1)
            yg = yg * _silu(z[:, g * gw:(g + 1) * gw])
            outs.append(yg * _rms_scale(yg) * g_ref[:, g * gw:(g + 1) * gw])
        y_ref[pl.ds(r0, q), :] = jnp.concatenate(outs, axis=1).astype(y_ref.dtype)

    for c in range(tb // q):
        chunk(c)


def _ssd(main3, dt3, conv_w, conv_b, dt_bias, a_log, d_skip, norm_g, *, tb):
    b, seq, _ = main3.shape
    z_blk = (3 * ATTN_WIDTH) // SSD_WIDTH
    xs_blk = (3 * ATTN_WIDTH + SSD_WIDTH) // SSD_WIDTH
    bc_blk = (3 * ATTN_WIDTH + 2 * SSD_WIDTH) // BC_WIDTH

    def const(shape):
        return pl.BlockSpec(shape, lambda i, j: (0,) * len(shape))

    return pl.pallas_call(
        _ssd_kernel,
        grid=(b, seq // tb),
        in_specs=[
            pl.BlockSpec((None, tb, SSD_WIDTH), lambda i, j: (i, j, z_blk)),
            pl.BlockSpec((None, tb, SSD_WIDTH), lambda i, j: (i, j, xs_blk)),
            pl.BlockSpec((None, tb, BC_WIDTH), lambda i, j: (i, j, bc_blk)),
            pl.BlockSpec((None, tb, LANES), lambda i, j: (i, j, 0)),
            const((SSD_CONV, CONV_CH)), const((1, CONV_CH)),
            const((1, LANES)), const((1, LANES)),
            const((1, SSD_WIDTH)), const((1, SSD_WIDTH)),
        ],
        out_specs=pl.BlockSpec((None, tb, SSD_WIDTH), lambda i, j: (i, j, 0)),
        out_shape=jax.ShapeDtypeStruct((b, seq, SSD_WIDTH), BF16),
        scratch_shapes=[
            pltpu.VMEM((2, SSD_CHUNK, CONV_CH), BF16),
            pltpu.VMEM((SSD_GROUPS, SSD_STATE, SSD_WIDTH // SSD_GROUPS), F32),
        ],
        compiler_params=_params(("parallel", "arbitrary")),
        name="ssd",
    )(main3, main3, main3, dt3, conv_w, conv_b, dt_bias, a_log, d_skip, norm_g)


def _out_proj_kernel(a_ref, y_ref, x_ref, g_ref, w_ref, o_ref):
    a = a_ref[...].astype(F32)
    an = (a * _rms_scale(a) * g_ref[...]).astype(BF16)
    mixed = _dot(an, w_ref[0:ATTN_WIDTH, :]) + _dot(y_ref[...], w_ref[ATTN_WIDTH:, :])
    o_ref[...] = x_ref[...] + mixed


def _out_proj(attn, y, x, g, w_all, layer, *, tm):
    m, d = x.shape
    return pl.pallas_call(
        _out_proj_kernel,
        grid=(m // tm,),
        in_specs=[
            pl.BlockSpec((tm, ATTN_WIDTH), lambda i: (i, 0)),
            pl.BlockSpec((tm, SSD_WIDTH), lambda i: (i, 0)),
            pl.BlockSpec((tm, d), lambda i: (i, 0)),
            _resident((1, ATTN_WIDTH)),
            _resident(w_all.shape[1:], layer),
        ],
        out_specs=pl.BlockSpec((tm, d), lambda i: (i, 0)),
        out_shape=jax.ShapeDtypeStruct((m, d), F32),
        compiler_params=_params(("parallel",)),
        name="out_proj",
    )(attn, y, x, g, w_all)


def _mlp_kernel(x_ref, g_ref, w1_ref, w2_ref, gf_ref, o_ref, hn_ref, *, final_norm):
    f = pl.program_id(1)

    @pl.when(f == 0)
    def _():
        x = x_ref[...]
        hn_ref[...] = (x * _rms_scale(x) * g_ref[...]).astype(BF16)
        o_ref[...] = x

    h = jnp.maximum(_dot(hn_ref[...], w1_ref[...]), 0.0)
    o_ref[...] += _dot((h * h).astype(BF16), w2_ref[...])

    if final_norm:
        @pl.when(f == pl.num_programs(1) - 1)
        def _():
            o = o_ref[...]
            o_ref[...] = o * _rms_scale(o) * gf_ref[...]


def _mlp(x, g, w1, w2, gf, *, tm, tf, final_norm):
    m, d = x.shape
    dff = w1.shape[1]
    return pl.pallas_call(
        functools.partial(_mlp_kernel, final_norm=final_norm),
        grid=(m // tm, dff // tf),
        in_specs=[
            pl.BlockSpec((tm, d), lambda i, j: (i, 0)),
            pl.BlockSpec((1, d), lambda i, j: (0, 0)),
            pl.BlockSpec((d, tf), lambda i, j: (0, j)),
            pl.BlockSpec((tf, d), lambda i, j: (j, 0)),
            pl.BlockSpec((1, d), lambda i, j: (0, 0)),
        ],
        out_specs=pl.BlockSpec((tm, d), lambda i, j: (i, 0)),
        out_shape=jax.ShapeDtypeStruct((m, d), F32),
        scratch_shapes=[pltpu.VMEM((tm, d), BF16)],
        compiler_params=_params(("parallel", "arbitrary")),
        name="mlp",
    )(x, g, w1, w2, gf)


def _alibi_slopes():
    return jnp.asarray(2.0 ** (-8.0 * (np.arange(ATTN_HEADS) + 1) / ATTN_HEADS), dtype=F32)


def _pad_lanes(v):
    return jnp.pad(v, ((0, 0), (0, LANES - v.shape[1])))


def kernel(x, ln1_g, w_in, conv_w, conv_b, dt_bias, a_log, d_skip, attn_norm_g, ssd_norm_g, w_out,
           ln2_g, w_mlp_in, w_mlp_out, final_norm_g):
    b, seq, d = x.shape
    m = b * seq
    slopes = _alibi_slopes()
    xf = x.reshape(m, d)
    w_in_bf = w_in.astype(BF16)
    w_dt_bf = jnp.pad(w_in[:, :, MAIN_WIDTH:], ((0, 0), (0, 0), (0, LANES - (w_in.shape[2] - MAIN_WIDTH)))).astype(BF16)
    w_out_bf = w_out.astype(BF16)
    for l in range(DEPTH):
        main, dt_raw = _in_proj(xf, ln1_g[l][None, :], w_in_bf, w_dt_bf, l,
                                tm=IN_PROJ_ROWS, tn=IN_PROJ_COLS)
        main3 = main.reshape(b, seq, MAIN_WIDTH)
        attn, w1_bf, w2_bf = _attention(slopes, main3, w_mlp_in, w_mlp_out, l)
        y = _ssd(
            main3, dt_raw.reshape(b, seq, LANES), conv_w[l], conv_b[l][None, :],
            _pad_lanes(dt_bias[l][None, :]), _pad_lanes(a_log[l][None, :]),
            jnp.repeat(d_skip[l], SSD_HEAD_DIM)[None, :], ssd_norm_g[l][None, :], tb=SSD_TOKENS)
        x1 = _out_proj(attn.reshape(m, ATTN_WIDTH), y.reshape(m, SSD_WIDTH), xf,
                       attn_norm_g[l][None, :], w_out_bf, l, tm=OUT_PROJ_ROWS)
        xf = _mlp(x1, ln2_g[l][None, :], w1_bf, w2_bf,
                  final_norm_g[None, :], tm=MLP_ROWS, tf=MLP_HIDDEN, final_norm=(l == DEPTH - 1))
    return xf.reshape(b, seq, d)
```

```python
import functools

import jax
import jax.numpy as jnp
import numpy as np
from jax import lax
from jax.experimental import pallas as pl
from jax.experimental.pallas import tpu as pltpu

D_MODEL = 2048
DEPTH = 2
HEAD_DIM = 64
ATTN_WIDTH = D_MODEL // 2
ATTN_HEADS = ATTN_WIDTH // HEAD_DIM
DILATED_BRANCHES = ((128, 1), (512, 4), (2048, 16))
ATTN_BLOCK = 128
SSD_WIDTH = D_MODEL // 2
SSD_HEAD_DIM = 64
SSD_HEADS = SSD_WIDTH // SSD_HEAD_DIM
SSD_GROUPS = 2
SSD_STATE = 128
SSD_CONV = 4
SSD_CHUNK = 128
BC_WIDTH = 2 * SSD_GROUPS * SSD_STATE
CONV_CH = SSD_WIDTH + BC_WIDTH
MAIN_WIDTH = 3 * ATTN_WIDTH + SSD_WIDTH + CONV_CH
D_FF = 4 * D_MODEL
NORM_EPS = 1e-5

LANES = 128
HEAD_PAIR = LANES // HEAD_DIM
NEG_BIG = -1e30
LOG2_E = 1.4426950408889634
Q_PRESCALE = HEAD_DIM ** -0.5 * LOG2_E
VMEM_LIMIT = 56 * 1024 * 1024

IN_PROJ_ROWS, IN_PROJ_COLS = 512, 512
OUT_PROJ_ROWS = 512
MLP_ROWS, MLP_HIDDEN = 1024, 512
SSD_TOKENS = 512

F32 = jnp.float32
BF16 = jnp.bfloat16


def _dot(a, b):
    return jnp.dot(a, b, preferred_element_type=F32)


def _dot_nt(a, b):
    return lax.dot_general(a, b, (((1,), (1,)), ((), ())), preferred_element_type=F32)


def _rms_scale(x):
    return lax.rsqrt(jnp.mean(x * x, axis=-1, keepdims=True) + NORM_EPS)


def _silu(x):
    return x * (1.0 / (1.0 + jnp.exp2(x * (-LOG2_E))))


def _params(semantics):
    return pltpu.CompilerParams(dimension_semantics=semantics, vmem_limit_bytes=VMEM_LIMIT)


def _in_proj_kernel(x_ref, g_ref, w_ref, wdt_ref, main_ref, dt_ref, hn_ref, *, tn):
    x = x_ref[...]
    hn_ref[...] = (x * _rms_scale(x) * g_ref[...]).astype(BF16)
    dt_ref[...] = _dot(hn_ref[...], wdt_ref[...])
    for j in range(main_ref.shape[1] // tn):
        cols = slice(j * tn, (j + 1) * tn)
        acc = _dot(hn_ref[...], w_ref[:, cols])
        if (j + 1) * tn <= ATTN_WIDTH:
            acc = acc * Q_PRESCALE
        main_ref[:, cols] = acc.astype(main_ref.dtype)


def _resident(shape, layer=None):
    if layer is None:
        return pl.BlockSpec(shape, lambda i: (0,) * len(shape), pipeline_mode=pl.Buffered(1))
    return pl.BlockSpec((None,) + tuple(shape), lambda i: (layer,) + (0,) * len(shape),
                        pipeline_mode=pl.Buffered(1))


def _in_proj(x, g, w_all, w_dt, layer, *, tm, tn):
    m, d = x.shape
    n = MAIN_WIDTH
    assert m % tm == 0 and n % tn == 0
    return pl.pallas_call(
        functools.partial(_in_proj_kernel, tn=tn),
        grid=(m // tm,),
        in_specs=[
            pl.BlockSpec((tm, d), lambda i: (i, 0)),
            _resident((1, d)),
            _resident(w_all.shape[1:], layer),
            _resident((d, LANES), layer),
        ],
        out_specs=[
            pl.BlockSpec((tm, n), lambda i: (i, 0)),
            pl.BlockSpec((tm, LANES), lambda i: (i, 0)),
        ],
        out_shape=[
            jax.ShapeDtypeStruct((m, n), BF16),
            jax.ShapeDtypeStruct((m, LANES), F32),
        ],
        scratch_shapes=[pltpu.VMEM((tm, d), BF16)],
        compiler_params=_params(("parallel",)),
        name="in_proj",
    )(x, g, w_all, w_dt)


def _attn_kernel(slopes_ref, q_ref, k_ref, v_ref, w1_ref, w2_ref, o_ref, w1o_ref, w2o_ref,
                 qf_ref, kf_ref, vf_ref, qm_ref, km_ref, vm_ref, of_ref,
                 qd_ref, kd_ref, vd_ref, qh_ref, kh_ref, vh_ref, st0_ref, st1_ref, bias_ref):
    seq = q_ref.shape[0]
    blk = ATTN_BLOCK
    hp = pl.program_id(1)

    w1o_ref[...] = w1_ref[...].astype(BF16)
    w2o_ref[...] = w2_ref[...].astype(BF16)

    qf_ref[...] = q_ref[...].astype(F32)
    kf_ref[...] = k_ref[...].astype(F32)
    vf_ref[...] = v_ref[...].astype(F32)

    low = lax.broadcasted_iota(jnp.int32, (blk, LANES), 1) < HEAD_DIM
    ones_bf = jnp.ones((2 * blk, LANES), BF16)

    row = lax.broadcasted_iota(jnp.int32, (2 * blk, 2 * blk), 0)
    col = lax.broadcasted_iota(jnp.int32, (2 * blk, 2 * blk), 1)
    delta = (row & (blk - 1)) - col + blk
    slope = jnp.where(row < blk, slopes_ref[HEAD_PAIR * hp], slopes_ref[HEAD_PAIR * hp + 1])
    for t, (window, dil) in enumerate(DILATED_BRANCHES):
        valid = (delta >= 0) & (delta <= window // dil)
        bias = jnp.where(valid, -slope * (delta * dil).astype(F32) * LOG2_E, NEG_BIG)
        bias_ref[2 * t] = bias
        bias_ref[2 * t + 1] = jnp.where(col >= blk, bias, NEG_BIG)

    def pair_rows(x):
        return jnp.where(low, jnp.broadcast_to(x[:blk], (blk, LANES)),
                         jnp.broadcast_to(x[blk:], (blk, LANES)))

    def pair_halves(x):
        return jnp.where(low, x[:blk], x[blk:])

    zero_bf = jnp.zeros((), BF16)
    n_blocks = seq // blk

    def block_rows(i):
        start = i * blk
        return pl.ds(start if isinstance(start, int) else pl.multiple_of(start, blk), blk)

    (_, d_lo), (_, d_mid), (_, d_hi) = DILATED_BRANCHES
    ratio = d_hi // d_mid
    assert d_lo == 1 and d_hi == d_mid * ratio and seq % (blk * d_hi) == 0
    mid_len = seq // d_mid

    def log2(n):
        assert n & (n - 1) == 0
        return n.bit_length() - 1

    def split_block(idx, dil):
        nb = n_blocks // dil
        return idx >> log2(nb), idx & (nb - 1)

    def token_rows(idx, dil):
        r, n = split_block(idx, dil)
        return pl.ds(r + n * (blk * dil), blk, stride=dil) if dil > 1 else block_rows(idx)

    def hi_rows_in_mid(idx):
        r, n = split_block(idx, d_hi)
        return pl.ds((r & (d_mid - 1)) * mid_len + n * (blk * ratio) + (r >> log2(d_mid)), blk, stride=ratio)

    def gather_mid(idx):
        rows = token_rows(idx, d_mid)
        for f_ref, mid_ref, d_ref in ((qf_ref, qm_ref, qd_ref), (kf_ref, km_ref, kd_ref), (vf_ref, vm_ref, vd_ref)):
            x = f_ref[rows, :]
            mid_ref[block_rows(idx), :] = x
            d_ref[block_rows(idx), :] = x.astype(BF16)

    def gather_hi(idx):
        rows = hi_rows_in_mid(idx)
        for mid_ref, h_ref in ((qm_ref, qh_ref), (km_ref, kh_ref), (vm_ref, vh_ref)):
            h_ref[block_rows(idx), :] = mid_ref[rows, :].astype(BF16)

    def block_softmax(t, idx, q_src, k_src, v_src):
        first = split_block(idx, DILATED_BRANCHES[t][1])[1] == 0
        cur = block_rows(idx)
        prev = block_rows(idx if first else idx - 1)
        q = q_src[cur, :]
        q2 = jnp.concatenate([jnp.where(low, q, zero_bf), jnp.where(low, zero_bf, q)], axis=0)
        kk = jnp.concatenate([k_src[prev, :], k_src[cur, :]], axis=0)
        vv = jnp.concatenate([jnp.concatenate([v_src[prev, :], v_src[cur, :]], axis=0), ones_bf], axis=1)
        s = _dot_nt(q2, kk) + bias_ref[2 * t + int(first)]
        m = jnp.max(s, axis=-1, keepdims=True)
        p = jnp.exp2(s - m)
        pv = _dot(p.astype(BF16), vv)
        return pair_rows(m), pair_halves(pv[:, LANES:]), pair_halves(pv[:, :LANES])

    def combine(a, b):
        (ma, la, acca), (mb, lb, accb) = a, b
        m = jnp.maximum(ma, mb)
        wa, wb = jnp.exp2(ma - m), jnp.exp2(mb - m)
        return m, wa * la + wb * lb, wa * acca + wb * accb

    def load_state(st_ref, rows):
        return tuple(st_ref[i, rows, :] for i in range(3))

    def store_state(st_ref, rows, state):
        for i, x in enumerate(state):
            st_ref[i, rows, :] = x

    def branch0(idx):
        store_state(st0_ref, block_rows(idx), block_softmax(0, idx, q_ref, k_ref, v_ref))
        gather_mid(idx)

    def branch1(idx):
        mine = block_softmax(1, idx, qd_ref, kd_ref, vd_ref)
        store_state(st1_ref, block_rows(idx), combine(load_state(st0_ref, token_rows(idx, d_mid)), mine))
        gather_hi(idx)

    def branch2(idx):
        mine = block_softmax(2, idx, qh_ref, kh_ref, vh_ref)
        _, l_all, acc_all = combine(load_state(st1_ref, hi_rows_in_mid(idx)), mine)
        of_ref[token_rows(idx, d_hi), :] = acc_all / l_all

    for branch in (branch0, branch1, branch2):
        for idx in range(n_blocks):
            branch(idx)
    o_ref[...] = of_ref[...].astype(o_ref.dtype)


def _attention(slopes, main3, w1_all, w2_all, layer):
    b, seq, _ = main3.shape
    n_pairs = ATTN_WIDTH // LANES
    blk = ATTN_BLOCK
    n_steps = b * n_pairs
    w1_shape, w2_shape = w1_all.shape[1:], w2_all.shape[1:]
    r1, r2 = w1_shape[0] // n_steps, w2_shape[0] // n_steps
    assert r1 * n_steps == w1_shape[0] and r2 * n_steps == w2_shape[0]

    def spec(offset):
        return pl.BlockSpec((None, seq, LANES), lambda i, j: (i, 0, offset + j))

    return pl.pallas_call(
        _attn_kernel,
        grid=(b, n_pairs),
        in_specs=[
            pl.BlockSpec(memory_space=pltpu.SMEM),
            spec(0), spec(n_pairs), spec(2 * n_pairs),
            pl.BlockSpec((None, r1, w1_shape[1]), lambda i, j: (layer, i * n_pairs + j, 0)),
            pl.BlockSpec((None, r2, w2_shape[1]), lambda i, j: (layer, i * n_pairs + j, 0)),
        ],
        out_specs=[
            pl.BlockSpec((None, seq, LANES), lambda i, j: (i, 0, j)),
            pl.BlockSpec((r1, w1_shape[1]), lambda i, j: (i * n_pairs + j, 0)),
            pl.BlockSpec((r2, w2_shape[1]), lambda i, j: (i * n_pairs + j, 0)),
        ],
        out_shape=[
            jax.ShapeDtypeStruct((b, seq, ATTN_WIDTH), BF16),
            jax.ShapeDtypeStruct(w1_shape, BF16),
            jax.ShapeDtypeStruct(w2_shape, BF16),
        ],
        scratch_shapes=[pltpu.VMEM((seq, LANES), F32)] * 7
        + [pltpu.VMEM((seq, LANES), BF16)] * 6
        + [pltpu.VMEM((3, seq, LANES), F32)] * 2
        + [pltpu.VMEM((2 * len(DILATED_BRANCHES), 2 * blk, 2 * blk), F32)],
        compiler_params=_params(("parallel", "parallel")),
        name="attention",
    )(slopes, main3, main3, main3, w1_all, w2_all)


def _ssd_kernel(z_ref, xs_ref, bc_ref, dt_ref, cw_ref, cb_ref, dtb_ref, alog_ref, dskip_ref, g_ref,
                y_ref, carry_ref, state_ref):
    tb = xs_ref.shape[0]
    q = SSD_CHUNK
    gw = SSD_WIDTH // SSD_GROUPS
    pairs_per_group = gw // LANES
    conv_cols = 512

    step = pl.program_id(1)
    slot = step & 1

    @pl.when(step == 0)
    def _():
        carry_ref[0] = jnp.zeros((q, CONV_CH), BF16)
        state_ref[...] = jnp.zeros(state_ref.shape, F32)

    carry_ref[1 - slot, :, 0:SSD_WIDTH] = xs_ref[tb - q:tb, :]
    carry_ref[1 - slot, :, SSD_WIDTH:CONV_CH] = bc_ref[tb - q:tb, :]

    low = lax.broadcasted_iota(jnp.int32, (q, LANES), 1) < SSD_HEAD_DIM
    ri = lax.broadcasted_iota(jnp.int32, (q, q), 0)
    ci = lax.broadcasted_iota(jnp.int32, (q, q), 1)
    tri = ri >= ci
    tri_bf = jnp.where(tri, 1.0, 0.0).astype(BF16)
    a_neg = -jnp.exp(alog_ref[...])

    sr = lax.broadcasted_iota(jnp.int32, (SSD_CONV * q, 2 * q), 0)
    sc = lax.broadcasted_iota(jnp.int32, (SSD_CONV * q, 2 * q), 1)
    src_row = (sr & (q - 1)) + (sr >> (q.bit_length() - 1)) + (q - (SSD_CONV - 1))
    shift = jnp.where(sc == src_row, 1.0, 0.0).astype(BF16)

    def cumsum_rows(v):
        hi = v.astype(BF16)
        r1 = v - hi.astype(F32)
        mid = r1.astype(BF16)
        lo = (r1 - mid.astype(F32)).astype(BF16)
        return _dot(tri_bf, hi) + _dot(tri_bf, mid) + _dot(tri_bf, lo)

    def lane_bcast(v, h):
        return jnp.broadcast_to(v[:, h:h + 1], (q, LANES))

    def chunk(c):
        r0 = c * q
        u_parts = []
        for j in range(CONV_CH // conv_cols):
            cols = slice(j * conv_cols, (j + 1) * conv_cols)
            src_ref, off = (xs_ref, 0) if j * conv_cols < SSD_WIDTH else (bc_ref, SSD_WIDTH)
            src_cols = slice(j * conv_cols - off, (j + 1) * conv_cols - off)
            prev = carry_ref[slot, :, cols] if c == 0 else src_ref[r0 - q:r0, src_cols]
            window = jnp.concatenate([prev, src_ref[r0:r0 + q, src_cols]], axis=0)
            taps = _dot(shift, window)
            uj = cb_ref[:, cols]
            for k in range(SSD_CONV):
                uj = uj + cw_ref[k:k + 1, cols] * taps[k * q:(k + 1) * q]
            u_parts.append(_silu(uj))
        u = jnp.concatenate(u_parts, axis=1)
        x = u[:, :SSD_WIDTH]
        dt_in = dt_ref[pl.ds(r0, q), :] + dtb_ref[...]
        dt = jnp.maximum(dt_in, 0.0) + jnp.log(1.0 + jnp.exp(-jnp.abs(dt_in)))
        acum = cumsum_rows(dt * a_neg) * LOG2_E
        acum_t = acum.T

        y_parts = []
        for g in range(SSD_GROUPS):
            bg = u[:, SSD_WIDTH + g * SSD_STATE:SSD_WIDTH + (g + 1) * SSD_STATE]
            cg = u[:, SSD_WIDTH + (SSD_GROUPS + g) * SSD_STATE:SSD_WIDTH + (SSD_GROUPS + g + 1) * SSD_STATE]
            cg_bf = cg.astype(BF16)
            cb = _dot_nt(cg_bf, bg.astype(BF16))
            st = state_ref[g]
            y_off = _dot(cg_bf, st.astype(BF16))
            xd_parts, decay_parts = [], []
            for pp in range(pairs_per_group):
                p = g * pairs_per_group + pp
                h0 = HEAD_PAIR * p
                xp = x[:, p * LANES:(p + 1) * LANES]
                cols = [lane_bcast(acum, h0 + e) for e in range(HEAD_PAIR)]
                colc = jnp.where(low, cols[0], cols[1])
                ms = []
                for e in range(HEAD_PAIR):
                    h = h0 + e
                    seg = jnp.exp2(jnp.where(tri, cols[e] - acum_t[h:h + 1, :], NEG_BIG))
                    ms.append(cb * seg)
                m_cat = jnp.concatenate(ms, axis=1).astype(BF16)
                dtc = jnp.where(low, lane_bcast(dt, h0), lane_bcast(dt, h0 + 1))
                xdt = xp * dtc
                x_blk = jnp.concatenate([jnp.where(low, xdt, 0.0), jnp.where(low, 0.0, xdt)],
                                        axis=0).astype(BF16)
                y_diag = _dot(m_cat, x_blk)
                y_o = y_off[:, pp * LANES:(pp + 1) * LANES] * jnp.exp2(colc)
                y_parts.append(y_diag + y_o + dskip_ref[:, p * LANES:(p + 1) * LANES] * xp)
                last = colc[q - 1:q, :]
                xd_parts.append(xdt * jnp.exp2(last - colc))
                decay_parts.append(jnp.exp2(last))
            xd = jnp.concatenate(xd_parts, axis=1).astype(BF16)
            decay = jnp.concatenate(decay_parts, axis=1)
            state_ref[g] = st * decay + _dot(bg.T.astype(BF16), xd)

        z = z_ref[pl.ds(r0, q), :].astype(F32)
        outs = []
        for g in range(SSD_GROUPS):
            yg = jnp.concatenate(y_parts[g * pairs_per_group:(g + 1) * pairs_per_group], axis=1)
            yg = yg * _silu(z[:, g * gw:(g + 1) * gw])
            outs.append(yg * _rms_scale(yg) * g_ref[:, g * gw:(g + 1) * gw])
        y_ref[pl.ds(r0, q), :] = jnp.concatenate(outs, axis=1).astype(y_ref.dtype)

    for c in range(tb // q):
        chunk(c)


def _ssd(main3, dt3, conv_w, conv_b, dt_bias, a_log, d_skip, norm_g, *, tb):
    b, seq, _ = main3.shape
    z_blk = (3 * ATTN_WIDTH) // SSD_WIDTH
    xs_blk = (3 * ATTN_WIDTH + SSD_WIDTH) // SSD_WIDTH
    bc_blk = (3 * ATTN_WIDTH + 2 * SSD_WIDTH) // BC_WIDTH

    def const(shape):
        return pl.BlockSpec(shape, lambda i, j: (0,) * len(shape))

    return pl.pallas_call(
        _ssd_kernel,
        grid=(b, seq // tb),
        in_specs=[
            pl.BlockSpec((None, tb, SSD_WIDTH), lambda i, j: (i, j, z_blk)),
            pl.BlockSpec((None, tb, SSD_WIDTH), lambda i, j: (i, j, xs_blk)),
            pl.BlockSpec((None, tb, BC_WIDTH), lambda i, j: (i, j, bc_blk)),
            pl.BlockSpec((None, tb, LANES), lambda i, j: (i, j, 0)),
            const((SSD_CONV, CONV_CH)), const((1, CONV_CH)),
            const((1, LANES)), const((1, LANES)),
            const((1, SSD_WIDTH)), const((1, SSD_WIDTH)),
        ],
        out_specs=pl.BlockSpec((None, tb, SSD_WIDTH), lambda i, j: (i, j, 0)),
        out_shape=jax.ShapeDtypeStruct((b, seq, SSD_WIDTH), BF16),
        scratch_shapes=[
            pltpu.VMEM((2, SSD_CHUNK, CONV_CH), BF16),
            pltpu.VMEM((SSD_GROUPS, SSD_STATE, SSD_WIDTH // SSD_GROUPS), F32),
        ],
        compiler_params=_params(("parallel", "arbitrary")),
        name="ssd",
    )(main3, main3, main3, dt3, conv_w, conv_b, dt_bias, a_log, d_skip, norm_g)


def _out_proj_kernel(a_ref, y_ref, x_ref, g_ref, w_ref, o_ref):
    a = a_ref[...].astype(F32)
    an = (a * _rms_scale(a) * g_ref[...]).astype(BF16)
    mixed = _dot(an, w_ref[0:ATTN_WIDTH, :]) + _dot(y_ref[...], w_ref[ATTN_WIDTH:, :])
    o_ref[...] = x_ref[...] + mixed


def _out_proj(attn, y, x, g, w_all, layer, *, tm):
    m, d = x.shape
    return pl.pallas_call(
        _out_proj_kernel,
        grid=(m // tm,),
        in_specs=[
            pl.BlockSpec((tm, ATTN_WIDTH), lambda i: (i, 0)),
            pl.BlockSpec((tm, SSD_WIDTH), lambda i: (i, 0)),
            pl.BlockSpec((tm, d), lambda i: (i, 0)),
            _resident((1, ATTN_WIDTH)),
            _resident(w_all.shape[1:], layer),
        ],
        out_specs=pl.BlockSpec((tm, d), lambda i: (i, 0)),
        out_shape=jax.ShapeDtypeStruct((m, d), F32),
        compiler_params=_params(("parallel",)),
        name="out_proj",
    )(attn, y, x, g, w_all)


def _mlp_kernel(x_ref, g_ref, w1_ref, w2_ref, gf_ref, o_ref, hn_ref, *, final_norm):
    f = pl.program_id(1)

    @pl.when(f == 0)
    def _():
        x = x_ref[...]
        hn_ref[...] = (x * _rms_scale(x) * g_ref[...]).astype(BF16)
        o_ref[...] = x

    h = jnp.maximum(_dot(hn_ref[...], w1_ref[...]), 0.0)
    o_ref[...] += _dot((h * h).astype(BF16), w2_ref[...])

    if final_norm:
        @pl.when(f == pl.num_programs(1) - 1)
        def _():
            o = o_ref[...]
            o_ref[...] = o * _rms_scale(o) * gf_ref[...]


def _mlp(x, g, w1, w2, gf, *, tm, tf, final_norm):
    m, d = x.shape
    dff = w1.shape[1]
    return pl.pallas_call(
        functools.partial(_mlp_kernel, final_norm=final_norm),
        grid=(m // tm, dff // tf),
        in_specs=[
            pl.BlockSpec((tm, d), lambda i, j: (i, 0)),
            pl.BlockSpec((1, d), lambda i, j: (0, 0)),
            pl.BlockSpec((d, tf), lambda i, j: (0, j)),
            pl.BlockSpec((tf, d), lambda i, j: (j, 0)),
            pl.BlockSpec((1, d), lambda i, j: (0, 0)),
        ],
        out_specs=pl.BlockSpec((tm, d), lambda i, j: (i, 0)),
        out_shape=jax.ShapeDtypeStruct((m, d), F32),
        scratch_shapes=[pltpu.VMEM((tm, d), BF16)],
        compiler_params=_params(("parallel", "arbitrary")),
        name="mlp",
    )(x, g, w1, w2, gf)


def _alibi_slopes():
    return jnp.asarray(2.0 ** (-8.0 * (np.arange(ATTN_HEADS) + 1) / ATTN_HEADS), dtype=F32)


def _pad_lanes(v):
    return jnp.pad(v, ((0, 0), (0, LANES - v.shape[1])))


def kernel(x, ln1_g, w_in, conv_w, conv_b, dt_bias, a_log, d_skip, attn_norm_g, ssd_norm_g, w_out,
           ln2_g, w_mlp_in, w_mlp_out, final_norm_g):
    b, seq, d = x.shape
    m = b * seq
    slopes = _alibi_slopes()
    xf = x.reshape(m, d)
    w_in_bf = w_in.astype(BF16)
    w_dt_bf = jnp.pad(w_in[:, :, MAIN_WIDTH:], ((0, 0), (0, 0), (0, LANES - (w_in.shape[2] - MAIN_WIDTH)))).astype(BF16)
    w_out_bf = w_out.astype(BF16)
    for l in range(DEPTH):
        main, dt_raw = _in_proj(xf, ln1_g[l][None, :], w_in_bf, w_dt_bf, l,
                                tm=IN_PROJ_ROWS, tn=IN_PROJ_COLS)
        main3 = main.reshape(b, seq, MAIN_WIDTH)
        attn, w1_bf, w2_bf = _attention(slopes, main3, w_mlp_in, w_mlp_out, l)
        y = _ssd(
            main3, dt_raw.reshape(b, seq, LANES), conv_w[l], conv_b[l][None, :],
            _pad_lanes(dt_bias[l][None, :]), _pad_lanes(a_log[l][None, :]),
            jnp.repeat(d_skip[l], SSD_HEAD_DIM)[None, :], ssd_norm_g[l][None, :], tb=SSD_TOKENS)
        x1 = _out_proj(attn.reshape(m, ATTN_WIDTH), y.reshape(m, SSD_WIDTH), xf,
                       attn_norm_g[l][None, :], w_out_bf, l, tm=OUT_PROJ_ROWS)
        xf = _mlp(x1, ln2_g[l][None, :], w1_bf, w2_bf,
                  final_norm_g[None, :], tm=MLP_ROWS, tf=MLP_HIDDEN, final_norm=(l == DEPTH - 1))
    return xf.reshape(b, seq, d)
```

```python
import functools

import jax
import jax.numpy as jnp
import numpy as np
from jax import lax
from jax.experimental import pallas as pl
from jax.experimental.pallas import tpu as pltpu

D_MODEL = 2048
DEPTH = 2
HEAD_DIM = 64
ATTN_WIDTH = D_MODEL // 2
ATTN_HEADS = ATTN_WIDTH // HEAD_DIM
DILATED_BRANCHES = ((128, 1), (512, 4), (2048, 16))
ATTN_BLOCK = 128
SSD_WIDTH = D_MODEL // 2
SSD_HEAD_DIM = 64
SSD_HEADS = SSD_WIDTH // SSD_HEAD_DIM
SSD_GROUPS = 2
SSD_STATE = 128
SSD_CONV = 4
SSD_CHUNK = 128
BC_WIDTH = 2 * SSD_GROUPS * SSD_STATE
CONV_CH = SSD_WIDTH + BC_WIDTH
MAIN_WIDTH = 3 * ATTN_WIDTH + SSD_WIDTH + CONV_CH
D_FF = 4 * D_MODEL
NORM_EPS = 1e-5

LANES = 128
HEAD_PAIR = LANES // HEAD_DIM
NEG_BIG = -1e30
LOG2_E = 1.4426950408889634
Q_PRESCALE = HEAD_DIM ** -0.5 * LOG2_E
VMEM_LIMIT = 56 * 1024 * 1024

IN_PROJ_ROWS, IN_PROJ_COLS = 512, 512
OUT_PROJ_ROWS = 512
MLP_ROWS, MLP_HIDDEN = 1024, 512
SSD_TOKENS = 1024

F32 = jnp.float32
BF16 = jnp.bfloat16


def _dot(a, b):
    return jnp.dot(a, b, preferred_element_type=F32)


def _dot_nt(a, b):
    return lax.dot_general(a, b, (((1,), (1,)), ((), ())), preferred_element_type=F32)


def _rms_scale(x):
    return lax.rsqrt(jnp.mean(x * x, axis=-1, keepdims=True) + NORM_EPS)


def _silu(x):
    return x * (1.0 / (1.0 + jnp.exp2(x * (-LOG2_E))))


def _params(semantics):
    return pltpu.CompilerParams(dimension_semantics=semantics, vmem_limit_bytes=VMEM_LIMIT)


def _in_proj_kernel(x_ref, g_ref, w_ref, wdt_ref, main_ref, dt_ref, hn_ref, *, tn):
    x = x_ref[...]
    hn_ref[...] = (x * _rms_scale(x) * g_ref[...]).astype(BF16)
    dt_ref[...] = _dot(hn_ref[...], wdt_ref[...])
    for j in range(main_ref.shape[1] // tn):
        cols = slice(j * tn, (j + 1) * tn)
        acc = _dot(hn_ref[...], w_ref[:, cols])
        if (j + 1) * tn <= ATTN_WIDTH:
            acc = acc * Q_PRESCALE
        main_ref[:, cols] = acc.astype(main_ref.dtype)


def _resident(shape, layer=None):
    if layer is None:
        return pl.BlockSpec(shape, lambda i: (0,) * len(shape), pipeline_mode=pl.Buffered(1))
    return pl.BlockSpec((None,) + tuple(shape), lambda i: (layer,) + (0,) * len(shape),
                        pipeline_mode=pl.Buffered(1))


def _in_proj(x, g, w_all, w_dt, layer, *, tm, tn):
    m, d = x.shape
    n = MAIN_WIDTH
    assert m % tm == 0 and n % tn == 0
    return pl.pallas_call(
        functools.partial(_in_proj_kernel, tn=tn),
        grid=(m // tm,),
        in_specs=[
            pl.BlockSpec((tm, d), lambda i: (i, 0)),
            _resident((1, d)),
            _resident(w_all.shape[1:], layer),
            _resident((d, LANES), layer),
        ],
        out_specs=[
            pl.BlockSpec((tm, n), lambda i: (i, 0)),
            pl.BlockSpec((tm, LANES), lambda i: (i, 0)),
        ],
        out_shape=[
            jax.ShapeDtypeStruct((m, n), BF16),
            jax.ShapeDtypeStruct((m, LANES), F32),
        ],
        scratch_shapes=[pltpu.VMEM((tm, d), BF16)],
        compiler_params=_params(("parallel",)),
        name="in_proj",
    )(x, g, w_all, w_dt)


def _attn_kernel(slopes_ref, q_ref, k_ref, v_ref, w1_ref, w2_ref, o_ref, w1o_ref, w2o_ref,
                 qf_ref, kf_ref, vf_ref, qm_ref, km_ref, vm_ref, of_ref,
                 qd_ref, kd_ref, vd_ref, qh_ref, kh_ref, vh_ref, st0_ref, st1_ref, bias_ref):
    seq = q_ref.shape[0]
    blk = ATTN_BLOCK
    hp = pl.program_id(1)

    w1o_ref[...] = w1_ref[...].astype(BF16)
    w2o_ref[...] = w2_ref[...].astype(BF16)

    qf_ref[...] = q_ref[...].astype(F32)
    kf_ref[...] = k_ref[...].astype(F32)
    vf_ref[...] = v_ref[...].astype(F32)

    low = lax.broadcasted_iota(jnp.int32, (blk, LANES), 1) < HEAD_DIM
    ones_bf = jnp.ones((2 * blk, LANES), BF16)

    row = lax.broadcasted_iota(jnp.int32, (2 * blk, 2 * blk), 0)
    col = lax.broadcasted_iota(jnp.int32, (2 * blk, 2 * blk), 1)
    delta = (row & (blk - 1)) - col + blk
    slope = jnp.where(row < blk, slopes_ref[HEAD_PAIR * hp], slopes_ref[HEAD_PAIR * hp + 1])
    for t, (window, dil) in enumerate(DILATED_BRANCHES):
        valid = (delta >= 0) & (delta <= window // dil)
        bias = jnp.where(valid, -slope * (delta * dil).astype(F32) * LOG2_E, NEG_BIG)
        bias_ref[2 * t] = bias
        bias_ref[2 * t + 1] = jnp.where(col >= blk, bias, NEG_BIG)

    def pair_rows(x):
        return jnp.where(low, jnp.broadcast_to(x[:blk], (blk, LANES)),
                         jnp.broadcast_to(x[blk:], (blk, LANES)))

    def pair_halves(x):
        return jnp.where(low, x[:blk], x[blk:])

    zero_bf = jnp.zeros((), BF16)
    n_blocks = seq // blk

    def block_rows(i):
        start = i * blk
        return pl.ds(start if isinstance(start, int) else pl.multiple_of(start, blk), blk)

    (_, d_lo), (_, d_mid), (_, d_hi) = DILATED_BRANCHES
    ratio = d_hi // d_mid
    assert d_lo == 1 and d_hi == d_mid * ratio and seq % (blk * d_hi) == 0
    mid_len = seq // d_mid

    def log2(n):
        assert n & (n - 1) == 0
        return n.bit_length() - 1

    def split_block(idx, dil):
        nb = n_blocks // dil
        return idx >> log2(nb), idx & (nb - 1)

    def token_rows(idx, dil):
        r, n = split_block(idx, dil)
        return pl.ds(r + n * (blk * dil), blk, stride=dil) if dil > 1 else block_rows(idx)

    def hi_rows_in_mid(idx):
        r, n = split_block(idx, d_hi)
        return pl.ds((r & (d_mid - 1)) * mid_len + n * (blk * ratio) + (r >> log2(d_mid)), blk, stride=ratio)

    def gather_mid(idx):
        rows = token_rows(idx, d_mid)
        for f_ref, mid_ref, d_ref in ((qf_ref, qm_ref, qd_ref), (kf_ref, km_ref, kd_ref), (vf_ref, vm_ref, vd_ref)):
            x = f_ref[rows, :]
            mid_ref[block_rows(idx), :] = x
            d_ref[block_rows(idx), :] = x.astype(BF16)

    def gather_hi(idx):
        rows = hi_rows_in_mid(idx)
        for mid_ref, h_ref in ((qm_ref, qh_ref), (km_ref, kh_ref), (vm_ref, vh_ref)):
            h_ref[block_rows(idx), :] = mid_ref[rows, :].astype(BF16)

    def block_softmax(t, idx, q_src, k_src, v_src):
        first = split_block(idx, DILATED_BRANCHES[t][1])[1] == 0
        cur = block_rows(idx)
        prev = block_rows(idx if first else idx - 1)
        q = q_src[cur, :]
        q2 = jnp.concatenate([jnp.where(low, q, zero_bf), jnp.where(low, zero_bf, q)], axis=0)
        kk = jnp.concatenate([k_src[prev, :], k_src[cur, :]], axis=0)
        vv = jnp.concatenate([jnp.concatenate([v_src[prev, :], v_src[cur, :]], axis=0), ones_bf], axis=1)
        s = _dot_nt(q2, kk) + bias_ref[2 * t + int(first)]
        m = jnp.max(s, axis=-1, keepdims=True)
        p = jnp.exp2(s - m)
        pv = _dot(p.astype(BF16), vv)
        return pair_rows(m), pair_halves(pv[:, LANES:]), pair_halves(pv[:, :LANES])

    def combine(a, b):
        (ma, la, acca), (mb, lb, accb) = a, b
        m = jnp.maximum(ma, mb)
        wa, wb = jnp.exp2(ma - m), jnp.exp2(mb - m)
        return m, wa * la + wb * lb, wa * acca + wb * accb

    def load_state(st_ref, rows):
        return tuple(st_ref[i, rows, :] for i in range(3))

    def store_state(st_ref, rows, state):
        for i, x in enumerate(state):
            st_ref[i, rows, :] = x

    def branch0(idx):
        store_state(st0_ref, block_rows(idx), block_softmax(0, idx, q_ref, k_ref, v_ref))
        gather_mid(idx)

    def branch1(idx):
        mine = block_softmax(1, idx, qd_ref, kd_ref, vd_ref)
        store_state(st1_ref, block_rows(idx), combine(load_state(st0_ref, token_rows(idx, d_mid)), mine))
        gather_hi(idx)

    def branch2(idx):
        mine = block_softmax(2, idx, qh_ref, kh_ref, vh_ref)
        _, l_all, acc_all = combine(load_state(st1_ref, hi_rows_in_mid(idx)), mine)
        of_ref[token_rows(idx, d_hi), :] = acc_all / l_all

    for branch in (branch0, branch1, branch2):
        for idx in range(n_blocks):
            branch(idx)
    o_ref[...] = of_ref[...].astype(o_ref.dtype)


def _attention(slopes, main3, w1_all, w2_all, layer):
    b, seq, _ = main3.shape
    n_pairs = ATTN_WIDTH // LANES
    blk = ATTN_BLOCK
    n_steps = b * n_pairs
    w1_shape, w2_shape = w1_all.shape[1:], w2_all.shape[1:]
    r1, r2 = w1_shape[0] // n_steps, w2_shape[0] // n_steps
    assert r1 * n_steps == w1_shape[0] and r2 * n_steps == w2_shape[0]

    def spec(offset):
        return pl.BlockSpec((None, seq, LANES), lambda i, j: (i, 0, offset + j))

    return pl.pallas_call(
        _attn_kernel,
        grid=(b, n_pairs),
        in_specs=[
            pl.BlockSpec(memory_space=pltpu.SMEM),
            spec(0), spec(n_pairs), spec(2 * n_pairs),
            pl.BlockSpec((None, r1, w1_shape[1]), lambda i, j: (layer, i * n_pairs + j, 0)),
            pl.BlockSpec((None, r2, w2_shape[1]), lambda i, j: (layer, i * n_pairs + j, 0)),
        ],
        out_specs=[
            pl.BlockSpec((None, seq, LANES), lambda i, j: (i, 0, j)),
            pl.BlockSpec((r1, w1_shape[1]), lambda i, j: (i * n_pairs + j, 0)),
            pl.BlockSpec((r2, w2_shape[1]), lambda i, j: (i * n_pairs + j, 0)),
        ],
        out_shape=[
            jax.ShapeDtypeStruct((b, seq, ATTN_WIDTH), BF16),
            jax.ShapeDtypeStruct(w1_shape, BF16),
            jax.ShapeDtypeStruct(w2_shape, BF16),
        ],
        scratch_shapes=[pltpu.VMEM((seq, LANES), F32)] * 7
        + [pltpu.VMEM((seq, LANES), BF16)] * 6
        + [pltpu.VMEM((3, seq, LANES), F32)] * 2
        + [pltpu.VMEM((2 * len(DILATED_BRANCHES), 2 * blk, 2 * blk), F32)],
        compiler_params=_params(("parallel", "parallel")),
        name="attention",
    )(slopes, main3, main3, main3, w1_all, w2_all)


def _ssd_kernel(z_ref, xs_ref, bc_ref, dt_ref, cw_ref, cb_ref, dtb_ref, alog_ref, dskip_ref, g_ref,
                y_ref, carry_ref, state_ref):
    tb = xs_ref.shape[0]
    q = SSD_CHUNK
    gw = SSD_WIDTH // SSD_GROUPS
    pairs_per_group = gw // LANES
    conv_cols = 512

    step = pl.program_id(1)
    slot = step & 1

    @pl.when(step == 0)
    def _():
        carry_ref[0] = jnp.zeros((q, CONV_CH), BF16)
        state_ref[...] = jnp.zeros(state_ref.shape, F32)

    carry_ref[1 - slot, :, 0:SSD_WIDTH] = xs_ref[tb - q:tb, :]
    carry_ref[1 - slot, :, SSD_WIDTH:CONV_CH] = bc_ref[tb - q:tb, :]

    low = lax.broadcasted_iota(jnp.int32, (q, LANES), 1) < SSD_HEAD_DIM
    ri = lax.broadcasted_iota(jnp.int32, (q, q), 0)
    ci = lax.broadcasted_iota(jnp.int32, (q, q), 1)
    tri = ri >= ci
    tri_bf = jnp.where(tri, 1.0, 0.0).astype(BF16)
    a_neg = -jnp.exp(alog_ref[...])

    sr = lax.broadcasted_iota(jnp.int32, (SSD_CONV * q, 2 * q), 0)
    sc = lax.broadcasted_iota(jnp.int32, (SSD_CONV * q, 2 * q), 1)
    src_row = (sr & (q - 1)) + (sr >> (q.bit_length() - 1)) + (q - (SSD_CONV - 1))
    shift = jnp.where(sc == src_row, 1.0, 0.0).astype(BF16)

    def cumsum_rows(v):
        hi = v.astype(BF16)
        r1 = v - hi.astype(F32)
        mid = r1.astype(BF16)
        lo = (r1 - mid.astype(F32)).astype(BF16)
        return _dot(tri_bf, hi) + _dot(tri_bf, mid) + _dot(tri_bf, lo)

    def lane_bcast(v, h):
        return jnp.broadcast_to(v[:, h:h + 1], (q, LANES))

    def chunk(c):
        r0 = c * q
        u_parts = []
        for j in range(CONV_CH // conv_cols):
            cols = slice(j * conv_cols, (j + 1) * conv_cols)
            src_ref, off = (xs_ref, 0) if j * conv_cols < SSD_WIDTH else (bc_ref, SSD_WIDTH)
            src_cols = slice(j * conv_cols - off, (j + 1) * conv_cols - off)
            prev = carry_ref[slot, :, cols] if c == 0 else src_ref[r0 - q:r0, src_cols]
            window = jnp.concatenate([prev, src_ref[r0:r0 + q, src_cols]], axis=0)
            taps = _dot(shift, window)
            uj = cb_ref[:, cols]
            for k in range(SSD_CONV):
                uj = uj + cw_ref[k:k + 1, cols] * taps[k * q:(k + 1) * q]
            u_parts.append(_silu(uj))
        u = jnp.concatenate(u_parts, axis=1)
        x = u[:, :SSD_WIDTH]
        dt_in = dt_ref[pl.ds(r0, q), :] + dtb_ref[...]
        dt = jnp.maximum(dt_in, 0.0) + jnp.log(1.0 + jnp.exp(-jnp.abs(dt_in)))
        acum = cumsum_rows(dt * a_neg) * LOG2_E
        acum_t = acum.T

        y_parts = []
        for g in range(SSD_GROUPS):
            bg = u[:, SSD_WIDTH + g * SSD_STATE:SSD_WIDTH + (g + 1) * SSD_STATE]
            cg = u[:, SSD_WIDTH + (SSD_GROUPS + g) * SSD_STATE:SSD_WIDTH + (SSD_GROUPS + g + 1) * SSD_STATE]
            cg_bf = cg.astype(BF16)
            cb = _dot_nt(cg_bf, bg.astype(BF16))
            st = state_ref[g]
            y_off = _dot(cg_bf, st.astype(BF16))
            xd_parts, decay_parts = [], []
            for pp in range(pairs_per_group):
                p = g * pairs_per_group + pp
                h0 = HEAD_PAIR * p
                xp = x[:, p * LANES:(p + 1) * LANES]
                cols = [lane_bcast(acum, h0 + e) for e in range(HEAD_PAIR)]
                colc = jnp.where(low, cols[0], cols[1])
                ms = []
                for e in range(HEAD_PAIR):
                    h = h0 + e
                    seg = jnp.exp2(jnp.where(tri, cols[e] - acum_t[h:h + 1, :], NEG_BIG))
                    ms.append(cb * seg)
                m_cat = jnp.concatenate(ms, axis=1).astype(BF16)
                dtc = jnp.where(low, lane_bcast(dt, h0), lane_bcast(dt, h0 + 1))
                xdt = xp * dtc
                x_blk = jnp.concatenate([jnp.where(low, xdt, 0.0), jnp.where(low, 0.0, xdt)],
                                        axis=0).astype(BF16)
                y_diag = _dot(m_cat, x_blk)
                y_o = y_off[:, pp * LANES:(pp + 1) * LANES] * jnp.exp2(colc)
                y_parts.append(y_diag + y_o + dskip_ref[:, p * LANES:(p + 1) * LANES] * xp)
                last = colc[q - 1:q, :]
                xd_parts.append(xdt * jnp.exp2(last - colc))
                decay_parts.append(jnp.exp2(last))
            xd = jnp.concatenate(xd_parts, axis=1).astype(BF16)
            decay = jnp.concatenate(decay_parts, axis=1)
            state_ref[g] = st * decay + _dot(bg.T.astype(BF16), xd)

        z = z_ref[pl.ds(r0, q), :].astype(F32)
        outs = []
        for g in range(SSD_GROUPS):
            yg = jnp.concatenate(y_parts[g * pairs_per_group:(g + 1) * pairs_per_group], axis=1)
            yg = yg * _silu(z[:, g * gw:(g + 1) * gw])
            outs.append(yg * _rms_scale(yg) * g_ref[:, g * gw:(g + 1) * gw])
        y_ref[pl.ds(r0, q), :] = jnp.concatenate(outs, axis=1).astype(y_ref.dtype)

    for c in range(tb // q):
        chunk(c)


def _ssd(main3, dt3, conv_w, conv_b, dt_bias, a_log, d_skip, norm_g, *, tb):
    b, seq, _ = main3.shape
    z_blk = (3 * ATTN_WIDTH) // SSD_WIDTH
    xs_blk = (3 * ATTN_WIDTH + SSD_WIDTH) // SSD_WIDTH
    bc_blk = (3 * ATTN_WIDTH + 2 * SSD_WIDTH) // BC_WIDTH

    def const(shape):
        return pl.BlockSpec(shape, lambda i, j: (0,) * len(shape))

    return pl.pallas_call(
        _ssd_kernel,
        grid=(b, seq // tb),
        in_specs=[
            pl.BlockSpec((None, tb, SSD_WIDTH), lambda i, j: (i, j, z_blk)),
            pl.BlockSpec((None, tb, SSD_WIDTH), lambda i, j: (i, j, xs_blk)),
            pl.BlockSpec((None, tb, BC_WIDTH), lambda i, j: (i, j, bc_blk)),
            pl.BlockSpec((None, tb, LANES), lambda i, j: (i, j, 0)),
            const((SSD_CONV, CONV_CH)), const((1, CONV_CH)),
            const((1, LANES)), const((1, LANES)),
            const((1, SSD_WIDTH)), const((1, SSD_WIDTH)),
        ],
        out_specs=pl.BlockSpec((None, tb, SSD_WIDTH), lambda i, j: (i, j, 0)),
        out_shape=jax.ShapeDtypeStruct((b, seq, SSD_WIDTH), BF16),
        scratch_shapes=[
            pltpu.VMEM((2, SSD_CHUNK, CONV_CH), BF16),
            pltpu.VMEM((SSD_GROUPS, SSD_STATE, SSD_WIDTH // SSD_GROUPS), F32),
        ],
        compiler_params=_params(("parallel", "arbitrary")),
        name="ssd",
    )(main3, main3, main3, dt3, conv_w, conv_b, dt_bias, a_log, d_skip, norm_g)


def _out_proj_kernel(a_ref, y_ref, x_ref, g_ref, w_ref, o_ref):
    a = a_ref[...].astype(F32)
    an = (a * _rms_scale(a) * g_ref[...]).astype(BF16)
    mixed = _dot(an, w_ref[0:ATTN_WIDTH, :]) + _dot(y_ref[...], w_ref[ATTN_WIDTH:, :])
    o_ref[...] = x_ref[...] + mixed


def _out_proj(attn, y, x, g, w_all, layer, *, tm):
    m, d = x.shape
    return pl.pallas_call(
        _out_proj_kernel,
        grid=(m // tm,),
        in_specs=[
            pl.BlockSpec((tm, ATTN_WIDTH), lambda i: (i, 0)),
            pl.BlockSpec((tm, SSD_WIDTH), lambda i: (i, 0)),
            pl.BlockSpec((tm, d), lambda i: (i, 0)),
            _resident((1, ATTN_WIDTH)),
            _resident(w_all.shape[1:], layer),
        ],
        out_specs=pl.BlockSpec((tm, d), lambda i: (i, 0)),
        out_shape=jax.ShapeDtypeStruct((m, d), F32),
        compiler_params=_params(("parallel",)),
        name="out_proj",
    )(attn, y, x, g, w_all)


def _mlp_kernel(x_ref, g_ref, w1_ref, w2_ref, gf_ref, o_ref, hn_ref, *, final_norm):
    f = pl.program_id(1)

    @pl.when(f == 0)
    def _():
        x = x_ref[...]
        hn_ref[...] = (x * _rms_scale(x) * g_ref[...]).astype(BF16)
        o_ref[...] = x

    h = jnp.maximum(_dot(hn_ref[...], w1_ref[...]), 0.0)
    o_ref[...] += _dot((h * h).astype(BF16), w2_ref[...])

    if final_norm:
        @pl.when(f == pl.num_programs(1) - 1)
        def _():
            o = o_ref[...]
            o_ref[...] = o * _rms_scale(o) * gf_ref[...]


def _mlp(x, g, w1, w2, gf, *, tm, tf, final_norm):
    m, d = x.shape
    dff = w1.shape[1]
    return pl.pallas_call(
        functools.partial(_mlp_kernel, final_norm=final_norm),
        grid=(m // tm, dff // tf),
        in_specs=[
            pl.BlockSpec((tm, d), lambda i, j: (i, 0)),
            pl.BlockSpec((1, d), lambda i, j: (0, 0)),
            pl.BlockSpec((d, tf), lambda i, j: (0, j)),
            pl.BlockSpec((tf, d), lambda i, j: (j, 0)),
            pl.BlockSpec((1, d), lambda i, j: (0, 0)),
        ],
        out_specs=pl.BlockSpec((tm, d), lambda i, j: (i, 0)),
        out_shape=jax.ShapeDtypeStruct((m, d), F32),
        scratch_shapes=[pltpu.VMEM((tm, d), BF16)],
        compiler_params=_params(("parallel", "arbitrary")),
        name="mlp",
    )(x, g, w1, w2, gf)


def _alibi_slopes():
    return jnp.asarray(2.0 ** (-8.0 * (np.arange(ATTN_HEADS) + 1) / ATTN_HEADS), dtype=F32)


def _pad_lanes(v):
    return jnp.pad(v, ((0, 0), (0, LANES - v.shape[1])))


def kernel(x, ln1_g, w_in, conv_w, conv_b, dt_bias, a_log, d_skip, attn_norm_g, ssd_norm_g, w_out,
           ln2_g, w_mlp_in, w_mlp_out, final_norm_g):
    b, seq, d = x.shape
    m = b * seq
    slopes = _alibi_slopes()
    xf = x.reshape(m, d)
    w_in_bf = w_in.astype(BF16)
    w_dt_bf = jnp.pad(w_in[:, :, MAIN_WIDTH:], ((0, 0), (0, 0), (0, LANES - (w_in.shape[2] - MAIN_WIDTH)))).astype(BF16)
    w_out_bf = w_out.astype(BF16)
    for l in range(DEPTH):
        main, dt_raw = _in_proj(xf, ln1_g[l][None, :], w_in_bf, w_dt_bf, l,
                                tm=IN_PROJ_ROWS, tn=IN_PROJ_COLS)
        main3 = main.reshape(b, seq, MAIN_WIDTH)
        attn, w1_bf, w2_bf = _attention(slopes, main3, w_mlp_in, w_mlp_out, l)
        y = _ssd(
            main3, dt_raw.reshape(b, seq, LANES), conv_w[l], conv_b[l][None, :],
            _pad_lanes(dt_bias[l][None, :]), _pad_lanes(a_log[l][None, :]),
            jnp.repeat(d_skip[l], SSD_HEAD_DIM)[None, :], ssd_norm_g[l][None, :], tb=SSD_TOKENS)
        x1 = _out_proj(attn.reshape(m, ATTN_WIDTH), y.reshape(m, SSD_WIDTH), xf,
                       attn_norm_g[l][None, :], w_out_bf, l, tm=OUT_PROJ_ROWS)
        xf = _mlp(x1, ln2_g[l][None, :], w1_bf, w2_bf,
                  final_norm_g[None, :], tm=MLP_ROWS, tf=MLP_HIDDEN, final_norm=(l == DEPTH - 1))
    return xf.reshape(b, seq, d)
```

```python
import functools

import jax
import jax.numpy as jnp
import numpy as np
from jax import lax
from jax.experimental import pallas as pl
from jax.experimental.pallas import tpu as pltpu

D_MODEL = 2048
DEPTH = 2
HEAD_DIM = 64
ATTN_WIDTH = D_MODEL // 2
ATTN_HEADS = ATTN_WIDTH // HEAD_DIM
DILATED_BRANCHES = ((128, 1), (512, 4), (2048, 16))
ATTN_BLOCK = 128
SSD_WIDTH = D_MODEL // 2
SSD_HEAD_DIM = 64
SSD_HEADS = SSD_WIDTH // SSD_HEAD_DIM
SSD_GROUPS = 2
SSD_STATE = 128
SSD_CONV = 4
SSD_CHUNK = 128
BC_WIDTH = 2 * SSD_GROUPS * SSD_STATE
CONV_CH = SSD_WIDTH + BC_WIDTH
MAIN_WIDTH = 3 * ATTN_WIDTH + SSD_WIDTH + CONV_CH
D_FF = 4 * D_MODEL
NORM_EPS = 1e-5

LANES = 128
HEAD_PAIR = LANES // HEAD_DIM
NEG_BIG = -1e30
LOG2_E = 1.4426950408889634
Q_PRESCALE = HEAD_DIM ** -0.5 * LOG2_E
VMEM_LIMIT = 56 * 1024 * 1024

IN_PROJ_ROWS, IN_PROJ_COLS = 512, 512
OUT_PROJ_ROWS = 512
MLP_ROWS, MLP_HIDDEN = 1024, 512

F32 = jnp.float32
BF16 = jnp.bfloat16


def _dot(a, b):
    return jnp.dot(a, b, preferred_element_type=F32)


def _dot_nt(a, b):
    return lax.dot_general(a, b, (((1,), (1,)), ((), ())), preferred_element_type=F32)


def _rms_scale(x):
    return lax.rsqrt(jnp.mean(x * x, axis=-1, keepdims=True) + NORM_EPS)


def _silu(x):
    return x * (1.0 / (1.0 + jnp.exp2(x * (-LOG2_E))))


def _params(semantics):
    return pltpu.CompilerParams(dimension_semantics=semantics, vmem_limit_bytes=VMEM_LIMIT)


def _in_proj_kernel(x_ref, g_ref, w_ref, wdt_ref, main_ref, dt_ref, hn_ref, *, tn):
    x = x_ref[...]
    hn_ref[...] = (x * _rms_scale(x) * g_ref[...]).astype(BF16)
    dt_ref[...] = _dot(hn_ref[...], wdt_ref[...])
    for j in range(main_ref.shape[1] // tn):
        cols = slice(j * tn, (j + 1) * tn)
        acc = _dot(hn_ref[...], w_ref[:, cols])
        if (j + 1) * tn <= ATTN_WIDTH:
            acc = acc * Q_PRESCALE
        main_ref[:, cols] = acc.astype(main_ref.dtype)


def _resident(shape, layer=None):
    if layer is None:
        return pl.BlockSpec(shape, lambda i: (0,) * len(shape), pipeline_mode=pl.Buffered(1))
    return pl.BlockSpec((None,) + tuple(shape), lambda i: (layer,) + (0,) * len(shape),
                        pipeline_mode=pl.Buffered(1))


def _in_proj(x, g, w_all, w_dt, layer, *, tm, tn):
    m, d = x.shape
    n = MAIN_WIDTH
    assert m % tm == 0 and n % tn == 0
    return pl.pallas_call(
        functools.partial(_in_proj_kernel, tn=tn),
        grid=(m // tm,),
        in_specs=[
            pl.BlockSpec((tm, d), lambda i: (i, 0)),
            _resident((1, d)),
            _resident(w_all.shape[1:], layer),
            _resident((d, LANES), layer),
        ],
        out_specs=[
            pl.BlockSpec((tm, n), lambda i: (i, 0)),
            pl.BlockSpec((tm, LANES), lambda i: (i, 0)),
        ],
        out_shape=[
            jax.ShapeDtypeStruct((m, n), BF16),
            jax.ShapeDtypeStruct((m, LANES), F32),
        ],
        scratch_shapes=[pltpu.VMEM((tm, d), BF16)],
        compiler_params=_params(("parallel",)),
        name="in_proj",
    )(x, g, w_all, w_dt)


def _attn_kernel(slopes_ref, q_ref, k_ref, v_ref, w1_ref, w2_ref, o_ref, w1o_ref, w2o_ref,
                 qf_ref, kf_ref, vf_ref, qm_ref, km_ref, vm_ref,
                 qd_ref, kd_ref, vd_ref, qh_ref, kh_ref, vh_ref, st0_ref, bias_ref):
    seq = q_ref.shape[0]
    blk = ATTN_BLOCK
    hp = pl.program_id(1)

    w1o_ref[...] = w1_ref[...].astype(BF16)
    w2o_ref[...] = w2_ref[...].astype(BF16)

    qf_ref[...] = q_ref[...].astype(F32)
    kf_ref[...] = k_ref[...].astype(F32)
    vf_ref[...] = v_ref[...].astype(F32)

    low = lax.broadcasted_iota(jnp.int32, (blk, LANES), 1) < HEAD_DIM
    ones_bf = jnp.ones((2 * blk, LANES), BF16)

    row = lax.broadcasted_iota(jnp.int32, (2 * blk, 2 * blk), 0)
    col = lax.broadcasted_iota(jnp.int32, (2 * blk, 2 * blk), 1)
    delta = (row & (blk - 1)) - col + blk
    slope = jnp.where(row < blk, slopes_ref[HEAD_PAIR * hp], slopes_ref[HEAD_PAIR * hp + 1])
    for t, (window, dil) in enumerate(DILATED_BRANCHES):
        valid = (delta >= 0) & (delta <= window // dil)
        bias = jnp.where(valid, -slope * (delta * dil).astype(F32) * LOG2_E, NEG_BIG)
        bias_ref[2 * t] = bias
        bias_ref[2 * t + 1] = jnp.where(col >= blk, bias, NEG_BIG)

    def pair_rows(x):
        return jnp.where(low, jnp.broadcast_to(x[:blk], (blk, LANES)),
                         jnp.broadcast_to(x[blk:], (blk, LANES)))

    def pair_halves(x):
        return jnp.where(low, x[:blk], x[blk:])

    zero_bf = jnp.zeros((), BF16)
    n_blocks = seq // blk

    def block_rows(i):
        start = i * blk
        return pl.ds(start if isinstance(start, int) else pl.multiple_of(start, blk), blk)

    (_, d_lo), (_, d_mid), (_, d_hi) = DILATED_BRANCHES
    ratio = d_hi // d_mid
    assert d_lo == 1 and d_hi == d_mid * ratio and seq % (blk * d_hi) == 0
    mid_len = seq // d_mid

    def log2(n):
        assert n & (n - 1) == 0
        return n.bit_length() - 1

    def split_block(idx, dil):
        nb = n_blocks // dil
        return idx >> log2(nb), idx & (nb - 1)

    def token_rows(idx, dil):
        r, n = split_block(idx, dil)
        return pl.ds(r + n * (blk * dil), blk, stride=dil) if dil > 1 else block_rows(idx)

    def hi_rows_in_mid(idx):
        r, n = split_block(idx, d_hi)
        return pl.ds((r & (d_mid - 1)) * mid_len + n * (blk * ratio) + (r >> log2(d_mid)), blk, stride=ratio)

    def gather_mid(idx):
        rows = token_rows(idx, d_mid)
        for f_ref, mid_ref, d_ref in ((qf_ref, qm_ref, qd_ref), (kf_ref, km_ref, kd_ref), (vf_ref, vm_ref, vd_ref)):
            x = f_ref[rows, :]
            mid_ref[block_rows(idx), :] = x
            d_ref[block_rows(idx), :] = x.astype(BF16)

    def gather_hi(idx):
        rows = hi_rows_in_mid(idx)
        for mid_ref, h_ref in ((qm_ref, qh_ref), (km_ref, kh_ref), (vm_ref, vh_ref)):
            h_ref[block_rows(idx), :] = mid_ref[rows, :].astype(BF16)

    def block_softmax(t, idx, q_src, k_src, v_src):
        first = split_block(idx, DILATED_BRANCHES[t][1])[1] == 0
        cur = block_rows(idx)
        prev = block_rows(idx if first else idx - 1)
        q = q_src[cur, :]
        q2 = jnp.concatenate([jnp.where(low, q, zero_bf), jnp.where(low, zero_bf, q)], axis=0)
        kk = jnp.concatenate([k_src[prev, :], k_src[cur, :]], axis=0)
        vv = jnp.concatenate([jnp.concatenate([v_src[prev, :], v_src[cur, :]], axis=0), ones_bf], axis=1)
        s = _dot_nt(q2, kk) + bias_ref[2 * t + int(first)]
        m = jnp.max(s, axis=-1, keepdims=True)
        p = jnp.exp2(s - m)
        pv = _dot(p.astype(BF16), vv)
        return pair_rows(m), pair_halves(pv[:, LANES:]), pair_halves(pv[:, :LANES])

    def combine(a, b):
        (ma, la, acca), (mb, lb, accb) = a, b
        m = jnp.maximum(ma, mb)
        wa, wb = jnp.exp2(ma - m), jnp.exp2(mb - m)
        return m, wa * la + wb * lb, wa * acca + wb * accb

    def load_state(st, rows):
        return tuple(ref[rows, :] for ref in st)

    def store_state(st, rows, state):
        for ref, x in zip(st, state):
            ref[rows, :] = x

    st0 = tuple(st0_ref.at[i] for i in range(3))
    st1 = (qf_ref, kf_ref, vf_ref)
    of_ref = qm_ref

    def branch0(idx):
        store_state(st0, block_rows(idx), block_softmax(0, idx, q_ref, k_ref, v_ref))
        gather_mid(idx)

    def branch1(idx):
        mine = block_softmax(1, idx, qd_ref, kd_ref, vd_ref)
        store_state(st1, block_rows(idx), combine(load_state(st0, token_rows(idx, d_mid)), mine))
        gather_hi(idx)

    def branch2(idx):
        mine = block_softmax(2, idx, qh_ref, kh_ref, vh_ref)
        _, l_all, acc_all = combine(load_state(st1, hi_rows_in_mid(idx)), mine)
        of_ref[token_rows(idx, d_hi), :] = acc_all / l_all

    for branch in (branch0, branch1, branch2):
        for idx in range(n_blocks):
            branch(idx)
    o_ref[...] = of_ref[...].astype(o_ref.dtype)


def _attention_parts(slopes, main3, w1_all, w2_all, layer):
    b, seq, _ = main3.shape
    n_pairs = ATTN_WIDTH // LANES
    blk = ATTN_BLOCK
    n_steps = b * n_pairs
    w1_shape, w2_shape = w1_all.shape[1:], w2_all.shape[1:]
    r1, r2 = w1_shape[0] // n_steps, w2_shape[0] // n_steps
    assert r1 * n_steps == w1_shape[0] and r2 * n_steps == w2_shape[0]

    def spec(offset):
        return pl.BlockSpec((None, seq, LANES), lambda i, j: (i, 0, offset + j))

    return dict(
        operands=[slopes, main3, main3, main3, w1_all, w2_all],
        in_specs=[
            pl.BlockSpec(memory_space=pltpu.SMEM),
            spec(0), spec(n_pairs), spec(2 * n_pairs),
            pl.BlockSpec((None, r1, w1_shape[1]), lambda i, j: (layer, i * n_pairs + j, 0)),
            pl.BlockSpec((None, r2, w2_shape[1]), lambda i, j: (layer, i * n_pairs + j, 0)),
        ],
        out_specs=[
            pl.BlockSpec((None, seq, LANES), lambda i, j: (i, 0, j)),
            pl.BlockSpec((r1, w1_shape[1]), lambda i, j: (i * n_pairs + j, 0)),
            pl.BlockSpec((r2, w2_shape[1]), lambda i, j: (i * n_pairs + j, 0)),
        ],
        out_shape=[
            jax.ShapeDtypeStruct((b, seq, ATTN_WIDTH), BF16),
            jax.ShapeDtypeStruct(w1_shape, BF16),
            jax.ShapeDtypeStruct(w2_shape, BF16),
        ],
        scratch_shapes=[pltpu.VMEM((seq, LANES), F32)] * 6
        + [pltpu.VMEM((seq, LANES), BF16)] * 6
        + [pltpu.VMEM((3, seq, LANES), F32)]
        + [pltpu.VMEM((2 * len(DILATED_BRANCHES), 2 * blk, 2 * blk), F32)],
        grid=(b, n_pairs),
    )


def _ssd_kernel(z_ref, xs_ref, bc_ref, dt_ref, cw_ref, cb_ref, dtb_ref, alog_ref, dskip_ref, g_ref,
                y_ref, carry_ref, state_ref):
    tb = xs_ref.shape[0]
    q = SSD_CHUNK
    gw = SSD_WIDTH // SSD_GROUPS
    pairs_per_group = gw // LANES
    conv_cols = 512

    step = pl.program_id(1)
    slot = step & 1

    @pl.when(step == 0)
    def _():
        carry_ref[0] = jnp.zeros((q, CONV_CH), BF16)
        state_ref[...] = jnp.zeros(state_ref.shape, F32)

    carry_ref[1 - slot, :, 0:SSD_WIDTH] = xs_ref[tb - q:tb, :]
    carry_ref[1 - slot, :, SSD_WIDTH:CONV_CH] = bc_ref[tb - q:tb, :]

    low = lax.broadcasted_iota(jnp.int32, (q, LANES), 1) < SSD_HEAD_DIM
    ri = lax.broadcasted_iota(jnp.int32, (q, q), 0)
    ci = lax.broadcasted_iota(jnp.int32, (q, q), 1)
    tri = ri >= ci
    tri_bf = jnp.where(tri, 1.0, 0.0).astype(BF16)
    a_neg = -jnp.exp(alog_ref[...])

    sr = lax.broadcasted_iota(jnp.int32, (SSD_CONV * q, 2 * q), 0)
    sc = lax.broadcasted_iota(jnp.int32, (SSD_CONV * q, 2 * q), 1)
    src_row = (sr & (q - 1)) + (sr >> (q.bit_length() - 1)) + (q - (SSD_CONV - 1))
    shift = jnp.where(sc == src_row, 1.0, 0.0).astype(BF16)

    def cumsum_rows(v):
        hi = v.astype(BF16)
        r1 = v - hi.astype(F32)
        mid = r1.astype(BF16)
        lo = (r1 - mid.astype(F32)).astype(BF16)
        return _dot(tri_bf, hi) + _dot(tri_bf, mid) + _dot(tri_bf, lo)

    def lane_bcast(v, h):
        return jnp.broadcast_to(v[:, h:h + 1], (q, LANES))

    def chunk(c):
        r0 = c * q
        u_parts = []
        for j in range(CONV_CH // conv_cols):
            cols = slice(j * conv_cols, (j + 1) * conv_cols)
            src_ref, off = (xs_ref, 0) if j * conv_cols < SSD_WIDTH else (bc_ref, SSD_WIDTH)
            src_cols = slice(j * conv_cols - off, (j + 1) * conv_cols - off)
            prev = carry_ref[slot, :, cols] if c == 0 else src_ref[r0 - q:r0, src_cols]
            window = jnp.concatenate([prev, src_ref[r0:r0 + q, src_cols]], axis=0)
            taps = _dot(shift, window)
            uj = cb_ref[:, cols]
            for k in range(SSD_CONV):
                uj = uj + cw_ref[k:k + 1, cols] * taps[k * q:(k + 1) * q]
            u_parts.append(_silu(uj))
        u = jnp.concatenate(u_parts, axis=1)
        x = u[:, :SSD_WIDTH]
        dt_in = dt_ref[pl.ds(r0, q), :] + dtb_ref[...]
        dt = jnp.maximum(dt_in, 0.0) + jnp.log(1.0 + jnp.exp(-jnp.abs(dt_in)))
        acum = cumsum_rows(dt * a_neg) * LOG2_E
        acum_t = acum.T

        y_parts = []
        for g in range(SSD_GROUPS):
            bg = u[:, SSD_WIDTH + g * SSD_STATE:SSD_WIDTH + (g + 1) * SSD_STATE]
            cg = u[:, SSD_WIDTH + (SSD_GROUPS + g) * SSD_STATE:SSD_WIDTH + (SSD_GROUPS + g + 1) * SSD_STATE]
            cg_bf = cg.astype(BF16)
            cb = _dot_nt(cg_bf, bg.astype(BF16))
            st = state_ref[g]
            y_off = _dot(cg_bf, st.astype(BF16))
            xd_parts, decay_parts = [], []
            for pp in range(pairs_per_group):
                p = g * pairs_per_group + pp
                h0 = HEAD_PAIR * p
                xp = x[:, p * LANES:(p + 1) * LANES]
                cols = [lane_bcast(acum, h0 + e) for e in range(HEAD_PAIR)]
                colc = jnp.where(low, cols[0], cols[1])
                ms = []
                for e in range(HEAD_PAIR):
                    h = h0 + e
                    seg = jnp.exp2(jnp.where(tri, cols[e] - acum_t[h:h + 1, :], NEG_BIG))
                    ms.append(cb * seg)
                m_cat = jnp.concatenate(ms, axis=1).astype(BF16)
                dtc = jnp.where(low, lane_bcast(dt, h0), lane_bcast(dt, h0 + 1))
                xdt = xp * dtc
                x_blk = jnp.concatenate([jnp.where(low, xdt, 0.0), jnp.where(low, 0.0, xdt)],
                                        axis=0).astype(BF16)
                y_diag = _dot(m_cat, x_blk)
                y_o = y_off[:, pp * LANES:(pp + 1) * LANES] * jnp.exp2(colc)
                y_parts.append(y_diag + y_o + dskip_ref[:, p * LANES:(p + 1) * LANES] * xp)
                last = colc[q - 1:q, :]
                xd_parts.append(xdt * jnp.exp2(last - colc))
                decay_parts.append(jnp.exp2(last))
            xd = jnp.concatenate(xd_parts, axis=1).astype(BF16)
            decay = jnp.concatenate(decay_parts, axis=1)
            state_ref[g] = st * decay + _dot(bg.T.astype(BF16), xd)

        z = z_ref[pl.ds(r0, q), :].astype(F32)
        outs = []
        for g in range(SSD_GROUPS):
            yg = jnp.concatenate(y_parts[g * pairs_per_group:(g + 1) * pairs_per_group], axis=1)
            yg = yg * _silu(z[:, g * gw:(g + 1) * gw])
            outs.append(yg * _rms_scale(yg) * g_ref[:, g * gw:(g + 1) * gw])
        y_ref[pl.ds(r0, q), :] = jnp.concatenate(outs, axis=1).astype(y_ref.dtype)

    for c in range(tb // q):
        chunk(c)


def _ssd_parts(main3, dt3, conv_w, conv_b, dt_bias, a_log, d_skip, norm_g, *, tb):
    b, seq, _ = main3.shape
    z_blk = (3 * ATTN_WIDTH) // SSD_WIDTH
    xs_blk = (3 * ATTN_WIDTH + SSD_WIDTH) // SSD_WIDTH
    bc_blk = (3 * ATTN_WIDTH + 2 * SSD_WIDTH) // BC_WIDTH

    def const(shape):
        return pl.BlockSpec(shape, lambda i, j: (0,) * len(shape))

    return dict(
        operands=[main3, main3, main3, dt3, conv_w, conv_b, dt_bias, a_log, d_skip, norm_g],
        in_specs=[
            pl.BlockSpec((None, tb, SSD_WIDTH), lambda i, j: (i, j, z_blk)),
            pl.BlockSpec((None, tb, SSD_WIDTH), lambda i, j: (i, j, xs_blk)),
            pl.BlockSpec((None, tb, BC_WIDTH), lambda i, j: (i, j, bc_blk)),
            pl.BlockSpec((None, tb, LANES), lambda i, j: (i, j, 0)),
            const((SSD_CONV, CONV_CH)), const((1, CONV_CH)),
            const((1, LANES)), const((1, LANES)),
            const((1, SSD_WIDTH)), const((1, SSD_WIDTH)),
        ],
        out_specs=[pl.BlockSpec((None, tb, SSD_WIDTH), lambda i, j: (i, j, 0))],
        out_shape=[jax.ShapeDtypeStruct((b, seq, SSD_WIDTH), BF16)],
        scratch_shapes=[
            pltpu.VMEM((2, SSD_CHUNK, CONV_CH), BF16),
            pltpu.VMEM((SSD_GROUPS, SSD_STATE, SSD_WIDTH // SSD_GROUPS), F32),
        ],
        grid=(b, seq // tb),
    )


def _mixers_kernel(*refs, counts):
    (a_in, s_in), (a_out, s_out), (a_scr, s_scr) = counts
    it = iter(refs)
    take = lambda n: [next(it) for _ in range(n)]
    a_ins, s_ins, a_outs, s_outs, a_scrs, s_scrs = (take(n) for n in (a_in, s_in, a_out, s_out, a_scr, s_scr))
    _ssd_kernel(*s_ins, *s_outs, *s_scrs)
    _attn_kernel(*a_ins, *a_outs, *a_scrs)


def _mixers(attn_parts, ssd_parts):
    assert attn_parts["grid"] == ssd_parts["grid"]
    parts = (attn_parts, ssd_parts)
    counts = tuple(tuple(len(p[key]) for p in parts) for key in ("in_specs", "out_specs", "scratch_shapes"))
    cat = lambda key: [x for p in parts for x in p[key]]
    return pl.pallas_call(
        functools.partial(_mixers_kernel, counts=counts),
        grid=attn_parts["grid"],
        in_specs=cat("in_specs"),
        out_specs=cat("out_specs"),
        out_shape=cat("out_shape"),
        scratch_shapes=cat("scratch_shapes"),
        compiler_params=_params(("parallel", "arbitrary")),
        name="mixers",
    )(*cat("operands"))


def _out_proj_kernel(a_ref, y_ref, x_ref, g_ref, w_ref, o_ref):
    a = a_ref[...].astype(F32)
    an = (a * _rms_scale(a) * g_ref[...]).astype(BF16)
    mixed = _dot(an, w_ref[0:ATTN_WIDTH, :]) + _dot(y_ref[...], w_ref[ATTN_WIDTH:, :])
    o_ref[...] = x_ref[...] + mixed


def _out_proj(attn, y, x, g, w_all, layer, *, tm):
    m, d = x.shape
    return pl.pallas_call(
        _out_proj_kernel,
        grid=(m // tm,),
        in_specs=[
            pl.BlockSpec((tm, ATTN_WIDTH), lambda i: (i, 0)),
            pl.BlockSpec((tm, SSD_WIDTH), lambda i: (i, 0)),
            pl.BlockSpec((tm, d), lambda i: (i, 0)),
            _resident((1, ATTN_WIDTH)),
            _resident(w_all.shape[1:], layer),
        ],
        out_specs=pl.BlockSpec((tm, d), lambda i: (i, 0)),
        out_shape=jax.ShapeDtypeStruct((m, d), F32),
        compiler_params=_params(("parallel",)),
        name="out_proj",
    )(attn, y, x, g, w_all)


def _mlp_kernel(x_ref, g_ref, w1_ref, w2_ref, gf_ref, o_ref, hn_ref, *, final_norm):
    f = pl.program_id(1)

    @pl.when(f == 0)
    def _():
        x = x_ref[...]
        hn_ref[...] = (x * _rms_scale(x) * g_ref[...]).astype(BF16)
        o_ref[...] = x

    h = jnp.maximum(_dot(hn_ref[...], w1_ref[...]), 0.0)
    o_ref[...] += _dot((h * h).astype(BF16), w2_ref[...])

    if final_norm:
        @pl.when(f == pl.num_programs(1) - 1)
        def _():
            o = o_ref[...]
            o_ref[...] = o * _rms_scale(o) * gf_ref[...]


def _mlp(x, g, w1, w2, gf, *, tm, tf, final_norm):
    m, d = x.shape
    dff = w1.shape[1]
    return pl.pallas_call(
        functools.partial(_mlp_kernel, final_norm=final_norm),
        grid=(m // tm, dff // tf),
        in_specs=[
            pl.BlockSpec((tm, d), lambda i, j: (i, 0)),
            pl.BlockSpec((1, d), lambda i, j: (0, 0)),
            pl.BlockSpec((d, tf), lambda i, j: (0, j)),
            pl.BlockSpec((tf, d), lambda i, j: (j, 0)),
            pl.BlockSpec((1, d), lambda i, j: (0, 0)),
        ],
        out_specs=pl.BlockSpec((tm, d), lambda i, j: (i, 0)),
        out_shape=jax.ShapeDtypeStruct((m, d), F32),
        scratch_shapes=[pltpu.VMEM((tm, d), BF16)],
        compiler_params=_params(("parallel", "arbitrary")),
        name="mlp",
    )(x, g, w1, w2, gf)


def _alibi_slopes():
    return jnp.asarray(2.0 ** (-8.0 * (np.arange(ATTN_HEADS) + 1) / ATTN_HEADS), dtype=F32)


def _pad_lanes(v):
    return jnp.pad(v, ((0, 0), (0, LANES - v.shape[1])))


def kernel(x, ln1_g, w_in, conv_w, conv_b, dt_bias, a_log, d_skip, attn_norm_g, ssd_norm_g, w_out,
           ln2_g, w_mlp_in, w_mlp_out, final_norm_g):
    b, seq, d = x.shape
    m = b * seq
    slopes = _alibi_slopes()
    xf = x.reshape(m, d)
    w_in_bf = w_in.astype(BF16)
    w_dt_bf = jnp.pad(w_in[:, :, MAIN_WIDTH:], ((0, 0), (0, 0), (0, LANES - (w_in.shape[2] - MAIN_WIDTH)))).astype(BF16)
    w_out_bf = w_out.astype(BF16)
    for l in range(DEPTH):
        main, dt_raw = _in_proj(xf, ln1_g[l][None, :], w_in_bf, w_dt_bf, l,
                                tm=IN_PROJ_ROWS, tn=IN_PROJ_COLS)
        main3 = main.reshape(b, seq, MAIN_WIDTH)
        attn_parts = _attention_parts(slopes, main3, w_mlp_in, w_mlp_out, l)
        ssd_parts = _ssd_parts(
            main3, dt_raw.reshape(b, seq, LANES), conv_w[l], conv_b[l][None, :],
            _pad_lanes(dt_bias[l][None, :]), _pad_lanes(a_log[l][None, :]),
            jnp.repeat(d_skip[l], SSD_HEAD_DIM)[None, :], ssd_norm_g[l][None, :],
            tb=seq // attn_parts["grid"][1])
        attn, w1_bf, w2_bf, y = _mixers(attn_parts, ssd_parts)
        x1 = _out_proj(attn.reshape(m, ATTN_WIDTH), y.reshape(m, SSD_WIDTH), xf,
                       attn_norm_g[l][None, :], w_out_bf, l, tm=OUT_PROJ_ROWS)
        xf = _mlp(x1, ln2_g[l][None, :], w1_bf, w2_bf,
                  final_norm_g[None, :], tm=MLP_ROWS, tf=MLP_HIDDEN, final_norm=(l == DEPTH - 1))
    return xf.reshape(b, seq, d)
```

```python
import functools

import jax
import jax.numpy as jnp
import numpy as np
from jax import lax
from jax.experimental import pallas as pl
from jax.experimental.pallas import tpu as pltpu

D_MODEL = 2048
DEPTH = 2
HEAD_DIM = 64
ATTN_WIDTH = D_MODEL // 2
ATTN_HEADS = ATTN_WIDTH // HEAD_DIM
DILATED_BRANCHES = ((128, 1), (512, 4), (2048, 16))
ATTN_BLOCK = 128
SSD_WIDTH = D_MODEL // 2
SSD_HEAD_DIM = 64
SSD_HEADS = SSD_WIDTH // SSD_HEAD_DIM
SSD_GROUPS = 2
SSD_STATE = 128
SSD_CONV = 4
SSD_CHUNK = 128
BC_WIDTH = 2 * SSD_GROUPS * SSD_STATE
CONV_CH = SSD_WIDTH + BC_WIDTH
MAIN_WIDTH = 3 * ATTN_WIDTH + SSD_WIDTH + CONV_CH
D_FF = 4 * D_MODEL
NORM_EPS = 1e-5

LANES = 128
HEAD_PAIR = LANES // HEAD_DIM
NEG_BIG = -1e30
LOG2_E = 1.4426950408889634
Q_PRESCALE = HEAD_DIM ** -0.5 * LOG2_E
VMEM_LIMIT = 56 * 1024 * 1024

IN_PROJ_ROWS, IN_PROJ_COLS = 512, 512
OUT_PROJ_ROWS = 512
MLP_ROWS, MLP_HIDDEN = 1024, 512
ATTN_PIECES_PER_TURN = 2

F32 = jnp.float32
BF16 = jnp.bfloat16


def _dot(a, b):
    return jnp.dot(a, b, preferred_element_type=F32)


def _dot_nt(a, b):
    return lax.dot_general(a, b, (((1,), (1,)), ((), ())), preferred_element_type=F32)


def _rms_scale(x):
    return lax.rsqrt(jnp.mean(x * x, axis=-1, keepdims=True) + NORM_EPS)


def _silu(x):
    return x * (1.0 / (1.0 + jnp.exp2(x * (-LOG2_E))))


def _params(semantics):
    return pltpu.CompilerParams(dimension_semantics=semantics, vmem_limit_bytes=VMEM_LIMIT)


def _in_proj_kernel(x_ref, g_ref, w_ref, wdt_ref, main_ref, dt_ref, hn_ref, *, tn):
    x = x_ref[...]
    hn_ref[...] = (x * _rms_scale(x) * g_ref[...]).astype(BF16)
    dt_ref[...] = _dot(hn_ref[...], wdt_ref[...])
    for j in range(main_ref.shape[1] // tn):
        cols = slice(j * tn, (j + 1) * tn)
        acc = _dot(hn_ref[...], w_ref[:, cols])
        if (j + 1) * tn <= ATTN_WIDTH:
            acc = acc * Q_PRESCALE
        main_ref[:, cols] = acc.astype(main_ref.dtype)


def _resident(shape, layer=None):
    if layer is None:
        return pl.BlockSpec(shape, lambda i: (0,) * len(shape), pipeline_mode=pl.Buffered(1))
    return pl.BlockSpec((None,) + tuple(shape), lambda i: (layer,) + (0,) * len(shape),
                        pipeline_mode=pl.Buffered(1))


def _in_proj(x, g, w_all, w_dt, layer, *, tm, tn):
    m, d = x.shape
    n = MAIN_WIDTH
    assert m % tm == 0 and n % tn == 0
    return pl.pallas_call(
        functools.partial(_in_proj_kernel, tn=tn),
        grid=(m // tm,),
        in_specs=[
            pl.BlockSpec((tm, d), lambda i: (i, 0)),
            _resident((1, d)),
            _resident(w_all.shape[1:], layer),
            _resident((d, LANES), layer),
        ],
        out_specs=[
            pl.BlockSpec((tm, n), lambda i: (i, 0)),
            pl.BlockSpec((tm, LANES), lambda i: (i, 0)),
        ],
        out_shape=[
            jax.ShapeDtypeStruct((m, n), BF16),
            jax.ShapeDtypeStruct((m, LANES), F32),
        ],
        scratch_shapes=[pltpu.VMEM((tm, d), BF16)],
        compiler_params=_params(("parallel",)),
        name="in_proj",
    )(x, g, w_all, w_dt)


def _attn_steps(slopes_ref, q_ref, k_ref, v_ref, w1_ref, w2_ref, o_ref, w1o_ref, w2o_ref,
                 qf_ref, kf_ref, vf_ref, qm_ref, km_ref, vm_ref,
                 qd_ref, kd_ref, vd_ref, qh_ref, kh_ref, vh_ref, st0_ref, bias_ref):
    seq = q_ref.shape[0]
    blk = ATTN_BLOCK
    hp = pl.program_id(1)

    w1o_ref[...] = w1_ref[...].astype(BF16)
    w2o_ref[...] = w2_ref[...].astype(BF16)

    qf_ref[...] = q_ref[...].astype(F32)
    kf_ref[...] = k_ref[...].astype(F32)
    vf_ref[...] = v_ref[...].astype(F32)

    low = lax.broadcasted_iota(jnp.int32, (blk, LANES), 1) < HEAD_DIM
    ones_bf = jnp.ones((2 * blk, LANES), BF16)

    row = lax.broadcasted_iota(jnp.int32, (2 * blk, 2 * blk), 0)
    col = lax.broadcasted_iota(jnp.int32, (2 * blk, 2 * blk), 1)
    delta = (row & (blk - 1)) - col + blk
    slope = jnp.where(row < blk, slopes_ref[HEAD_PAIR * hp], slopes_ref[HEAD_PAIR * hp + 1])
    for t, (window, dil) in enumerate(DILATED_BRANCHES):
        valid = (delta >= 0) & (delta <= window // dil)
        bias = jnp.where(valid, -slope * (delta * dil).astype(F32) * LOG2_E, NEG_BIG)
        bias_ref[2 * t] = bias
        bias_ref[2 * t + 1] = jnp.where(col >= blk, bias, NEG_BIG)

    def pair_rows(x):
        return jnp.where(low, jnp.broadcast_to(x[:blk], (blk, LANES)),
                         jnp.broadcast_to(x[blk:], (blk, LANES)))

    def pair_halves(x):
        return jnp.where(low, x[:blk], x[blk:])

    zero_bf = jnp.zeros((), BF16)
    n_blocks = seq // blk

    def block_rows(i):
        start = i * blk
        return pl.ds(start if isinstance(start, int) else pl.multiple_of(start, blk), blk)

    (_, d_lo), (_, d_mid), (_, d_hi) = DILATED_BRANCHES
    ratio = d_hi // d_mid
    assert d_lo == 1 and d_hi == d_mid * ratio and seq % (blk * d_hi) == 0
    mid_len = seq // d_mid

    def log2(n):
        assert n & (n - 1) == 0
        return n.bit_length() - 1

    def split_block(idx, dil):
        nb = n_blocks // dil
        return idx >> log2(nb), idx & (nb - 1)

    def token_rows(idx, dil):
        r, n = split_block(idx, dil)
        return pl.ds(r + n * (blk * dil), blk, stride=dil) if dil > 1 else block_rows(idx)

    def hi_rows_in_mid(idx):
        r, n = split_block(idx, d_hi)
        return pl.ds((r & (d_mid - 1)) * mid_len + n * (blk * ratio) + (r >> log2(d_mid)), blk, stride=ratio)

    def gather_mid(idx):
        rows = token_rows(idx, d_mid)
        for f_ref, mid_ref, d_ref in ((qf_ref, qm_ref, qd_ref), (kf_ref, km_ref, kd_ref), (vf_ref, vm_ref, vd_ref)):
            x = f_ref[rows, :]
            mid_ref[block_rows(idx), :] = x
            d_ref[block_rows(idx), :] = x.astype(BF16)

    def gather_hi(idx):
        rows = hi_rows_in_mid(idx)
        for mid_ref, h_ref in ((qm_ref, qh_ref), (km_ref, kh_ref), (vm_ref, vh_ref)):
            h_ref[block_rows(idx), :] = mid_ref[rows, :].astype(BF16)

    def block_softmax(t, idx, q_src, k_src, v_src):
        first = split_block(idx, DILATED_BRANCHES[t][1])[1] == 0
        cur = block_rows(idx)
        prev = block_rows(idx if first else idx - 1)
        q = q_src[cur, :]
        q2 = jnp.concatenate([jnp.where(low, q, zero_bf), jnp.where(low, zero_bf, q)], axis=0)
        kk = jnp.concatenate([k_src[prev, :], k_src[cur, :]], axis=0)
        vv = jnp.concatenate([jnp.concatenate([v_src[prev, :], v_src[cur, :]], axis=0), ones_bf], axis=1)
        s = _dot_nt(q2, kk) + bias_ref[2 * t + int(first)]
        m = jnp.max(s, axis=-1, keepdims=True)
        p = jnp.exp2(s - m)
        pv = _dot(p.astype(BF16), vv)
        return pair_rows(m), pair_halves(pv[:, LANES:]), pair_halves(pv[:, :LANES])

    def combine(a, b):
        (ma, la, acca), (mb, lb, accb) = a, b
        m = jnp.maximum(ma, mb)
        wa, wb = jnp.exp2(ma - m), jnp.exp2(mb - m)
        return m, wa * la + wb * lb, wa * acca + wb * accb

    def load_state(st, rows):
        return tuple(ref[rows, :] for ref in st)

    def store_state(st, rows, state):
        for ref, x in zip(st, state):
            ref[rows, :] = x

    st0 = tuple(st0_ref.at[i] for i in range(3))
    st1 = (qf_ref, kf_ref, vf_ref)
    of_ref = qm_ref

    def branch0(idx):
        store_state(st0, block_rows(idx), block_softmax(0, idx, q_ref, k_ref, v_ref))
        gather_mid(idx)

    def branch1(idx):
        mine = block_softmax(1, idx, qd_ref, kd_ref, vd_ref)
        store_state(st1, block_rows(idx), combine(load_state(st0, token_rows(idx, d_mid)), mine))
        gather_hi(idx)

    def branch2(idx):
        mine = block_softmax(2, idx, qh_ref, kh_ref, vh_ref)
        _, l_all, acc_all = combine(load_state(st1, hi_rows_in_mid(idx)), mine)
        of_ref[token_rows(idx, d_hi), :] = acc_all / l_all

    yield
    for branch in (branch0, branch1, branch2):
        for idx in range(n_blocks):
            branch(idx)
            yield
    o_ref[...] = of_ref[...].astype(o_ref.dtype)


def _attention_parts(slopes, main3, w1_all, w2_all, layer):
    b, seq, _ = main3.shape
    n_pairs = ATTN_WIDTH // LANES
    blk = ATTN_BLOCK
    n_steps = b * n_pairs
    w1_shape, w2_shape = w1_all.shape[1:], w2_all.shape[1:]
    r1, r2 = w1_shape[0] // n_steps, w2_shape[0] // n_steps
    assert r1 * n_steps == w1_shape[0] and r2 * n_steps == w2_shape[0]

    def spec(offset):
        return pl.BlockSpec((None, seq, LANES), lambda i, j: (i, 0, offset + j))

    return dict(
        operands=[slopes, main3, main3, main3, w1_all, w2_all],
        in_specs=[
            pl.BlockSpec(memory_space=pltpu.SMEM),
            spec(0), spec(n_pairs), spec(2 * n_pairs),
            pl.BlockSpec((None, r1, w1_shape[1]), lambda i, j: (layer, i * n_pairs + j, 0)),
            pl.BlockSpec((None, r2, w2_shape[1]), lambda i, j: (layer, i * n_pairs + j, 0)),
        ],
        out_specs=[
            pl.BlockSpec((None, seq, LANES), lambda i, j: (i, 0, j)),
            pl.BlockSpec((r1, w1_shape[1]), lambda i, j: (i * n_pairs + j, 0)),
            pl.BlockSpec((r2, w2_shape[1]), lambda i, j: (i * n_pairs + j, 0)),
        ],
        out_shape=[
            jax.ShapeDtypeStruct((b, seq, ATTN_WIDTH), BF16),
            jax.ShapeDtypeStruct(w1_shape, BF16),
            jax.ShapeDtypeStruct(w2_shape, BF16),
        ],
        scratch_shapes=[pltpu.VMEM((seq, LANES), F32)] * 6
        + [pltpu.VMEM((seq, LANES), BF16)] * 6
        + [pltpu.VMEM((3, seq, LANES), F32)]
        + [pltpu.VMEM((2 * len(DILATED_BRANCHES), 2 * blk, 2 * blk), F32)],
        grid=(b, n_pairs),
    )


def _ssd_steps(z_ref, xs_ref, bc_ref, dt_ref, cw_ref, cb_ref, dtb_ref, alog_ref, dskip_ref, g_ref,
               y_ref, carry_ref, state_ref):
    tb = xs_ref.shape[0]
    q = SSD_CHUNK
    gw = SSD_WIDTH // SSD_GROUPS
    pairs_per_group = gw // LANES
    conv_cols = 512

    step = pl.program_id(1)
    slot = step & 1

    @pl.when(step == 0)
    def _():
        carry_ref[0] = jnp.zeros((q, CONV_CH), BF16)
        state_ref[...] = jnp.zeros(state_ref.shape, F32)

    carry_ref[1 - slot, :, 0:SSD_WIDTH] = xs_ref[tb - q:tb, :]
    carry_ref[1 - slot, :, SSD_WIDTH:CONV_CH] = bc_ref[tb - q:tb, :]

    low = lax.broadcasted_iota(jnp.int32, (q, LANES), 1) < SSD_HEAD_DIM
    ri = lax.broadcasted_iota(jnp.int32, (q, q), 0)
    ci = lax.broadcasted_iota(jnp.int32, (q, q), 1)
    tri = ri >= ci
    tri_bf = jnp.where(tri, 1.0, 0.0).astype(BF16)
    a_neg = -jnp.exp(alog_ref[...])

    sr = lax.broadcasted_iota(jnp.int32, (SSD_CONV * q, 2 * q), 0)
    sc = lax.broadcasted_iota(jnp.int32, (SSD_CONV * q, 2 * q), 1)
    src_row = (sr & (q - 1)) + (sr >> (q.bit_length() - 1)) + (q - (SSD_CONV - 1))
    shift = jnp.where(sc == src_row, 1.0, 0.0).astype(BF16)

    def cumsum_rows(v):
        hi = v.astype(BF16)
        r1 = v - hi.astype(F32)
        mid = r1.astype(BF16)
        lo = (r1 - mid.astype(F32)).astype(BF16)
        return _dot(tri_bf, hi) + _dot(tri_bf, mid) + _dot(tri_bf, lo)

    def lane_bcast(v, h):
        return jnp.broadcast_to(v[:, h:h + 1], (q, LANES))

    def chunk(c):
        r0 = c * q
        u_parts = []
        for j in range(CONV_CH // conv_cols):
            cols = slice(j * conv_cols, (j + 1) * conv_cols)
            src_ref, off = (xs_ref, 0) if j * conv_cols < SSD_WIDTH else (bc_ref, SSD_WIDTH)
            src_cols = slice(j * conv_cols - off, (j + 1) * conv_cols - off)
            prev = carry_ref[slot, :, cols] if c == 0 else src_ref[r0 - q:r0, src_cols]
            window = jnp.concatenate([prev, src_ref[r0:r0 + q, src_cols]], axis=0)
            taps = _dot(shift, window)
            uj = cb_ref[:, cols]
            for k in range(SSD_CONV):
                uj = uj + cw_ref[k:k + 1, cols] * taps[k * q:(k + 1) * q]
            u_parts.append(_silu(uj))
        yield
        u = jnp.concatenate(u_parts, axis=1)
        x = u[:, :SSD_WIDTH]
        dt_in = dt_ref[pl.ds(r0, q), :] + dtb_ref[...]
        dt = jnp.maximum(dt_in, 0.0) + jnp.log(1.0 + jnp.exp(-jnp.abs(dt_in)))
        acum = cumsum_rows(dt * a_neg) * LOG2_E
        acum_t = acum.T

        y_parts = []
        for g in range(SSD_GROUPS):
            bg = u[:, SSD_WIDTH + g * SSD_STATE:SSD_WIDTH + (g + 1) * SSD_STATE]
            cg = u[:, SSD_WIDTH + (SSD_GROUPS + g) * SSD_STATE:SSD_WIDTH + (SSD_GROUPS + g + 1) * SSD_STATE]
            cg_bf = cg.astype(BF16)
            cb = _dot_nt(cg_bf, bg.astype(BF16))
            st = state_ref[g]
            y_off = _dot(cg_bf, st.astype(BF16))
            xd_parts, decay_parts = [], []
            for pp in range(pairs_per_group):
                p = g * pairs_per_group + pp
                h0 = HEAD_PAIR * p
                xp = x[:, p * LANES:(p + 1) * LANES]
                cols = [lane_bcast(acum, h0 + e) for e in range(HEAD_PAIR)]
                colc = jnp.where(low, cols[0], cols[1])
                ms = []
                for e in range(HEAD_PAIR):
                    h = h0 + e
                    seg = jnp.exp2(jnp.where(tri, cols[e] - acum_t[h:h + 1, :], NEG_BIG))
                    ms.append(cb * seg)
                m_cat = jnp.concatenate(ms, axis=1).astype(BF16)
                dtc = jnp.where(low, lane_bcast(dt, h0), lane_bcast(dt, h0 + 1))
                xdt = xp * dtc
                x_blk = jnp.concatenate([jnp.where(low, xdt, 0.0), jnp.where(low, 0.0, xdt)],
                                        axis=0).astype(BF16)
                y_diag = _dot(m_cat, x_blk)
                y_o = y_off[:, pp * LANES:(pp + 1) * LANES] * jnp.exp2(colc)
                y_parts.append(y_diag + y_o + dskip_ref[:, p * LANES:(p + 1) * LANES] * xp)
                last = colc[q - 1:q, :]
                xd_parts.append(xdt * jnp.exp2(last - colc))
                yield
                decay_parts.append(jnp.exp2(last))
            xd = jnp.concatenate(xd_parts, axis=1).astype(BF16)
            decay = jnp.concatenate(decay_parts, axis=1)
            state_ref[g] = st * decay + _dot(bg.T.astype(BF16), xd)

        z = z_ref[pl.ds(r0, q), :].astype(F32)
        outs = []
        for g in range(SSD_GROUPS):
            yg = jnp.concatenate(y_parts[g * pairs_per_group:(g + 1) * pairs_per_group], axis=1)
            yg = yg * _silu(z[:, g * gw:(g + 1) * gw])
            outs.append(yg * _rms_scale(yg) * g_ref[:, g * gw:(g + 1) * gw])
        y_ref[pl.ds(r0, q), :] = jnp.concatenate(outs, axis=1).astype(y_ref.dtype)

    for c in range(tb // q):
        yield from chunk(c)
        yield


def _ssd_parts(main3, dt3, conv_w, conv_b, dt_bias, a_log, d_skip, norm_g, *, tb):
    b, seq, _ = main3.shape
    z_blk = (3 * ATTN_WIDTH) // SSD_WIDTH
    xs_blk = (3 * ATTN_WIDTH + SSD_WIDTH) // SSD_WIDTH
    bc_blk = (3 * ATTN_WIDTH + 2 * SSD_WIDTH) // BC_WIDTH

    def const(shape):
        return pl.BlockSpec(shape, lambda i, j: (0,) * len(shape))

    return dict(
        operands=[main3, main3, main3, dt3, conv_w, conv_b, dt_bias, a_log, d_skip, norm_g],
        in_specs=[
            pl.BlockSpec((None, tb, SSD_WIDTH), lambda i, j: (i, j, z_blk)),
            pl.BlockSpec((None, tb, SSD_WIDTH), lambda i, j: (i, j, xs_blk)),
            pl.BlockSpec((None, tb, BC_WIDTH), lambda i, j: (i, j, bc_blk)),
            pl.BlockSpec((None, tb, LANES), lambda i, j: (i, j, 0)),
            const((SSD_CONV, CONV_CH)), const((1, CONV_CH)),
            const((1, LANES)), const((1, LANES)),
            const((1, SSD_WIDTH)), const((1, SSD_WIDTH)),
        ],
        out_specs=[pl.BlockSpec((None, tb, SSD_WIDTH), lambda i, j: (i, j, 0))],
        out_shape=[jax.ShapeDtypeStruct((b, seq, SSD_WIDTH), BF16)],
        scratch_shapes=[
            pltpu.VMEM((2, SSD_CHUNK, CONV_CH), BF16),
            pltpu.VMEM((SSD_GROUPS, SSD_STATE, SSD_WIDTH // SSD_GROUPS), F32),
        ],
        grid=(b, seq // tb),
    )


def _mixers_kernel(*refs, counts):
    (a_in, s_in), (a_out, s_out), (a_scr, s_scr) = counts
    it = iter(refs)
    take = lambda n: [next(it) for _ in range(n)]
    a_ins, s_ins, a_outs, s_outs, a_scrs, s_scrs = (take(n) for n in (a_in, s_in, a_out, s_out, a_scr, s_scr))
    attn = _attn_steps(*a_ins, *a_outs, *a_scrs)
    ssd = _ssd_steps(*s_ins, *s_outs, *s_scrs)
    live = [attn, ssd]
    while live:
        for gen, pieces in ((attn, ATTN_PIECES_PER_TURN), (ssd, 1)):
            for _ in range(pieces):
                if gen in live and next(gen, StopIteration) is StopIteration:
                    live.remove(gen)


def _mixers(attn_parts, ssd_parts):
    assert attn_parts["grid"] == ssd_parts["grid"]
    parts = (attn_parts, ssd_parts)
    counts = tuple(tuple(len(p[key]) for p in parts) for key in ("in_specs", "out_specs", "scratch_shapes"))
    cat = lambda key: [x for p in parts for x in p[key]]
    return pl.pallas_call(
        functools.partial(_mixers_kernel, counts=counts),
        grid=attn_parts["grid"],
        in_specs=cat("in_specs"),
        out_specs=cat("out_specs"),
        out_shape=cat("out_shape"),
        scratch_shapes=cat("scratch_shapes"),
        compiler_params=_params(("parallel", "arbitrary")),
        name="mixers",
    )(*cat("operands"))


def _out_proj_kernel(a_ref, y_ref, x_ref, g_ref, w_ref, o_ref):
    a = a_ref[...].astype(F32)
    an = (a * _rms_scale(a) * g_ref[...]).astype(BF16)
    mixed = _dot(an, w_ref[0:ATTN_WIDTH, :]) + _dot(y_ref[...], w_ref[ATTN_WIDTH:, :])
    o_ref[...] = x_ref[...] + mixed


def _out_proj(attn, y, x, g, w_all, layer, *, tm):
    m, d = x.shape
    return pl.pallas_call(
        _out_proj_kernel,
        grid=(m // tm,),
        in_specs=[
            pl.BlockSpec((tm, ATTN_WIDTH), lambda i: (i, 0)),
            pl.BlockSpec((tm, SSD_WIDTH), lambda i: (i, 0)),
            pl.BlockSpec((tm, d), lambda i: (i, 0)),
            _resident((1, ATTN_WIDTH)),
            _resident(w_all.shape[1:], layer),
        ],
        out_specs=pl.BlockSpec((tm, d), lambda i: (i, 0)),
        out_shape=jax.ShapeDtypeStruct((m, d), F32),
        compiler_params=_params(("parallel",)),
        name="out_proj",
    )(attn, y, x, g, w_all)


def _mlp_kernel(x_ref, g_ref, w1_ref, w2_ref, gf_ref, o_ref, hn_ref, *, final_norm):
    f = pl.program_id(1)

    @pl.when(f == 0)
    def _():
        x = x_ref[...]
        hn_ref[...] = (x * _rms_scale(x) * g_ref[...]).astype(BF16)
        o_ref[...] = x

    h = jnp.maximum(_dot(hn_ref[...], w1_ref[...]), 0.0)
    o_ref[...] += _dot((h * h).astype(BF16), w2_ref[...])

    if final_norm:
        @pl.when(f == pl.num_programs(1) - 1)
        def _():
            o = o_ref[...]
            o_ref[...] = o * _rms_scale(o) * gf_ref[...]


def _mlp(x, g, w1, w2, gf, *, tm, tf, final_norm):
    m, d = x.shape
    dff = w1.shape[1]
    return pl.pallas_call(
        functools.partial(_mlp_kernel, final_norm=final_norm),
        grid=(m // tm, dff // tf),
        in_specs=[
            pl.BlockSpec((tm, d), lambda i, j: (i, 0)),
            pl.BlockSpec((1, d), lambda i, j: (0, 0)),
            pl.BlockSpec((d, tf), lambda i, j: (0, j)),
            pl.BlockSpec((tf, d), lambda i, j: (j, 0)),
            pl.BlockSpec((1, d), lambda i, j: (0, 0)),
        ],
        out_specs=pl.BlockSpec((tm, d), lambda i, j: (i, 0)),
        out_shape=jax.ShapeDtypeStruct((m, d), F32),
        scratch_shapes=[pltpu.VMEM((tm, d), BF16)],
        compiler_params=_params(("parallel", "arbitrary")),
        name="mlp",
    )(x, g, w1, w2, gf)


def _alibi_slopes():
    return jnp.asarray(2.0 ** (-8.0 * (np.arange(ATTN_HEADS) + 1) / ATTN_HEADS), dtype=F32)


def _pad_lanes(v):
    return jnp.pad(v, ((0, 0), (0, LANES - v.shape[1])))


def kernel(x, ln1_g, w_in, conv_w, conv_b, dt_bias, a_log, d_skip, attn_norm_g, ssd_norm_g, w_out,
           ln2_g, w_mlp_in, w_mlp_out, final_norm_g):
    b, seq, d = x.shape
    m = b * seq
    slopes = _alibi_slopes()
    xf = x.reshape(m, d)
    w_in_bf = w_in.astype(BF16)
    w_dt_bf = jnp.pad(w_in[:, :, MAIN_WIDTH:], ((0, 0), (0, 0), (0, LANES - (w_in.shape[2] - MAIN_WIDTH)))).astype(BF16)
    w_out_bf = w_out.astype(BF16)
    for l in range(DEPTH):
        main, dt_raw = _in_proj(xf, ln1_g[l][None, :], w_in_bf, w_dt_bf, l,
                                tm=IN_PROJ_ROWS, tn=IN_PROJ_COLS)
        main3 = main.reshape(b, seq, MAIN_WIDTH)
        attn_parts = _attention_parts(slopes, main3, w_mlp_in, w_mlp_out, l)
        ssd_parts = _ssd_parts(
            main3, dt_raw.reshape(b, seq, LANES), conv_w[l], conv_b[l][None, :],
            _pad_lanes(dt_bias[l][None, :]), _pad_lanes(a_log[l][None, :]),
            jnp.repeat(d_skip[l], SSD_HEAD_DIM)[None, :], ssd_norm_g[l][None, :],
            tb=seq // attn_parts["grid"][1])
        attn, w1_bf, w2_bf, y = _mixers(attn_parts, ssd_parts)
        x1 = _out_proj(attn.reshape(m, ATTN_WIDTH), y.reshape(m, SSD_WIDTH), xf,
                       attn_norm_g[l][None, :], w_out_bf, l, tm=OUT_PROJ_ROWS)
        xf = _mlp(x1, ln2_g[l][None, :], w1_bf, w2_bf,
                  final_norm_g[None, :], tm=MLP_ROWS, tf=MLP_HIDDEN, final_norm=(l == DEPTH - 1))
    return xf.reshape(b, seq, d)
```

```python
import functools

import jax
import jax.numpy as jnp
import numpy as np
from jax import lax
from jax.experimental import pallas as pl
from jax.experimental.pallas import tpu as pltpu

D_MODEL = 2048
DEPTH = 2
HEAD_DIM = 64
ATTN_WIDTH = D_MODEL // 2
ATTN_HEADS = ATTN_WIDTH // HEAD_DIM
DILATED_BRANCHES = ((128, 1), (512, 4), (2048, 16))
ATTN_BLOCK = 128
SSD_WIDTH = D_MODEL // 2
SSD_HEAD_DIM = 64
SSD_HEADS = SSD_WIDTH // SSD_HEAD_DIM
SSD_GROUPS = 2
SSD_STATE = 128
SSD_CONV = 4
SSD_CHUNK = 128
BC_WIDTH = 2 * SSD_GROUPS * SSD_STATE
CONV_CH = SSD_WIDTH + BC_WIDTH
MAIN_WIDTH = 3 * ATTN_WIDTH + SSD_WIDTH + CONV_CH
D_FF = 4 * D_MODEL
NORM_EPS = 1e-5

LANES = 128
HEAD_PAIR = LANES // HEAD_DIM
NEG_BIG = -1e30
LOG2_E = 1.4426950408889634
Q_PRESCALE = HEAD_DIM ** -0.5 * LOG2_E
VMEM_LIMIT = 56 * 1024 * 1024

IN_PROJ_ROWS, IN_PROJ_COLS = 512, 512
OUT_PROJ_ROWS = 512
MLP_ROWS, MLP_HIDDEN = 1024, 512
SSD_TOKENS = 1024

F32 = jnp.float32
BF16 = jnp.bfloat16


def _dot(a, b):
    return jnp.dot(a, b, preferred_element_type=F32)


def _dot_nt(a, b):
    return lax.dot_general(a, b, (((1,), (1,)), ((), ())), preferred_element_type=F32)


def _rms_scale(x):
    return lax.rsqrt(jnp.mean(x * x, axis=-1, keepdims=True) + NORM_EPS)


def _silu(x):
    return x * (1.0 / (1.0 + jnp.exp2(x * (-LOG2_E))))


def _params(semantics):
    return pltpu.CompilerParams(dimension_semantics=semantics, vmem_limit_bytes=VMEM_LIMIT)


def _in_proj_kernel(x_ref, g_ref, w_ref, wdt_ref, main_ref, dt_ref, hn_ref, *, tn):
    x = x_ref[...]
    hn_ref[...] = (x * _rms_scale(x) * g_ref[...]).astype(BF16)
    dt_ref[...] = _dot(hn_ref[...], wdt_ref[...])
    for j in range(main_ref.shape[1] // tn):
        cols = slice(j * tn, (j + 1) * tn)
        acc = _dot(hn_ref[...], w_ref[:, cols])
        if (j + 1) * tn <= ATTN_WIDTH:
            acc = acc * Q_PRESCALE
        main_ref[:, cols] = acc.astype(main_ref.dtype)


def _resident(shape, layer=None):
    if layer is None:
        return pl.BlockSpec(shape, lambda i: (0,) * len(shape), pipeline_mode=pl.Buffered(1))
    return pl.BlockSpec((None,) + tuple(shape), lambda i: (layer,) + (0,) * len(shape),
                        pipeline_mode=pl.Buffered(1))


def _in_proj(x, g, w_all, w_dt, layer, *, tm, tn):
    m, d = x.shape
    n = MAIN_WIDTH
    assert m % tm == 0 and n % tn == 0
    return pl.pallas_call(
        functools.partial(_in_proj_kernel, tn=tn),
        grid=(m // tm,),
        in_specs=[
            pl.BlockSpec((tm, d), lambda i: (i, 0)),
            _resident((1, d)),
            _resident(w_all.shape[1:], layer),
            _resident((d, LANES), layer),
        ],
        out_specs=[
            pl.BlockSpec((tm, n), lambda i: (i, 0)),
            pl.BlockSpec((tm, LANES), lambda i: (i, 0)),
        ],
        out_shape=[
            jax.ShapeDtypeStruct((m, n), BF16),
            jax.ShapeDtypeStruct((m, LANES), F32),
        ],
        scratch_shapes=[pltpu.VMEM((tm, d), BF16)],
        compiler_params=_params(("parallel",)),
        name="in_proj",
    )(x, g, w_all, w_dt)


def _attn_kernel(slopes_ref, q_ref, k_ref, v_ref, w1_ref, w2_ref, o_ref, w1o_ref, w2o_ref,
                 qf_ref, kf_ref, vf_ref, qm_ref, km_ref, vm_ref,
                 qd_ref, kd_ref, vd_ref, qh_ref, kh_ref, vh_ref, st0_ref, bias_ref):
    seq = q_ref.shape[0]
    blk = ATTN_BLOCK
    hp = pl.program_id(1)

    w1o_ref[...] = w1_ref[...].astype(BF16)
    w2o_ref[...] = w2_ref[...].astype(BF16)

    qf_ref[...] = q_ref[...].astype(F32)
    kf_ref[...] = k_ref[...].astype(F32)
    vf_ref[...] = v_ref[...].astype(F32)

    low = lax.broadcasted_iota(jnp.int32, (blk, LANES), 1) < HEAD_DIM
    ones_bf = jnp.ones((2 * blk, LANES), BF16)

    grp = 16
    row = lax.broadcasted_iota(jnp.int32, (2 * blk, 2 * blk), 0)
    col = lax.broadcasted_iota(jnp.int32, (2 * blk, 2 * blk), 1)
    query = ((row >> grp.bit_length()) << (grp.bit_length() - 1)) | (row & (grp - 1))
    delta = query - col + blk
    slope = jnp.where((row & grp) == 0, slopes_ref[HEAD_PAIR * hp], slopes_ref[HEAD_PAIR * hp + 1])

    def stack_heads(a, b):
        return jnp.concatenate([x[g * grp:(g + 1) * grp] for g in range(blk // grp) for x in (a, b)], axis=0)

    def unstack_heads(x):
        return tuple(jnp.concatenate([x[(2 * g + h) * grp:(2 * g + h + 1) * grp] for g in range(blk // grp)], axis=0)
                     for h in range(HEAD_PAIR))
    for t, (window, dil) in enumerate(DILATED_BRANCHES):
        valid = (delta >= 0) & (delta <= window // dil)
        bias = jnp.where(valid, -slope * (delta * dil).astype(F32) * LOG2_E, NEG_BIG)
        bias_ref[2 * t] = bias
        bias_ref[2 * t + 1] = jnp.where(col >= blk, bias, NEG_BIG)

    def pair_rows(x):
        first, second = unstack_heads(x)
        return jnp.where(low, jnp.broadcast_to(first, (blk, LANES)), jnp.broadcast_to(second, (blk, LANES)))

    def pair_halves(x):
        first, second = unstack_heads(x)
        return jnp.where(low, first, second)

    zero_bf = jnp.zeros((), BF16)
    n_blocks = seq // blk

    def block_rows(i):
        start = i * blk
        return pl.ds(start if isinstance(start, int) else pl.multiple_of(start, blk), blk)

    (_, d_lo), (_, d_mid), (_, d_hi) = DILATED_BRANCHES
    ratio = d_hi // d_mid
    assert d_lo == 1 and d_hi == d_mid * ratio and seq % (blk * d_hi) == 0
    mid_len = seq // d_mid

    def log2(n):
        assert n & (n - 1) == 0
        return n.bit_length() - 1

    def split_block(idx, dil):
        nb = n_blocks // dil
        return idx >> log2(nb), idx & (nb - 1)

    def token_rows(idx, dil):
        r, n = split_block(idx, dil)
        return pl.ds(r + n * (blk * dil), blk, stride=dil) if dil > 1 else block_rows(idx)

    def hi_rows_in_mid(idx):
        r, n = split_block(idx, d_hi)
        return pl.ds((r & (d_mid - 1)) * mid_len + n * (blk * ratio) + (r >> log2(d_mid)), blk, stride=ratio)

    def gather_mid(idx):
        rows = token_rows(idx, d_mid)
        for f_ref, mid_ref, d_ref in ((qf_ref, qm_ref, qd_ref), (kf_ref, km_ref, kd_ref), (vf_ref, vm_ref, vd_ref)):
            x = f_ref[rows, :]
            mid_ref[block_rows(idx), :] = x
            d_ref[block_rows(idx), :] = x.astype(BF16)

    def gather_hi(idx):
        rows = hi_rows_in_mid(idx)
        for mid_ref, h_ref in ((qm_ref, qh_ref), (km_ref, kh_ref), (vm_ref, vh_ref)):
            h_ref[block_rows(idx), :] = mid_ref[rows, :].astype(BF16)

    def block_softmax(t, idx, q_src, k_src, v_src):
        first = split_block(idx, DILATED_BRANCHES[t][1])[1] == 0
        cur = block_rows(idx)
        prev = block_rows(idx if first else idx - 1)
        q = q_src[cur, :]
        q2 = stack_heads(jnp.where(low, q, zero_bf), jnp.where(low, zero_bf, q))
        kk = jnp.concatenate([k_src[prev, :], k_src[cur, :]], axis=0)
        vv = jnp.concatenate([jnp.concatenate([v_src[prev, :], v_src[cur, :]], axis=0), ones_bf], axis=1)
        s = _dot_nt(q2, kk) + bias_ref[2 * t + int(first)]
        m = jnp.max(s, axis=-1, keepdims=True)
        p = jnp.exp2(s - m)
        pv = _dot(p.astype(BF16), vv)
        return pair_rows(m), pair_halves(pv[:, LANES:]), pair_halves(pv[:, :LANES])

    def combine(a, b):
        (ma, la, acca), (mb, lb, accb) = a, b
        m = jnp.maximum(ma, mb)
        wa, wb = jnp.exp2(ma - m), jnp.exp2(mb - m)
        return m, wa * la + wb * lb, wa * acca + wb * accb

    def load_state(st, rows):
        return tuple(ref[rows, :] for ref in st)

    def store_state(st, rows, state):
        for ref, x in zip(st, state):
            ref[rows, :] = x

    st0 = tuple(st0_ref.at[i] for i in range(3))
    st1 = (qf_ref, kf_ref, vf_ref)
    of_ref = qm_ref

    def branch0(idx):
        store_state(st0, block_rows(idx), block_softmax(0, idx, q_ref, k_ref, v_ref))
        gather_mid(idx)

    def branch1(idx):
        mine = block_softmax(1, idx, qd_ref, kd_ref, vd_ref)
        store_state(st1, block_rows(idx), combine(load_state(st0, token_rows(idx, d_mid)), mine))
        gather_hi(idx)

    def branch2(idx):
        mine = block_softmax(2, idx, qh_ref, kh_ref, vh_ref)
        _, l_all, acc_all = combine(load_state(st1, hi_rows_in_mid(idx)), mine)
        of_ref[token_rows(idx, d_hi), :] = acc_all / l_all

    for branch in (branch0, branch1, branch2):
        for idx in range(n_blocks):
            branch(idx)
    o_ref[...] = of_ref[...].astype(o_ref.dtype)


def _attention(slopes, main3, w1_all, w2_all, layer):
    b, seq, _ = main3.shape
    n_pairs = ATTN_WIDTH // LANES
    blk = ATTN_BLOCK
    n_steps = b * n_pairs
    w1_shape, w2_shape = w1_all.shape[1:], w2_all.shape[1:]
    r1, r2 = w1_shape[0] // n_steps, w2_shape[0] // n_steps
    assert r1 * n_steps == w1_shape[0] and r2 * n_steps == w2_shape[0]

    def spec(offset):
        return pl.BlockSpec((None, seq, LANES), lambda i, j: (i, 0, offset + j))

    return pl.pallas_call(
        _attn_kernel,
        grid=(b, n_pairs),
        in_specs=[
            pl.BlockSpec(memory_space=pltpu.SMEM),
            spec(0), spec(n_pairs), spec(2 * n_pairs),
            pl.BlockSpec((None, r1, w1_shape[1]), lambda i, j: (layer, i * n_pairs + j, 0)),
            pl.BlockSpec((None, r2, w2_shape[1]), lambda i, j: (layer, i * n_pairs + j, 0)),
        ],
        out_specs=[
            pl.BlockSpec((None, seq, LANES), lambda i, j: (i, 0, j)),
            pl.BlockSpec((r1, w1_shape[1]), lambda i, j: (i * n_pairs + j, 0)),
            pl.BlockSpec((r2, w2_shape[1]), lambda i, j: (i * n_pairs + j, 0)),
        ],
        out_shape=[
            jax.ShapeDtypeStruct((b, seq, ATTN_WIDTH), BF16),
            jax.ShapeDtypeStruct(w1_shape, BF16),
            jax.ShapeDtypeStruct(w2_shape, BF16),
        ],
        scratch_shapes=[pltpu.VMEM((seq, LANES), F32)] * 6
        + [pltpu.VMEM((seq, LANES), BF16)] * 6
        + [pltpu.VMEM((3, seq, LANES), F32)]
        + [pltpu.VMEM((2 * len(DILATED_BRANCHES), 2 * blk, 2 * blk), F32)],
        compiler_params=_params(("parallel", "parallel")),
        name="attention",
    )(slopes, main3, main3, main3, w1_all, w2_all)


def _ssd_kernel(z_ref, xs_ref, bc_ref, dt_ref, cw_ref, cb_ref, dtb_ref, alog_ref, dskip_ref, g_ref,
                y_ref, carry_ref, state_ref):
    tb = xs_ref.shape[0]
    q = SSD_CHUNK
    gw = SSD_WIDTH // SSD_GROUPS
    pairs_per_group = gw // LANES
    conv_cols = 512

    step = pl.program_id(1)
    slot = step & 1

    @pl.when(step == 0)
    def _():
        carry_ref[0] = jnp.zeros((q, CONV_CH), BF16)
        state_ref[...] = jnp.zeros(state_ref.shape, F32)

    carry_ref[1 - slot, :, 0:SSD_WIDTH] = xs_ref[tb - q:tb, :]
    carry_ref[1 - slot, :, SSD_WIDTH:CONV_CH] = bc_ref[tb - q:tb, :]

    low = lax.broadcasted_iota(jnp.int32, (q, LANES), 1) < SSD_HEAD_DIM
    ri = lax.broadcasted_iota(jnp.int32, (q, q), 0)
    ci = lax.broadcasted_iota(jnp.int32, (q, q), 1)
    tri = ri >= ci
    tri_bf = jnp.where(tri, 1.0, 0.0).astype(BF16)
    a_neg = -jnp.exp(alog_ref[...])

    sr = lax.broadcasted_iota(jnp.int32, (SSD_CONV * q, 2 * q), 0)
    sc = lax.broadcasted_iota(jnp.int32, (SSD_CONV * q, 2 * q), 1)
    src_row = (sr & (q - 1)) + (sr >> (q.bit_length() - 1)) + (q - (SSD_CONV - 1))
    shift = jnp.where(sc == src_row, 1.0, 0.0).astype(BF16)

    def cumsum_rows(v):
        hi = v.astype(BF16)
        r1 = v - hi.astype(F32)
        mid = r1.astype(BF16)
        lo = (r1 - mid.astype(F32)).astype(BF16)
        return _dot(tri_bf, hi) + _dot(tri_bf, mid) + _dot(tri_bf, lo)

    def lane_bcast(v, h):
        return jnp.broadcast_to(v[:, h:h + 1], (q, LANES))

    def chunk(c):
        r0 = c * q
        u_parts = []
        for j in range(CONV_CH // conv_cols):
            cols = slice(j * conv_cols, (j + 1) * conv_cols)
            src_ref, off = (xs_ref, 0) if j * conv_cols < SSD_WIDTH else (bc_ref, SSD_WIDTH)
            src_cols = slice(j * conv_cols - off, (j + 1) * conv_cols - off)
            prev = carry_ref[slot, :, cols] if c == 0 else src_ref[r0 - q:r0, src_cols]
            window = jnp.concatenate([prev, src_ref[r0:r0 + q, src_cols]], axis=0)
            taps = _dot(shift, window)
            uj = cb_ref[:, cols]
            for k in range(SSD_CONV):
                uj = uj + cw_ref[k:k + 1, cols] * taps[k * q:(k + 1) * q]
            u_parts.append(_silu(uj))
        u = jnp.concatenate(u_parts, axis=1)
        x = u[:, :SSD_WIDTH]
        dt_in = dt_ref[pl.ds(r0, q), :] + dtb_ref[...]
        dt = jnp.maximum(dt_in, 0.0) + jnp.log(1.0 + jnp.exp(-jnp.abs(dt_in)))
        acum = cumsum_rows(dt * a_neg) * LOG2_E
        acum_t = acum.T

        y_parts = []
        for g in range(SSD_GROUPS):
            bg = u[:, SSD_WIDTH + g * SSD_STATE:SSD_WIDTH + (g + 1) * SSD_STATE]
            cg = u[:, SSD_WIDTH + (SSD_GROUPS + g) * SSD_STATE:SSD_WIDTH + (SSD_GROUPS + g + 1) * SSD_STATE]
            cg_bf = cg.astype(BF16)
            cb = _dot_nt(cg_bf, bg.astype(BF16))
            st = state_ref[g]
            y_off = _dot(cg_bf, st.astype(BF16))
            xd_parts, decay_parts = [], []
            for pp in range(pairs_per_group):
                p = g * pairs_per_group + pp
                h0 = HEAD_PAIR * p
                xp = x[:, p * LANES:(p + 1) * LANES]
                cols = [lane_bcast(acum, h0 + e) for e in range(HEAD_PAIR)]
                colc = jnp.where(low, cols[0], cols[1])
                ms = []
                for e in range(HEAD_PAIR):
                    h = h0 + e
                    seg = jnp.exp2(jnp.where(tri, cols[e] - acum_t[h:h + 1, :], NEG_BIG))
                    ms.append(cb * seg)
                m_cat = jnp.concatenate(ms, axis=1).astype(BF16)
                dtc = jnp.where(low, lane_bcast(dt, h0), lane_bcast(dt, h0 + 1))
                xdt = xp * dtc
                x_blk = jnp.concatenate([jnp.where(low, xdt, 0.0), jnp.where(low, 0.0, xdt)],
                                        axis=0).astype(BF16)
                y_diag = _dot(m_cat, x_blk)
                y_o = y_off[:, pp * LANES:(pp + 1) * LANES] * jnp.exp2(colc)
                y_parts.append(y_diag + y_o + dskip_ref[:, p * LANES:(p + 1) * LANES] * xp)
                last = colc[q - 1:q, :]
                xd_parts.append(xdt * jnp.exp2(last - colc))
                decay_parts.append(jnp.exp2(last))
            xd = jnp.concatenate(xd_parts, axis=1).astype(BF16)
            decay = jnp.concatenate(decay_parts, axis=1)
            state_ref[g] = st * decay + _dot(bg.T.astype(BF16), xd)

        z = z_ref[pl.ds(r0, q), :].astype(F32)
        outs = []
        for g in range(SSD_GROUPS):
            yg = jnp.concatenate(y_parts[g * pairs_per_group:(g + 1) * pairs_per_group], axis=1)
            yg = yg * _silu(z[:, g * gw:(g + 1) * gw])
            outs.append(yg * _rms_scale(yg) * g_ref[:, g * gw:(g + 1) * gw])
        y_ref[pl.ds(r0, q), :] = jnp.concatenate(outs, axis=1).astype(y_ref.dtype)

    for c in range(tb // q):
        chunk(c)


def _ssd(main3, dt3, conv_w, conv_b, dt_bias, a_log, d_skip, norm_g, *, tb):
    b, seq, _ = main3.shape
    z_blk = (3 * ATTN_WIDTH) // SSD_WIDTH
    xs_blk = (3 * ATTN_WIDTH + SSD_WIDTH) // SSD_WIDTH
    bc_blk = (3 * ATTN_WIDTH + 2 * SSD_WIDTH) // BC_WIDTH

    def const(shape):
        return pl.BlockSpec(shape, lambda i, j: (0,) * len(shape))

    return pl.pallas_call(
        _ssd_kernel,
        grid=(b, seq // tb),
        in_specs=[
            pl.BlockSpec((None, tb, SSD_WIDTH), lambda i, j: (i, j, z_blk)),
            pl.BlockSpec((None, tb, SSD_WIDTH), lambda i, j: (i, j, xs_blk)),
            pl.BlockSpec((None, tb, BC_WIDTH), lambda i, j: (i, j, bc_blk)),
            pl.BlockSpec((None, tb, LANES), lambda i, j: (i, j, 0)),
            const((SSD_CONV, CONV_CH)), const((1, CONV_CH)),
            const((1, LANES)), const((1, LANES)),
            const((1, SSD_WIDTH)), const((1, SSD_WIDTH)),
        ],
        out_specs=pl.BlockSpec((None, tb, SSD_WIDTH), lambda i, j: (i, j, 0)),
        out_shape=jax.ShapeDtypeStruct((b, seq, SSD_WIDTH), BF16),
        scratch_shapes=[
            pltpu.VMEM((2, SSD_CHUNK, CONV_CH), BF16),
            pltpu.VMEM((SSD_GROUPS, SSD_STATE, SSD_WIDTH // SSD_GROUPS), F32),
        ],
        compiler_params=_params(("parallel", "arbitrary")),
        name="ssd",
    )(main3, main3, main3, dt3, conv_w, conv_b, dt_bias, a_log, d_skip, norm_g)


def _out_proj_kernel(a_ref, y_ref, x_ref, g_ref, w_ref, o_ref):
    a = a_ref[...].astype(F32)
    an = (a * _rms_scale(a) * g_ref[...]).astype(BF16)
    mixed = _dot(an, w_ref[0:ATTN_WIDTH, :]) + _dot(y_ref[...], w_ref[ATTN_WIDTH:, :])
    o_ref[...] = x_ref[...] + mixed


def _out_proj(attn, y, x, g, w_all, layer, *, tm):
    m, d = x.shape
    return pl.pallas_call(
        _out_proj_kernel,
        grid=(m // tm,),
        in_specs=[
            pl.BlockSpec((tm, ATTN_WIDTH), lambda i: (i, 0)),
            pl.BlockSpec((tm, SSD_WIDTH), lambda i: (i, 0)),
            pl.BlockSpec((tm, d), lambda i: (i, 0)),
            _resident((1, ATTN_WIDTH)),
            _resident(w_all.shape[1:], layer),
        ],
        out_specs=pl.BlockSpec((tm, d), lambda i: (i, 0)),
        out_shape=jax.ShapeDtypeStruct((m, d), F32),
        compiler_params=_params(("parallel",)),
        name="out_proj",
    )(attn, y, x, g, w_all)


def _mlp_kernel(x_ref, g_ref, w1_ref, w2_ref, gf_ref, o_ref, hn_ref, *, final_norm):
    f = pl.program_id(1)

    @pl.when(f == 0)
    def _():
        x = x_ref[...]
        hn_ref[...] = (x * _rms_scale(x) * g_ref[...]).astype(BF16)
        o_ref[...] = x

    h = jnp.maximum(_dot(hn_ref[...], w1_ref[...]), 0.0)
    o_ref[...] += _dot((h * h).astype(BF16), w2_ref[...])

    if final_norm:
        @pl.when(f == pl.num_programs(1) - 1)
        def _():
            o = o_ref[...]
            o_ref[...] = o * _rms_scale(o) * gf_ref[...]


def _mlp(x, g, w1, w2, gf, *, tm, tf, final_norm):
    m, d = x.shape
    dff = w1.shape[1]
    return pl.pallas_call(
        functools.partial(_mlp_kernel, final_norm=final_norm),
        grid=(m // tm, dff // tf),
        in_specs=[
            pl.BlockSpec((tm, d), lambda i, j: (i, 0)),
            pl.BlockSpec((1, d), lambda i, j: (0, 0)),
            pl.BlockSpec((d, tf), lambda i, j: (0, j)),
            pl.BlockSpec((tf, d), lambda i, j: (j, 0)),
            pl.BlockSpec((1, d), lambda i, j: (0, 0)),
        ],
        out_specs=pl.BlockSpec((tm, d), lambda i, j: (i, 0)),
        out_shape=jax.ShapeDtypeStruct((m, d), F32),
        scratch_shapes=[pltpu.VMEM((tm, d), BF16)],
        compiler_params=_params(("parallel", "arbitrary")),
        name="mlp",
    )(x, g, w1, w2, gf)


def _alibi_slopes():
    return jnp.asarray(2.0 ** (-8.0 * (np.arange(ATTN_HEADS) + 1) / ATTN_HEADS), dtype=F32)


def _pad_lanes(v):
    return jnp.pad(v, ((0, 0), (0, LANES - v.shape[1])))


def kernel(x, ln1_g, w_in, conv_w, conv_b, dt_bias, a_log, d_skip, attn_norm_g, ssd_norm_g, w_out,
           ln2_g, w_mlp_in, w_mlp_out, final_norm_g):
    b, seq, d = x.shape
    m = b * seq
    slopes = _alibi_slopes()
    xf = x.reshape(m, d)
    w_in_bf = w_in.astype(BF16)
    w_dt_bf = jnp.pad(w_in[:, :, MAIN_WIDTH:], ((0, 0), (0, 0), (0, LANES - (w_in.shape[2] - MAIN_WIDTH)))).astype(BF16)
    w_out_bf = w_out.astype(BF16)
    for l in range(DEPTH):
        main, dt_raw = _in_proj(xf, ln1_g[l][None, :], w_in_bf, w_dt_bf, l,
                                tm=IN_PROJ_ROWS, tn=IN_PROJ_COLS)
        main3 = main.reshape(b, seq, MAIN_WIDTH)
        attn, w1_bf, w2_bf = _attention(slopes, main3, w_mlp_in, w_mlp_out, l)
        y = _ssd(
            main3, dt_raw.reshape(b, seq, LANES), conv_w[l], conv_b[l][None, :],
            _pad_lanes(dt_bias[l][None, :]), _pad_lanes(a_log[l][None, :]),
            jnp.repeat(d_skip[l], SSD_HEAD_DIM)[None, :], ssd_norm_g[l][None, :], tb=SSD_TOKENS)
        x1 = _out_proj(attn.reshape(m, ATTN_WIDTH), y.reshape(m, SSD_WIDTH), xf,
                       attn_norm_g[l][None, :], w_out_bf, l, tm=OUT_PROJ_ROWS)
        xf = _mlp(x1, ln2_g[l][None, :], w1_bf, w2_bf,
                  final_norm_g[None, :], tm=MLP_ROWS, tf=MLP_HIDDEN, final_norm=(l == DEPTH - 1))
    return xf.reshape(b, seq, d)
```

```python
import functools

import jax
import jax.numpy as jnp
import numpy as np
from jax import lax
from jax.experimental import pallas as pl
from jax.experimental.pallas import tpu as pltpu

D_MODEL = 2048
DEPTH = 2
HEAD_DIM = 64
ATTN_WIDTH = D_MODEL // 2
ATTN_HEADS = ATTN_WIDTH // HEAD_DIM
DILATED_BRANCHES = ((128, 1), (512, 4), (2048, 16))
ATTN_BLOCK = 128
SSD_WIDTH = D_MODEL // 2
SSD_HEAD_DIM = 64
SSD_HEADS = SSD_WIDTH // SSD_HEAD_DIM
SSD_GROUPS = 2
SSD_STATE = 128
SSD_CONV = 4
SSD_CHUNK = 128
BC_WIDTH = 2 * SSD_GROUPS * SSD_STATE
CONV_CH = SSD_WIDTH + BC_WIDTH
MAIN_WIDTH = 3 * ATTN_WIDTH + SSD_WIDTH + CONV_CH
D_FF = 4 * D_MODEL
NORM_EPS = 1e-5

LANES = 128
HEAD_PAIR = LANES // HEAD_DIM
NEG_BIG = -1e30
LOG2_E = 1.4426950408889634
Q_PRESCALE = HEAD_DIM ** -0.5 * LOG2_E
VMEM_LIMIT = 56 * 1024 * 1024

IN_PROJ_ROWS, IN_PROJ_COLS = 512, 512
OUT_PROJ_ROWS = 512
MLP_ROWS, MLP_HIDDEN = 1024, 512
SSD_TOKENS = 1024

F32 = jnp.float32
BF16 = jnp.bfloat16


def _dot(a, b):
    return jnp.dot(a, b, preferred_element_type=F32)


def _dot_nt(a, b):
    return lax.dot_general(a, b, (((1,), (1,)), ((), ())), preferred_element_type=F32)


def _rms_scale(x):
    return lax.rsqrt(jnp.mean(x * x, axis=-1, keepdims=True) + NORM_EPS)


def _silu(x):
    return x * (1.0 / (1.0 + jnp.exp2(x * (-LOG2_E))))


def _params(semantics):
    return pltpu.CompilerParams(dimension_semantics=semantics, vmem_limit_bytes=VMEM_LIMIT)


def _in_proj_kernel(x_ref, g_ref, w_ref, wdt_ref, main_ref, dt_ref, hn_ref, *, tn):
    x = x_ref[...]
    hn_ref[...] = (x * _rms_scale(x) * g_ref[...]).astype(BF16)
    dt_ref[...] = _dot(hn_ref[...], wdt_ref[...])
    for j in range(main_ref.shape[1] // tn):
        cols = slice(j * tn, (j + 1) * tn)
        acc = _dot(hn_ref[...], w_ref[:, cols])
        if (j + 1) * tn <= ATTN_WIDTH:
            acc = acc * Q_PRESCALE
        main_ref[:, cols] = acc.astype(main_ref.dtype)


def _resident(shape, layer=None):
    if layer is None:
        return pl.BlockSpec(shape, lambda i: (0,) * len(shape), pipeline_mode=pl.Buffered(1))
    return pl.BlockSpec((None,) + tuple(shape), lambda i: (layer,) + (0,) * len(shape),
                        pipeline_mode=pl.Buffered(1))


def _in_proj(x, g, w_all, w_dt, layer, *, tm, tn):
    m, d = x.shape
    n = MAIN_WIDTH
    assert m % tm == 0 and n % tn == 0
    return pl.pallas_call(
        functools.partial(_in_proj_kernel, tn=tn),
        grid=(m // tm,),
        in_specs=[
            pl.BlockSpec((tm, d), lambda i: (i, 0)),
            _resident((1, d)),
            _resident(w_all.shape[1:], layer),
            _resident((d, LANES), layer),
        ],
        out_specs=[
            pl.BlockSpec((tm, n), lambda i: (i, 0)),
            pl.BlockSpec((tm, LANES), lambda i: (i, 0)),
        ],
        out_shape=[
            jax.ShapeDtypeStruct((m, n), BF16),
            jax.ShapeDtypeStruct((m, LANES), F32),
        ],
        scratch_shapes=[pltpu.VMEM((tm, d), BF16)],
        compiler_params=_params(("parallel",)),
        name="in_proj",
    )(x, g, w_all, w_dt)


def _attn_kernel(slopes_ref, q_ref, k_ref, v_ref, w1_ref, w2_ref, o_ref, w1o_ref, w2o_ref,
                 qf_ref, kf_ref, vf_ref, qm_ref, km_ref, vm_ref, of_ref,
                 qd_ref, kd_ref, vd_ref, qh_ref, kh_ref, vh_ref, st0_ref, st1_ref, bias_ref):
    seq = q_ref.shape[0]
    blk = ATTN_BLOCK
    hp = pl.program_id(1)

    w1o_ref[...] = w1_ref[...].astype(BF16)
    w2o_ref[...] = w2_ref[...].astype(BF16)

    qf_ref[...] = q_ref[...].astype(F32)
    kf_ref[...] = k_ref[...].astype(F32)
    vf_ref[...] = v_ref[...].astype(F32)

    low = lax.broadcasted_iota(jnp.int32, (blk, LANES), 1) < HEAD_DIM
    ones_bf = jnp.ones((2 * blk, LANES), BF16)

    row = lax.broadcasted_iota(jnp.int32, (2 * blk, 2 * blk), 0)
    col = lax.broadcasted_iota(jnp.int32, (2 * blk, 2 * blk), 1)
    delta = (row & (blk - 1)) - col + blk
    slope = jnp.where(row < blk, slopes_ref[HEAD_PAIR * hp], slopes_ref[HEAD_PAIR * hp + 1])
    for t, (window, dil) in enumerate(DILATED_BRANCHES):
        valid = (delta >= 0) & (delta <= window // dil)
        bias = jnp.where(valid, -slope * (delta * dil).astype(F32) * LOG2_E, NEG_BIG)
        bias_ref[2 * t] = bias
        bias_ref[2 * t + 1] = jnp.where(col >= blk, bias, NEG_BIG)

    def pair_rows(x):
        return jnp.where(low, jnp.broadcast_to(x[:blk], (blk, LANES)),
                         jnp.broadcast_to(x[blk:], (blk, LANES)))

    def pair_halves(x):
        return jnp.where(low, x[:blk], x[blk:])

    zero_bf = jnp.zeros((), BF16)
    n_blocks = seq // blk

    def block_rows(i):
        start = i * blk
        return pl.ds(start if isinstance(start, int) else pl.multiple_of(start, blk), blk)

    (_, d_lo), (_, d_mid), (_, d_hi) = DILATED_BRANCHES
    ratio = d_hi // d_mid
    assert d_lo == 1 and d_hi == d_mid * ratio and seq % (blk * d_hi) == 0
    mid_len = seq // d_mid

    def log2(n):
        assert n & (n - 1) == 0
        return n.bit_length() - 1

    def split_block(idx, dil):
        nb = n_blocks // dil
        return idx >> log2(nb), idx & (nb - 1)

    def token_rows(idx, dil):
        r, n = split_block(idx, dil)
        return pl.ds(r + n * (blk * dil), blk, stride=dil) if dil > 1 else block_rows(idx)

    def hi_rows_in_mid(idx):
        r, n = split_block(idx, d_hi)
        return pl.ds((r & (d_mid - 1)) * mid_len + n * (blk * ratio) + (r >> log2(d_mid)), blk, stride=ratio)

    def gather_mid(idx):
        rows = token_rows(idx, d_mid)
        for f_ref, mid_ref, d_ref in ((qf_ref, qm_ref, qd_ref), (kf_ref, km_ref, kd_ref), (vf_ref, vm_ref, vd_ref)):
            x = f_ref[rows, :]
            mid_ref[block_rows(idx), :] = x
            d_ref[block_rows(idx), :] = x.astype(BF16)

    def gather_hi(idx):
        rows = hi_rows_in_mid(idx)
        for mid_ref, h_ref in ((qm_ref, qh_ref), (km_ref, kh_ref), (vm_ref, vh_ref)):
            h_ref[block_rows(idx), :] = mid_ref[rows, :].astype(BF16)

    def block_softmax(t, idx, q_src, k_src, v_src):
        first = split_block(idx, DILATED_BRANCHES[t][1])[1] == 0
        cur = block_rows(idx)
        prev = block_rows(idx if first else idx - 1)
        q = q_src[cur, :]
        q2 = jnp.concatenate([jnp.where(low, q, zero_bf), jnp.where(low, zero_bf, q)], axis=0)
        kk = jnp.concatenate([k_src[prev, :], k_src[cur, :]], axis=0)
        vv = jnp.concatenate([v_src[prev, :], v_src[cur, :]], axis=0)
        s = _dot_nt(q2, kk) + bias_ref[2 * t + int(first)]
        m = jnp.max(s, axis=-1, keepdims=True)
        p = jnp.exp2(s - m)
        l = jnp.sum(p, axis=-1, keepdims=True)
        pv = _dot(p.astype(BF16), vv)
        return pair_rows(m), pair_rows(l), pair_halves(pv)

    def combine(a, b):
        (ma, la, acca), (mb, lb, accb) = a, b
        m = jnp.maximum(ma, mb)
        wa, wb = jnp.exp2(ma - m), jnp.exp2(mb - m)
        return m, wa * la + wb * lb, wa * acca + wb * accb

    def load_state(st_ref, rows):
        return tuple(st_ref[i, rows, :] for i in range(3))

    def store_state(st_ref, rows, state):
        for i, x in enumerate(state):
            st_ref[i, rows, :] = x

    def branch0(idx):
        store_state(st0_ref, block_rows(idx), block_softmax(0, idx, q_ref, k_ref, v_ref))
        gather_mid(idx)

    def branch1(idx):
        mine = block_softmax(1, idx, qd_ref, kd_ref, vd_ref)
        store_state(st1_ref, block_rows(idx), combine(load_state(st0_ref, token_rows(idx, d_mid)), mine))
        gather_hi(idx)

    def branch2(idx):
        mine = block_softmax(2, idx, qh_ref, kh_ref, vh_ref)
        _, l_all, acc_all = combine(load_state(st1_ref, hi_rows_in_mid(idx)), mine)
        of_ref[token_rows(idx, d_hi), :] = acc_all / l_all

    for branch in (branch0, branch1, branch2):
        for idx in range(n_blocks):
            branch(idx)
    o_ref[...] = of_ref[...].astype(o_ref.dtype)


def _attention(slopes, main3, w1_all, w2_all, layer):
    b, seq, _ = main3.shape
    n_pairs = ATTN_WIDTH // LANES
    blk = ATTN_BLOCK
    n_steps = b * n_pairs
    w1_shape, w2_shape = w1_all.shape[1:], w2_all.shape[1:]
    r1, r2 = w1_shape[0] // n_steps, w2_shape[0] // n_steps
    assert r1 * n_steps == w1_shape[0] and r2 * n_steps == w2_shape[0]

    def spec(offset):
        return pl.BlockSpec((None, seq, LANES), lambda i, j: (i, 0, offset + j))

    return pl.pallas_call(
        _attn_kernel,
        grid=(b, n_pairs),
        in_specs=[
            pl.BlockSpec(memory_space=pltpu.SMEM),
            spec(0), spec(n_pairs), spec(2 * n_pairs),
            pl.BlockSpec((None, r1, w1_shape[1]), lambda i, j: (layer, i * n_pairs + j, 0)),
            pl.BlockSpec((None, r2, w2_shape[1]), lambda i, j: (layer, i * n_pairs + j, 0)),
        ],
        out_specs=[
            pl.BlockSpec((None, seq, LANES), lambda i, j: (i, 0, j)),
            pl.BlockSpec((r1, w1_shape[1]), lambda i, j: (i * n_pairs + j, 0)),
            pl.BlockSpec((r2, w2_shape[1]), lambda i, j: (i * n_pairs + j, 0)),
        ],
        out_shape=[
            jax.ShapeDtypeStruct((b, seq, ATTN_WIDTH), BF16),
            jax.ShapeDtypeStruct(w1_shape, BF16),
            jax.ShapeDtypeStruct(w2_shape, BF16),
        ],
        scratch_shapes=[pltpu.VMEM((seq, LANES), F32)] * 7
        + [pltpu.VMEM((seq, LANES), BF16)] * 6
        + [pltpu.VMEM((3, seq, LANES), F32)] * 2
        + [pltpu.VMEM((2 * len(DILATED_BRANCHES), 2 * blk, 2 * blk), F32)],
        compiler_params=_params(("parallel", "parallel")),
        name="attention",
    )(slopes, main3, main3, main3, w1_all, w2_all)


def _ssd_kernel(z_ref, xs_ref, bc_ref, dt_ref, cw_ref, cb_ref, dtb_ref, alog_ref, dskip_ref, g_ref,
                y_ref, carry_ref, state_ref):
    tb = xs_ref.shape[0]
    q = SSD_CHUNK
    gw = SSD_WIDTH // SSD_GROUPS
    pairs_per_group = gw // LANES
    conv_cols = 512

    step = pl.program_id(1)
    slot = step & 1

    @pl.when(step == 0)
    def _():
        carry_ref[0] = jnp.zeros((q, CONV_CH), BF16)
        state_ref[...] = jnp.zeros(state_ref.shape, F32)

    carry_ref[1 - slot, :, 0:SSD_WIDTH] = xs_ref[tb - q:tb, :]
    carry_ref[1 - slot, :, SSD_WIDTH:CONV_CH] = bc_ref[tb - q:tb, :]

    low = lax.broadcasted_iota(jnp.int32, (q, LANES), 1) < SSD_HEAD_DIM
    ri = lax.broadcasted_iota(jnp.int32, (q, q), 0)
    ci = lax.broadcasted_iota(jnp.int32, (q, q), 1)
    tri = ri >= ci
    tri_bf = jnp.where(tri, 1.0, 0.0).astype(BF16)
    a_neg = -jnp.exp(alog_ref[...])

    sr = lax.broadcasted_iota(jnp.int32, (SSD_CONV * q, 2 * q), 0)
    sc = lax.broadcasted_iota(jnp.int32, (SSD_CONV * q, 2 * q), 1)
    src_row = (sr & (q - 1)) + (sr >> (q.bit_length() - 1)) + (q - (SSD_CONV - 1))
    shift = jnp.where(sc == src_row, 1.0, 0.0).astype(BF16)

    def cumsum_rows(v):
        hi = v.astype(BF16)
        r1 = v - hi.astype(F32)
        mid = r1.astype(BF16)
        lo = (r1 - mid.astype(F32)).astype(BF16)
        return _dot(tri_bf, hi) + _dot(tri_bf, mid) + _dot(tri_bf, lo)

    def lane_bcast(v, h):
        return jnp.broadcast_to(v[:, h:h + 1], (q, LANES))

    def chunk(c):
        r0 = c * q
        u_parts = []
        for j in range(CONV_CH // conv_cols):
            cols = slice(j * conv_cols, (j + 1) * conv_cols)
            src_ref, off = (xs_ref, 0) if j * conv_cols < SSD_WIDTH else (bc_ref, SSD_WIDTH)
            src_cols = slice(j * conv_cols - off, (j + 1) * conv_cols - off)
            prev = carry_ref[slot, :, cols] if c == 0 else src_ref[r0 - q:r0, src_cols]
            window = jnp.concatenate([prev, src_ref[r0:r0 + q, src_cols]], axis=0)
            taps = _dot(shift, window)
            uj = cb_ref[:, cols]
            for k in range(SSD_CONV):
                uj = uj + cw_ref[k:k + 1, cols] * taps[k * q:(k + 1) * q]
            u_parts.append(_silu(uj))
        u = jnp.concatenate(u_parts, axis=1)
        x = u[:, :SSD_WIDTH]
        dt_in = dt_ref[pl.ds(r0, q), :] + dtb_ref[...]
        dt = jnp.maximum(dt_in, 0.0) + jnp.log(1.0 + jnp.exp(-jnp.abs(dt_in)))
        acum = cumsum_rows(dt * a_neg) * LOG2_E
        acum_t = acum.T

        y_parts = []
        for g in range(SSD_GROUPS):
            bg = u[:, SSD_WIDTH + g * SSD_STATE:SSD_WIDTH + (g + 1) * SSD_STATE]
            cg = u[:, SSD_WIDTH + (SSD_GROUPS + g) * SSD_STATE:SSD_WIDTH + (SSD_GROUPS + g + 1) * SSD_STATE]
            cg_bf = cg.astype(BF16)
            cb = _dot_nt(cg_bf, bg.astype(BF16))
            st = state_ref[g]
            y_off = _dot(cg_bf, st.astype(BF16))
            xd_parts, decay_parts = [], []
            for pp in range(pairs_per_group):
                p = g * pairs_per_group + pp
                h0 = HEAD_PAIR * p
                xp = x[:, p * LANES:(p + 1) * LANES]
                cols = [lane_bcast(acum, h0 + e) for e in range(HEAD_PAIR)]
                colc = jnp.where(low, cols[0], cols[1])
                ms = []
                for e in range(HEAD_PAIR):
                    h = h0 + e
                    seg = jnp.exp2(jnp.where(tri, cols[e] - acum_t[h:h + 1, :], NEG_BIG))
                    ms.append(cb * seg)
                m_cat = jnp.concatenate(ms, axis=1).astype(BF16)
                dtc = jnp.where(low, lane_bcast(dt, h0), lane_bcast(dt, h0 + 1))
                xdt = xp * dtc
                x_blk = jnp.concatenate([jnp.where(low, xdt, 0.0), jnp.where(low, 0.0, xdt)],
                                        axis=0).astype(BF16)
                y_diag = _dot(m_cat, x_blk)
                y_o = y_off[:, pp * LANES:(pp + 1) * LANES] * jnp.exp2(colc)
                y_parts.append(y_diag + y_o + dskip_ref[:, p * LANES:(p + 1) * LANES] * xp)
                last = colc[q - 1:q, :]
                xd_parts.append(xdt * jnp.exp2(last - colc))
                decay_parts.append(jnp.exp2(last))
            xd = jnp.concatenate(xd_parts, axis=1).astype(BF16)
            decay = jnp.concatenate(decay_parts, axis=1)
            state_ref[g] = st * decay + _dot(bg.T.astype(BF16), xd)

        z = z_ref[pl.ds(r0, q), :].astype(F32)
        outs = []
        for g in range(SSD_GROUPS):
            yg = jnp.concatenate(y_parts[g * pairs_per_group:(g + 1) * pairs_per_group], axis=1)
            yg = yg * _silu(z[:, g * gw:(g + 1) * gw])
            outs.append(yg * _rms_scale(yg) * g_ref[:, g * gw:(g + 1) * gw])
        y_ref[pl.ds(r0, q), :] = jnp.concatenate(outs, axis=1).astype(y_ref.dtype)

    for c in range(tb // q):
        chunk(c)


def _ssd(main3, dt3, conv_w, conv_b, dt_bias, a_log, d_skip, norm_g, *, tb):
    b, seq, _ = main3.shape
    z_blk = (3 * ATTN_WIDTH) // SSD_WIDTH
    xs_blk = (3 * ATTN_WIDTH + SSD_WIDTH) // SSD_WIDTH
    bc_blk = (3 * ATTN_WIDTH + 2 * SSD_WIDTH) // BC_WIDTH

    def const(shape):
        return pl.BlockSpec(shape, lambda i, j: (0,) * len(shape))

    return pl.pallas_call(
        _ssd_kernel,
        grid=(b, seq // tb),
        in_specs=[
            pl.BlockSpec((None, tb, SSD_WIDTH), lambda i, j: (i, j, z_blk)),
            pl.BlockSpec((None, tb, SSD_WIDTH), lambda i, j: (i, j, xs_blk)),
            pl.BlockSpec((None, tb, BC_WIDTH), lambda i, j: (i, j, bc_blk)),
            pl.BlockSpec((None, tb, LANES), lambda i, j: (i, j, 0)),
            const((SSD_CONV, CONV_CH)), const((1, CONV_CH)),
            const((1, LANES)), const((1, LANES)),
            const((1, SSD_WIDTH)), const((1, SSD_WIDTH)),
        ],
        out_specs=pl.BlockSpec((None, tb, SSD_WIDTH), lambda i, j: (i, j, 0)),
        out_shape=jax.ShapeDtypeStruct((b, seq, SSD_WIDTH), BF16),
        scratch_shapes=[
            pltpu.VMEM((2, SSD_CHUNK, CONV_CH), BF16),
            pltpu.VMEM((SSD_GROUPS, SSD_STATE, SSD_WIDTH // SSD_GROUPS), F32),
        ],
        compiler_params=_params(("parallel", "arbitrary")),
        name="ssd",
    )(main3, main3, main3, dt3, conv_w, conv_b, dt_bias, a_log, d_skip, norm_g)


def _out_proj_kernel(a_ref, y_ref, x_ref, g_ref, w_ref, o_ref):
    a = a_ref[...].astype(F32)
    an = (a * _rms_scale(a) * g_ref[...]).astype(BF16)
    mixed = _dot(an, w_ref[0:ATTN_WIDTH, :]) + _dot(y_ref[...], w_ref[ATTN_WIDTH:, :])
    o_ref[...] = x_ref[...] + mixed


def _out_proj(attn, y, x, g, w_all, layer, *, tm):
    m, d = x.shape
    return pl.pallas_call(
        _out_proj_kernel,
        grid=(m // tm,),
        in_specs=[
            pl.BlockSpec((tm, ATTN_WIDTH), lambda i: (i, 0)),
            pl.BlockSpec((tm, SSD_WIDTH), lambda i: (i, 0)),
            pl.BlockSpec((tm, d), lambda i: (i, 0)),
            _resident((1, ATTN_WIDTH)),
            _resident(w_all.shape[1:], layer),
        ],
        out_specs=pl.BlockSpec((tm, d), lambda i: (i, 0)),
        out_shape=jax.ShapeDtypeStruct((m, d), F32),
        compiler_params=_params(("parallel",)),
        name="out_proj",
    )(attn, y, x, g, w_all)


def _mlp_kernel(x_ref, g_ref, w1_ref, w2_ref, gf_ref, o_ref, hn_ref, *, final_norm):
    f = pl.program_id(1)

    @pl.when(f == 0)
    def _():
        x = x_ref[...]
        hn_ref[...] = (x * _rms_scale(x) * g_ref[...]).astype(BF16)
        o_ref[...] = x

    h = jnp.maximum(_dot(hn_ref[...], w1_ref[...]), 0.0)
    o_ref[...] += _dot((h * h).astype(BF16), w2_ref[...])

    if final_norm:
        @pl.when(f == pl.num_programs(1) - 1)
        def _():
            o = o_ref[...]
            o_ref[...] = o * _rms_scale(o) * gf_ref[...]


def _mlp(x, g, w1, w2, gf, *, tm, tf, final_norm):
    m, d = x.shape
    dff = w1.shape[1]
    return pl.pallas_call(
        functools.partial(_mlp_kernel, final_norm=final_norm),
        grid=(m // tm, dff // tf),
        in_specs=[
            pl.BlockSpec((tm, d), lambda i, j: (i, 0)),
            pl.BlockSpec((1, d), lambda i, j: (0, 0)),
            pl.BlockSpec((d, tf), lambda i, j: (0, j)),
            pl.BlockSpec((tf, d), lambda i, j: (j, 0)),
            pl.BlockSpec((1, d), lambda i, j: (0, 0)),
        ],
        out_specs=pl.BlockSpec((tm, d), lambda i, j: (i, 0)),
        out_shape=jax.ShapeDtypeStruct((m, d), F32),
        scratch_shapes=[pltpu.VMEM((tm, d), BF16)],
        compiler_params=_params(("parallel", "arbitrary")),
        name="mlp",
    )(x, g, w1, w2, gf)


def _alibi_slopes():
    return jnp.asarray(2.0 ** (-8.0 * (np.arange(ATTN_HEADS) + 1) / ATTN_HEADS), dtype=F32)


def _pad_lanes(v):
    return jnp.pad(v, ((0, 0), (0, LANES - v.shape[1])))


def kernel(x, ln1_g, w_in, conv_w, conv_b, dt_bias, a_log, d_skip, attn_norm_g, ssd_norm_g, w_out,
           ln2_g, w_mlp_in, w_mlp_out, final_norm_g):
    b, seq, d = x.shape
    m = b * seq
    slopes = _alibi_slopes()
    xf = x.reshape(m, d)
    w_in_bf = w_in.astype(BF16)
    w_dt_bf = jnp.pad(w_in[:, :, MAIN_WIDTH:], ((0, 0), (0, 0), (0, LANES - (w_in.shape[2] - MAIN_WIDTH)))).astype(BF16)
    w_out_bf = w_out.astype(BF16)
    for l in range(DEPTH):
        main, dt_raw = _in_proj(xf, ln1_g[l][None, :], w_in_bf, w_dt_bf, l,
                                tm=IN_PROJ_ROWS, tn=IN_PROJ_COLS)
        main3 = main.reshape(b, seq, MAIN_WIDTH)
        attn, w1_bf, w2_bf = _attention(slopes, main3, w_mlp_in, w_mlp_out, l)
        y = _ssd(
            main3, dt_raw.reshape(b, seq, LANES), conv_w[l], conv_b[l][None, :],
            _pad_lanes(dt_bias[l][None, :]), _pad_lanes(a_log[l][None, :]),
            jnp.repeat(d_skip[l], SSD_HEAD_DIM)[None, :], ssd_norm_g[l][None, :], tb=SSD_TOKENS)
        x1 = _out_proj(attn.reshape(m, ATTN_WIDTH), y.reshape(m, SSD_WIDTH), xf,
                       attn_norm_g[l][None, :], w_out_bf, l, tm=OUT_PROJ_ROWS)
        xf = _mlp(x1, ln2_g[l][None, :], w1_bf, w2_bf,
                  final_norm_g[None, :], tm=MLP_ROWS, tf=MLP_HIDDEN, final_norm=(l == DEPTH - 1))
    return xf.reshape(b, seq, d)
```

```python
import functools

import jax
import jax.numpy as jnp
import numpy as np
from jax import lax
from jax.experimental import pallas as pl
from jax.experimental.pallas import tpu as pltpu

D_MODEL = 2048
DEPTH = 2
HEAD_DIM = 64
ATTN_WIDTH = D_MODEL // 2
ATTN_HEADS = ATTN_WIDTH // HEAD_DIM
DILATED_BRANCHES = ((128, 1), (512, 4), (2048, 16))
ATTN_BLOCK = 128
SSD_WIDTH = D_MODEL // 2
SSD_HEAD_DIM = 64
SSD_HEADS = SSD_WIDTH // SSD_HEAD_DIM
SSD_GROUPS = 2
SSD_STATE = 128
SSD_CONV = 4
SSD_CHUNK = 128
BC_WIDTH = 2 * SSD_GROUPS * SSD_STATE
CONV_CH = SSD_WIDTH + BC_WIDTH
MAIN_WIDTH = 3 * ATTN_WIDTH + SSD_WIDTH + CONV_CH
D_FF = 4 * D_MODEL
NORM_EPS = 1e-5

LANES = 128
HEAD_PAIR = LANES // HEAD_DIM
NEG_BIG = -1e30
LOG2_E = 1.4426950408889634
Q_PRESCALE = HEAD_DIM ** -0.5 * LOG2_E
VMEM_LIMIT = 56 * 1024 * 1024

IN_PROJ_ROWS, IN_PROJ_COLS = 512, 512
OUT_PROJ_ROWS = 512
MLP_ROWS, MLP_HIDDEN = 1024, 512
SSD_TOKENS = 1024

F32 = jnp.float32
BF16 = jnp.bfloat16


def _dot(a, b):
    return jnp.dot(a, b, preferred_element_type=F32)


def _dot_nt(a, b):
    return lax.dot_general(a, b, (((1,), (1,)), ((), ())), preferred_element_type=F32)


def _rms_scale(x):
    return lax.rsqrt(jnp.mean(x * x, axis=-1, keepdims=True) + NORM_EPS)


def _silu(x):
    return x * (1.0 / (1.0 + jnp.exp2(x * (-LOG2_E))))


def _params(semantics):
    return pltpu.CompilerParams(dimension_semantics=semantics, vmem_limit_bytes=VMEM_LIMIT)


def _in_proj_kernel(x_ref, g_ref, w_ref, wdt_ref, main_ref, dt_ref, hn_ref, *, tn):
    x = x_ref[...]
    hn_ref[...] = (x * _rms_scale(x) * g_ref[...]).astype(BF16)
    dt_ref[...] = _dot(hn_ref[...], wdt_ref[...])
    for j in range(main_ref.shape[1] // tn):
        cols = slice(j * tn, (j + 1) * tn)
        acc = _dot(hn_ref[...], w_ref[:, cols])
        if (j + 1) * tn <= ATTN_WIDTH:
            acc = acc * Q_PRESCALE
        main_ref[:, cols] = acc.astype(main_ref.dtype)


def _resident(shape, layer=None):
    if layer is None:
        return pl.BlockSpec(shape, lambda i: (0,) * len(shape), pipeline_mode=pl.Buffered(1))
    return pl.BlockSpec((None,) + tuple(shape), lambda i: (layer,) + (0,) * len(shape),
                        pipeline_mode=pl.Buffered(1))


def _in_proj(x, g, w_all, w_dt, layer, *, tm, tn):
    m, d = x.shape
    n = MAIN_WIDTH
    assert m % tm == 0 and n % tn == 0
    return pl.pallas_call(
        functools.partial(_in_proj_kernel, tn=tn),
        grid=(m // tm,),
        in_specs=[
            pl.BlockSpec((tm, d), lambda i: (i, 0)),
            _resident((1, d)),
            _resident(w_all.shape[1:], layer),
            _resident((d, LANES), layer),
        ],
        out_specs=[
            pl.BlockSpec((tm, n), lambda i: (i, 0)),
            pl.BlockSpec((tm, LANES), lambda i: (i, 0)),
        ],
        out_shape=[
            jax.ShapeDtypeStruct((m, n), BF16),
            jax.ShapeDtypeStruct((m, LANES), F32),
        ],
        scratch_shapes=[pltpu.VMEM((tm, d), BF16)],
        compiler_params=_params(("parallel",)),
        name="in_proj",
    )(x, g, w_all, w_dt)


def _attn_kernel(slopes_ref, q_ref, k_ref, v_ref, w1_ref, w2_ref, o_ref, w1o_ref, w2o_ref,
                 qf_ref, kf_ref, vf_ref, qm_ref, km_ref, vm_ref, of_ref,
                 qd_ref, kd_ref, vd_ref, qh_ref, kh_ref, vh_ref, st0_ref, st1_ref, bias_ref):
    seq = q_ref.shape[0]
    blk = ATTN_BLOCK
    hp = pl.program_id(1)

    w1o_ref[...] = w1_ref[...].astype(BF16)
    w2o_ref[...] = w2_ref[...].astype(BF16)

    qf_ref[...] = q_ref[...].astype(F32)
    kf_ref[...] = k_ref[...].astype(F32)
    vf_ref[...] = v_ref[...].astype(F32)

    low = lax.broadcasted_iota(jnp.int32, (blk, LANES), 1) < HEAD_DIM
    ones_bf = jnp.ones((2 * blk, LANES), BF16)

    row = lax.broadcasted_iota(jnp.int32, (2 * blk, 2 * blk), 0)
    col = lax.broadcasted_iota(jnp.int32, (2 * blk, 2 * blk), 1)
    delta = (row & (blk - 1)) - col + blk
    slope = jnp.where(row < blk, slopes_ref[HEAD_PAIR * hp], slopes_ref[HEAD_PAIR * hp + 1])
    for t, (window, dil) in enumerate(DILATED_BRANCHES):
        valid = (delta >= 0) & (delta <= window // dil)
        bias = jnp.where(valid, -slope * (delta * dil).astype(F32) * LOG2_E, NEG_BIG)
        bias_ref[2 * t] = bias
        bias_ref[2 * t + 1] = jnp.where(col >= blk, bias, NEG_BIG)

    def pair_rows(x):
        return jnp.where(low, jnp.broadcast_to(x[:blk], (blk, LANES)),
                         jnp.broadcast_to(x[blk:], (blk, LANES)))

    def pair_halves(x):
        return jnp.where(low, x[:blk], x[blk:])

    zero_bf = jnp.zeros((), BF16)
    n_blocks = seq // blk

    def block_rows(i):
        start = i * blk
        return pl.ds(start if isinstance(start, int) else pl.multiple_of(start, blk), blk)

    (_, d_lo), (_, d_mid), (_, d_hi) = DILATED_BRANCHES
    ratio = d_hi // d_mid
    assert d_lo == 1 and d_hi == d_mid * ratio and seq % (blk * d_hi) == 0
    mid_len = seq // d_mid

    def log2(n):
        assert n & (n - 1) == 0
        return n.bit_length() - 1

    def split_block(idx, dil):
        nb = n_blocks // dil
        return idx >> log2(nb), idx & (nb - 1)

    def token_rows(idx, dil):
        r, n = split_block(idx, dil)
        return pl.ds(r + n * (blk * dil), blk, stride=dil) if dil > 1 else block_rows(idx)

    def hi_rows_in_mid(idx):
        r, n = split_block(idx, d_hi)
        return pl.ds((r & (d_mid - 1)) * mid_len + n * (blk * ratio) + (r >> log2(d_mid)), blk, stride=ratio)

    def gather_mid(idx):
        rows = token_rows(idx, d_mid)
        for f_ref, mid_ref, d_ref in ((qf_ref, qm_ref, qd_ref), (kf_ref, km_ref, kd_ref), (vf_ref, vm_ref, vd_ref)):
            x = f_ref[rows, :]
            mid_ref[block_rows(idx), :] = x
            d_ref[block_rows(idx), :] = x.astype(BF16)

    def gather_hi(idx):
        rows = hi_rows_in_mid(idx)
        for mid_ref, h_ref in ((qm_ref, qh_ref), (km_ref, kh_ref), (vm_ref, vh_ref)):
            h_ref[block_rows(idx), :] = mid_ref[rows, :].astype(BF16)

    def block_softmax(t, idx, q_src, k_src, v_src):
        first = split_block(idx, DILATED_BRANCHES[t][1])[1] == 0
        cur = block_rows(idx)
        prev = block_rows(idx if first else idx - 1)
        q = q_src[cur, :]
        q2 = jnp.concatenate([jnp.where(low, q, zero_bf), jnp.where(low, zero_bf, q)], axis=0)
        kk = jnp.concatenate([k_src[prev, :], k_src[cur, :]], axis=0)
        vv = jnp.concatenate([jnp.concatenate([v_src[prev, :], v_src[cur, :]], axis=0), ones_bf], axis=1)
        s = _dot_nt(q2, kk) + bias_ref[2 * t + int(first)]
        m = jnp.max(s, axis=-1, keepdims=True)
        p = jnp.exp2(s - m)
        pv = _dot(p.astype(BF16), vv)
        return pair_rows(m), pair_halves(pv[:, LANES:]), pair_halves(pv[:, :LANES])

    def combine(a, b):
        (ma, la, acca), (mb, lb, accb) = a, b
        m = jnp.maximum(ma, mb)
        wa, wb = jnp.exp2(ma - m), jnp.exp2(mb - m)
        return m, wa * la + wb * lb, wa * acca + wb * accb

    def load_state(st_ref, rows):
        return tuple(st_ref[i, rows, :] for i in range(3))

    def store_state(st_ref, rows, state):
        for i, x in enumerate(state):
            st_ref[i, rows, :] = x

    def branch0(idx):
        store_state(st0_ref, block_rows(idx), block_softmax(0, idx, q_ref, k_ref, v_ref))
        gather_mid(idx)

    def branch1(idx):
        mine = block_softmax(1, idx, qd_ref, kd_ref, vd_ref)
        store_state(st1_ref, block_rows(idx), combine(load_state(st0_ref, token_rows(idx, d_mid)), mine))
        gather_hi(idx)

    def branch2(idx):
        mine = block_softmax(2, idx, qh_ref, kh_ref, vh_ref)
        _, l_all, acc_all = combine(load_state(st1_ref, hi_rows_in_mid(idx)), mine)
        of_ref[token_rows(idx, d_hi), :] = acc_all / l_all

    for branch in (branch0, branch1, branch2):
        for idx in range(n_blocks):
            branch(idx)
    o_ref[...] = of_ref[...].astype(o_ref.dtype)


def _attention(slopes, main3, w1_all, w2_all, layer):
    b, seq, _ = main3.shape
    n_pairs = ATTN_WIDTH // LANES
    blk = ATTN_BLOCK
    n_steps = b * n_pairs
    w1_shape, w2_shape = w1_all.shape[1:], w2_all.shape[1:]
    r1, r2 = w1_shape[0] // n_steps, w2_shape[0] // n_steps
    assert r1 * n_steps == w1_shape[0] and r2 * n_steps == w2_shape[0]

    def spec(offset):
        return pl.BlockSpec((None, seq, LANES), lambda i, j: (i, 0, offset + j))

    return pl.pallas_call(
        _attn_kernel,
        grid=(b, n_pairs),
        in_specs=[
            pl.BlockSpec(memory_space=pltpu.SMEM),
            spec(0), spec(n_pairs), spec(2 * n_pairs),
            pl.BlockSpec((None, r1, w1_shape[1]), lambda i, j: (layer, i * n_pairs + j, 0)),
            pl.BlockSpec((None, r2, w2_shape[1]), lambda i, j: (layer, i * n_pairs + j, 0)),
        ],
        out_specs=[
            pl.BlockSpec((None, seq, LANES), lambda i, j: (i, 0, j)),
            pl.BlockSpec((r1, w1_shape[1]), lambda i, j: (i * n_pairs + j, 0)),
            pl.BlockSpec((r2, w2_shape[1]), lambda i, j: (i * n_pairs + j, 0)),
        ],
        out_shape=[
            jax.ShapeDtypeStruct((b, seq, ATTN_WIDTH), BF16),
            jax.ShapeDtypeStruct(w1_shape, BF16),
            jax.ShapeDtypeStruct(w2_shape, BF16),
        ],
        scratch_shapes=[pltpu.VMEM((seq, LANES), F32)] * 7
        + [pltpu.VMEM((seq, LANES), BF16)] * 6
        + [pltpu.VMEM((3, seq, LANES), F32)] * 2
        + [pltpu.VMEM((2 * len(DILATED_BRANCHES), 2 * blk, 2 * blk), F32)],
        compiler_params=_params(("parallel", "parallel")),
        name="attention",
    )(slopes, main3, main3, main3, w1_all, w2_all)


def _ssd_kernel(z_ref, xs_ref, bc_ref, dt_ref, cw_ref, cb_ref, dtb_ref, alog_ref, dskip_ref, g_ref,
                y_ref, carry_ref, state_ref):
    tb = xs_ref.shape[0]
    q = SSD_CHUNK
    gw = SSD_WIDTH // SSD_GROUPS
    pairs_per_group = gw // LANES
    conv_cols = 512

    step = pl.program_id(1)
    slot = step & 1

    @pl.when(step == 0)
    def _():
        carry_ref[0] = jnp.zeros((q, CONV_CH), BF16)
        state_ref[...] = jnp.zeros(state_ref.shape, F32)

    carry_ref[1 - slot, :, 0:SSD_WIDTH] = xs_ref[tb - q:tb, :]
    carry_ref[1 - slot, :, SSD_WIDTH:CONV_CH] = bc_ref[tb - q:tb, :]

    low = lax.broadcasted_iota(jnp.int32, (q, LANES), 1) < SSD_HEAD_DIM
    ri = lax.broadcasted_iota(jnp.int32, (q, q), 0)
    ci = lax.broadcasted_iota(jnp.int32, (q, q), 1)
    tri = ri >= ci
    tri_bf = jnp.where(tri, 1.0, 0.0).astype(BF16)
    a_neg = -jnp.exp(alog_ref[...])

    sr = lax.broadcasted_iota(jnp.int32, (SSD_CONV * q, 2 * q), 0)
    sc = lax.broadcasted_iota(jnp.int32, (SSD_CONV * q, 2 * q), 1)
    src_row = (sr & (q - 1)) + (sr >> (q.bit_length() - 1)) + (q - (SSD_CONV - 1))
    shift = jnp.where(sc == src_row, 1.0, 0.0).astype(BF16)

    def cumsum_rows(v):
        hi = v.astype(BF16)
        r1 = v - hi.astype(F32)
        mid = r1.astype(BF16)
        lo = (r1 - mid.astype(F32)).astype(BF16)
        return _dot(tri_bf, hi) + _dot(tri_bf, mid) + _dot(tri_bf, lo)

    def lane_bcast(v, h):
        return jnp.broadcast_to(v[:, h:h + 1], (q, LANES))

    def chunk(c):
        r0 = c * q
        u_parts = []
        for j in range(CONV_CH // conv_cols):
            cols = slice(j * conv_cols, (j + 1) * conv_cols)
            src_ref, off = (xs_ref, 0) if j * conv_cols < SSD_WIDTH else (bc_ref, SSD_WIDTH)
            src_cols = slice(j * conv_cols - off, (j + 1) * conv_cols - off)
            prev = carry_ref[slot, :, cols] if c == 0 else src_ref[r0 - q:r0, src_cols]
            window = jnp.concatenate([prev, src_ref[r0:r0 + q, src_cols]], axis=0)
            taps = _dot(shift, window)
            uj = cb_ref[:, cols]
            for k in range(SSD_CONV):
                uj = uj + cw_ref[k:k + 1, cols] * taps[k * q:(k + 1) * q]
            u_parts.append(_silu(uj))
        u = jnp.concatenate(u_parts, axis=1)
        x = u[:, :SSD_WIDTH]
        dt_in = dt_ref[pl.ds(r0, q), :] + dtb_ref[...]
        dt = jnp.maximum(dt_in, 0.0) + jnp.log(1.0 + jnp.exp(-jnp.abs(dt_in)))
        acum = cumsum_rows(dt * a_neg) * LOG2_E
        acum_t = acum.T

        y_parts = []
        for g in range(SSD_GROUPS):
            bg = u[:, SSD_WIDTH + g * SSD_STATE:SSD_WIDTH + (g + 1) * SSD_STATE]
            cg = u[:, SSD_WIDTH + (SSD_GROUPS + g) * SSD_STATE:SSD_WIDTH + (SSD_GROUPS + g + 1) * SSD_STATE]
            cg_bf = cg.astype(BF16)
            cb = _dot_nt(cg_bf, bg.astype(BF16))
            st = state_ref[g]
            y_off = _dot(cg_bf, st.astype(BF16))
            xd_parts, decay_parts = [], []
            for pp in range(pairs_per_group):
                p = g * pairs_per_group + pp
                h0 = HEAD_PAIR * p
                xp = x[:, p * LANES:(p + 1) * LANES]
                cols = [lane_bcast(acum, h0 + e) for e in range(HEAD_PAIR)]
                colc = jnp.where(low, cols[0], cols[1])
                ms = []
                for e in range(HEAD_PAIR):
                    h = h0 + e
                    seg = jnp.exp2(jnp.where(tri, cols[e] - acum_t[h:h + 1, :], NEG_BIG))
                    ms.append(cb * seg)
                m_cat = jnp.concatenate(ms, axis=1).astype(BF16)
                dtc = jnp.where(low, lane_bcast(dt, h0), lane_bcast(dt, h0 + 1))
                xdt = xp * dtc
                x_blk = jnp.concatenate([jnp.where(low, xdt, 0.0), jnp.where(low, 0.0, xdt)],
                                        axis=0).astype(BF16)
                y_diag = _dot(m_cat, x_blk)
                y_o = y_off[:, pp * LANES:(pp + 1) * LANES] * jnp.exp2(colc)
                y_parts.append(y_diag + y_o + dskip_ref[:, p * LANES:(p + 1) * LANES] * xp)
                last = colc[q - 1:q, :]
                xd_parts.append(xdt * jnp.exp2(last - colc))
                decay_parts.append(jnp.exp2(last))
            xd = jnp.concatenate(xd_parts, axis=1).astype(BF16)
            decay = jnp.concatenate(decay_parts, axis=1)
            state_ref[g] = st * decay + _dot(bg.T.astype(BF16), xd)

        z = z_ref[pl.ds(r0, q), :].astype(F32)
        outs = []
        for g in range(SSD_GROUPS):
            yg = jnp.concatenate(y_parts[g * pairs_per_group:(g + 1) * pairs_per_group], axis=1)
            yg = yg * _silu(z[:, g * gw:(g + 1) * gw])
            outs.append(yg * _rms_scale(yg) * g_ref[:, g * gw:(g + 1) * gw])
        y_ref[pl.ds(r0, q), :] = jnp.concatenate(outs, axis=1).astype(y_ref.dtype)

    for c in range(tb // q):
        chunk(c)


def _ssd(main3, dt3, conv_w, conv_b, dt_bias, a_log, d_skip, norm_g, *, tb):
    b, seq, _ = main3.shape
    z_blk = (3 * ATTN_WIDTH) // SSD_WIDTH
    xs_blk = (3 * ATTN_WIDTH + SSD_WIDTH) // SSD_WIDTH
    bc_blk = (3 * ATTN_WIDTH + 2 * SSD_WIDTH) // BC_WIDTH

    def const(shape):
        return pl.BlockSpec(shape, lambda i, j: (0,) * len(shape))

    return pl.pallas_call(
        _ssd_kernel,
        grid=(b, seq // tb),
        in_specs=[
            pl.BlockSpec((None, tb, SSD_WIDTH), lambda i, j: (i, j, z_blk)),
            pl.BlockSpec((None, tb, SSD_WIDTH), lambda i, j: (i, j, xs_blk)),
            pl.BlockSpec((None, tb, BC_WIDTH), lambda i, j: (i, j, bc_blk)),
            pl.BlockSpec((None, tb, LANES), lambda i, j: (i, j, 0)),
            const((SSD_CONV, CONV_CH)), const((1, CONV_CH)),
            const((1, LANES)), const((1, LANES)),
            const((1, SSD_WIDTH)), const((1, SSD_WIDTH)),
        ],
        out_specs=pl.BlockSpec((None, tb, SSD_WIDTH), lambda i, j: (i, j, 0)),
        out_shape=jax.ShapeDtypeStruct((b, seq, SSD_WIDTH), BF16),
        scratch_shapes=[
            pltpu.VMEM((2, SSD_CHUNK, CONV_CH), BF16),
            pltpu.VMEM((SSD_GROUPS, SSD_STATE, SSD_WIDTH // SSD_GROUPS), F32),
        ],
        compiler_params=_params(("parallel", "arbitrary")),
        name="ssd",
    )(main3, main3, main3, dt3, conv_w, conv_b, dt_bias, a_log, d_skip, norm_g)


def _out_proj_kernel(a_ref, y_ref, x_ref, g_ref, w_ref, g2_ref, o_ref, hn_ref):
    a = a_ref[...].astype(F32)
    an = (a * _rms_scale(a) * g_ref[...]).astype(BF16)
    mixed = _dot(an, w_ref[0:ATTN_WIDTH, :]) + _dot(y_ref[...], w_ref[ATTN_WIDTH:, :])
    x1 = x_ref[...] + mixed
    o_ref[...] = x1
    hn_ref[...] = (x1 * _rms_scale(x1) * g2_ref[...]).astype(BF16)


def _out_proj(attn, y, x, g, w_all, layer, g2, *, tm):
    m, d = x.shape
    return pl.pallas_call(
        _out_proj_kernel,
        grid=(m // tm,),
        in_specs=[
            pl.BlockSpec((tm, ATTN_WIDTH), lambda i: (i, 0)),
            pl.BlockSpec((tm, SSD_WIDTH), lambda i: (i, 0)),
            pl.BlockSpec((tm, d), lambda i: (i, 0)),
            _resident((1, ATTN_WIDTH)),
            _resident(w_all.shape[1:], layer),
            _resident((1, d)),
        ],
        out_specs=[pl.BlockSpec((tm, d), lambda i: (i, 0)), pl.BlockSpec((tm, d), lambda i: (i, 0))],
        out_shape=[jax.ShapeDtypeStruct((m, d), F32), jax.ShapeDtypeStruct((m, d), BF16)],
        compiler_params=_params(("parallel",)),
        name="out_proj",
    )(attn, y, x, g, w_all, g2)


def _mlp_kernel(x_ref, hn_ref, w1_ref, w2_ref, gf_ref, o_ref, *, final_norm):
    f = pl.program_id(1)

    @pl.when(f == 0)
    def _():
        o_ref[...] = x_ref[...]

    h = jnp.maximum(_dot(hn_ref[...], w1_ref[...]), 0.0)
    o_ref[...] += _dot((h * h).astype(BF16), w2_ref[...])

    if final_norm:
        @pl.when(f == pl.num_programs(1) - 1)
        def _():
            o = o_ref[...]
            o_ref[...] = o * _rms_scale(o) * gf_ref[...]


def _mlp(x, hn, w1, w2, gf, *, tm, tf, final_norm):
    m, d = x.shape
    dff = w1.shape[1]
    return pl.pallas_call(
        functools.partial(_mlp_kernel, final_norm=final_norm),
        grid=(m // tm, dff // tf),
        in_specs=[
            pl.BlockSpec((tm, d), lambda i, j: (i, 0)),
            pl.BlockSpec((tm, d), lambda i, j: (i, 0)),
            pl.BlockSpec((d, tf), lambda i, j: (0, j)),
            pl.BlockSpec((tf, d), lambda i, j: (j, 0)),
            pl.BlockSpec((1, d), lambda i, j: (0, 0)),
        ],
        out_specs=pl.BlockSpec((tm, d), lambda i, j: (i, 0)),
        out_shape=jax.ShapeDtypeStruct((m, d), F32),
        compiler_params=_params(("parallel", "arbitrary")),
        name="mlp",
    )(x, hn, w1, w2, gf)


def _alibi_slopes():
    return jnp.asarray(2.0 ** (-8.0 * (np.arange(ATTN_HEADS) + 1) / ATTN_HEADS), dtype=F32)


def _pad_lanes(v):
    return jnp.pad(v, ((0, 0), (0, LANES - v.shape[1])))


def kernel(x, ln1_g, w_in, conv_w, conv_b, dt_bias, a_log, d_skip, attn_norm_g, ssd_norm_g, w_out,
           ln2_g, w_mlp_in, w_mlp_out, final_norm_g):
    b, seq, d = x.shape
    m = b * seq
    slopes = _alibi_slopes()
    xf = x.reshape(m, d)
    w_in_bf = w_in.astype(BF16)
    w_dt_bf = jnp.pad(w_in[:, :, MAIN_WIDTH:], ((0, 0), (0, 0), (0, LANES - (w_in.shape[2] - MAIN_WIDTH)))).astype(BF16)
    w_out_bf = w_out.astype(BF16)
    for l in range(DEPTH):
        main, dt_raw = _in_proj(xf, ln1_g[l][None, :], w_in_bf, w_dt_bf, l,
                                tm=IN_PROJ_ROWS, tn=IN_PROJ_COLS)
        main3 = main.reshape(b, seq, MAIN_WIDTH)
        attn, w1_bf, w2_bf = _attention(slopes, main3, w_mlp_in, w_mlp_out, l)
        y = _ssd(
            main3, dt_raw.reshape(b, seq, LANES), conv_w[l], conv_b[l][None, :],
            _pad_lanes(dt_bias[l][None, :]), _pad_lanes(a_log[l][None, :]),
            jnp.repeat(d_skip[l], SSD_HEAD_DIM)[None, :], ssd_norm_g[l][None, :], tb=SSD_TOKENS)
        x1, hn1 = _out_proj(attn.reshape(m, ATTN_WIDTH), y.reshape(m, SSD_WIDTH), xf,
                            attn_norm_g[l][None, :], w_out_bf, l, ln2_g[l][None, :], tm=OUT_PROJ_ROWS)
        xf = _mlp(x1, hn1, w1_bf, w2_bf,
                  final_norm_g[None, :], tm=MLP_ROWS, tf=MLP_HIDDEN, final_norm=(l == DEPTH - 1))
    return xf.reshape(b, seq, d)
```

```python
import functools

import jax
import jax.numpy as jnp
import numpy as np
from jax import lax
from jax.experimental import pallas as pl
from jax.experimental.pallas import tpu as pltpu

D_MODEL = 2048
DEPTH = 2
HEAD_DIM = 64
ATTN_WIDTH = D_MODEL // 2
ATTN_HEADS = ATTN_WIDTH // HEAD_DIM
DILATED_BRANCHES = ((128, 1), (512, 4), (2048, 16))
ATTN_BLOCK = 128
SSD_WIDTH = D_MODEL // 2
SSD_HEAD_DIM = 64
SSD_HEADS = SSD_WIDTH // SSD_HEAD_DIM
SSD_GROUPS = 2
SSD_STATE = 128
SSD_CONV = 4
SSD_CHUNK = 128
BC_WIDTH = 2 * SSD_GROUPS * SSD_STATE
CONV_CH = SSD_WIDTH + BC_WIDTH
MAIN_WIDTH = 3 * ATTN_WIDTH + SSD_WIDTH + CONV_CH
D_FF = 4 * D_MODEL
NORM_EPS = 1e-5

LANES = 128
HEAD_PAIR = LANES // HEAD_DIM
NEG_BIG = -1e30
LOG2_E = 1.4426950408889634
Q_PRESCALE = HEAD_DIM ** -0.5 * LOG2_E
VMEM_LIMIT = 56 * 1024 * 1024

IN_PROJ_ROWS, IN_PROJ_COLS = 512, 512
OUT_PROJ_ROWS = 512
MLP_ROWS, MLP_HIDDEN = 1024, 512
SSD_TOKENS = 1024

F32 = jnp.float32
BF16 = jnp.bfloat16


def _dot(a, b):
    return jnp.dot(a, b, preferred_element_type=F32)


def _dot_nt(a, b):
    return lax.dot_general(a, b, (((1,), (1,)), ((), ())), preferred_element_type=F32)


def _rms_scale(x):
    return lax.rsqrt(jnp.mean(x * x, axis=-1, keepdims=True) + NORM_EPS)


def _silu(x):
    return x * (1.0 / (1.0 + jnp.exp2(x * (-LOG2_E))))


def _params(semantics):
    return pltpu.CompilerParams(dimension_semantics=semantics, vmem_limit_bytes=VMEM_LIMIT)


def _in_proj_kernel(x_ref, g_ref, w_ref, wdt_ref, cw_ref, cb_ref, main_ref, dt_ref,
                    hn_ref, ext_ref, tail_ref, *, tn, tiles_per_seq):
    tm = x_ref.shape[0]
    halo = 8
    conv0 = MAIN_WIDTH - CONV_CH
    assert conv0 % tn == 0 and CONV_CH % tn == 0

    @pl.when(pl.program_id(0) % tiles_per_seq == 0)
    def _():
        tail_ref[...] = jnp.zeros(tail_ref.shape, F32)

    x = x_ref[...]
    hn_ref[...] = (x * _rms_scale(x) * g_ref[...]).astype(BF16)
    dt_ref[...] = _dot(hn_ref[...], wdt_ref[...])
    for j in range(main_ref.shape[1] // tn):
        cols = slice(j * tn, (j + 1) * tn)
        acc = _dot(hn_ref[...], w_ref[:, cols])
        if (j + 1) * tn <= ATTN_WIDTH:
            acc = acc * Q_PRESCALE
        if j * tn >= conv0:
            c = (j * tn - conv0) // tn
            ccols = slice(j * tn - conv0, (j + 1) * tn - conv0)
            ext_ref[0:halo, :] = tail_ref[c]
            ext_ref[halo:halo + tm, :] = acc
            tail_ref[c] = acc[tm - halo:tm]
            u = cb_ref[:, ccols]
            for k in range(SSD_CONV):
                u = u + cw_ref[k:k + 1, ccols] * ext_ref[pl.ds(halo - (SSD_CONV - 1) + k, tm), :]
            acc = _silu(u)
        main_ref[:, cols] = acc.astype(main_ref.dtype)


def _resident(shape, layer=None):
    if layer is None:
        return pl.BlockSpec(shape, lambda i: (0,) * len(shape), pipeline_mode=pl.Buffered(1))
    return pl.BlockSpec((None,) + tuple(shape), lambda i: (layer,) + (0,) * len(shape),
                        pipeline_mode=pl.Buffered(1))


def _in_proj(x, g, w_all, w_dt, layer, conv_w, conv_b, *, seq, tm, tn):
    m, d = x.shape
    n = MAIN_WIDTH
    assert m % tm == 0 and n % tn == 0 and seq % tm == 0
    return pl.pallas_call(
        functools.partial(_in_proj_kernel, tn=tn, tiles_per_seq=seq // tm),
        grid=(m // tm,),
        in_specs=[
            pl.BlockSpec((tm, d), lambda i: (i, 0)),
            _resident((1, d)),
            _resident(w_all.shape[1:], layer),
            _resident((d, LANES), layer),
            _resident((SSD_CONV, CONV_CH)),
            _resident((1, CONV_CH)),
        ],
        out_specs=[
            pl.BlockSpec((tm, n), lambda i: (i, 0)),
            pl.BlockSpec((tm, LANES), lambda i: (i, 0)),
        ],
        out_shape=[
            jax.ShapeDtypeStruct((m, n), BF16),
            jax.ShapeDtypeStruct((m, LANES), F32),
        ],
        scratch_shapes=[
            pltpu.VMEM((tm, d), BF16),
            pltpu.VMEM((tm + 8, tn), F32),
            pltpu.VMEM((CONV_CH // tn, 8, tn), F32),
        ],
        compiler_params=_params(("arbitrary",)),
        name="in_proj",
    )(x, g, w_all, w_dt, conv_w, conv_b)


def _attn_kernel(slopes_ref, q_ref, k_ref, v_ref, w1_ref, w2_ref, o_ref, w1o_ref, w2o_ref,
                 qf_ref, kf_ref, vf_ref, qm_ref, km_ref, vm_ref, of_ref,
                 qd_ref, kd_ref, vd_ref, qh_ref, kh_ref, vh_ref, st0_ref, st1_ref, bias_ref):
    seq = q_ref.shape[0]
    blk = ATTN_BLOCK
    hp = pl.program_id(1)

    w1o_ref[...] = w1_ref[...].astype(BF16)
    w2o_ref[...] = w2_ref[...].astype(BF16)

    qf_ref[...] = q_ref[...].astype(F32)
    kf_ref[...] = k_ref[...].astype(F32)
    vf_ref[...] = v_ref[...].astype(F32)

    low = lax.broadcasted_iota(jnp.int32, (blk, LANES), 1) < HEAD_DIM
    ones_bf = jnp.ones((2 * blk, LANES), BF16)

    row = lax.broadcasted_iota(jnp.int32, (2 * blk, 2 * blk), 0)
    col = lax.broadcasted_iota(jnp.int32, (2 * blk, 2 * blk), 1)
    delta = (row & (blk - 1)) - col + blk
    slope = jnp.where(row < blk, slopes_ref[HEAD_PAIR * hp], slopes_ref[HEAD_PAIR * hp + 1])
    for t, (window, dil) in enumerate(DILATED_BRANCHES):
        valid = (delta >= 0) & (delta <= window // dil)
        bias = jnp.where(valid, -slope * (delta * dil).astype(F32) * LOG2_E, NEG_BIG)
        bias_ref[2 * t] = bias
        bias_ref[2 * t + 1] = jnp.where(col >= blk, bias, NEG_BIG)

    def pair_rows(x):
        return jnp.where(low, jnp.broadcast_to(x[:blk], (blk, LANES)),
                         jnp.broadcast_to(x[blk:], (blk, LANES)))

    def pair_halves(x):
        return jnp.where(low, x[:blk], x[blk:])

    zero_bf = jnp.zeros((), BF16)
    n_blocks = seq // blk

    def block_rows(i):
        start = i * blk
        return pl.ds(start if isinstance(start, int) else pl.multiple_of(start, blk), blk)

    (_, d_lo), (_, d_mid), (_, d_hi) = DILATED_BRANCHES
    ratio = d_hi // d_mid
    assert d_lo == 1 and d_hi == d_mid * ratio and seq % (blk * d_hi) == 0
    mid_len = seq // d_mid

    def log2(n):
        assert n & (n - 1) == 0
        return n.bit_length() - 1

    def split_block(idx, dil):
        nb = n_blocks // dil
        return idx >> log2(nb), idx & (nb - 1)

    def token_rows(idx, dil):
        r, n = split_block(idx, dil)
        return pl.ds(r + n * (blk * dil), blk, stride=dil) if dil > 1 else block_rows(idx)

    def hi_rows_in_mid(idx):
        r, n = split_block(idx, d_hi)
        return pl.ds((r & (d_mid - 1)) * mid_len + n * (blk * ratio) + (r >> log2(d_mid)), blk, stride=ratio)

    def gather_mid(idx):
        rows = token_rows(idx, d_mid)
        for f_ref, mid_ref, d_ref in ((qf_ref, qm_ref, qd_ref), (kf_ref, km_ref, kd_ref), (vf_ref, vm_ref, vd_ref)):
            x = f_ref[rows, :]
            mid_ref[block_rows(idx), :] = x
            d_ref[block_rows(idx), :] = x.astype(BF16)

    def gather_hi(idx):
        rows = hi_rows_in_mid(idx)
        for mid_ref, h_ref in ((qm_ref, qh_ref), (km_ref, kh_ref), (vm_ref, vh_ref)):
            h_ref[block_rows(idx), :] = mid_ref[rows, :].astype(BF16)

    def block_softmax(t, idx, q_src, k_src, v_src):
        first = split_block(idx, DILATED_BRANCHES[t][1])[1] == 0
        cur = block_rows(idx)
        prev = block_rows(idx if first else idx - 1)
        q = q_src[cur, :]
        q2 = jnp.concatenate([jnp.where(low, q, zero_bf), jnp.where(low, zero_bf, q)], axis=0)
        kk = jnp.concatenate([k_src[prev, :], k_src[cur, :]], axis=0)
        vv = jnp.concatenate([jnp.concatenate([v_src[prev, :], v_src[cur, :]], axis=0), ones_bf], axis=1)
        s = _dot_nt(q2, kk) + bias_ref[2 * t + int(first)]
        m = jnp.max(s, axis=-1, keepdims=True)
        p = jnp.exp2(s - m)
        pv = _dot(p.astype(BF16), vv)
        return pair_rows(m), pair_halves(pv[:, LANES:]), pair_halves(pv[:, :LANES])

    def combine(a, b):
        (ma, la, acca), (mb, lb, accb) = a, b
        m = jnp.maximum(ma, mb)
        wa, wb = jnp.exp2(ma - m), jnp.exp2(mb - m)
        return m, wa * la + wb * lb, wa * acca + wb * accb

    def load_state(st_ref, rows):
        return tuple(st_ref[i, rows, :] for i in range(3))

    def store_state(st_ref, rows, state):
        for i, x in enumerate(state):
            st_ref[i, rows, :] = x

    def branch0(idx):
        store_state(st0_ref, block_rows(idx), block_softmax(0, idx, q_ref, k_ref, v_ref))
        gather_mid(idx)

    def branch1(idx):
        mine = block_softmax(1, idx, qd_ref, kd_ref, vd_ref)
        store_state(st1_ref, block_rows(idx), combine(load_state(st0_ref, token_rows(idx, d_mid)), mine))
        gather_hi(idx)

    def branch2(idx):
        mine = block_softmax(2, idx, qh_ref, kh_ref, vh_ref)
        _, l_all, acc_all = combine(load_state(st1_ref, hi_rows_in_mid(idx)), mine)
        of_ref[token_rows(idx, d_hi), :] = acc_all / l_all

    for branch in (branch0, branch1, branch2):
        for idx in range(n_blocks):
            branch(idx)
    o_ref[...] = of_ref[...].astype(o_ref.dtype)


def _attention(slopes, main3, w1_all, w2_all, layer):
    b, seq, _ = main3.shape
    n_pairs = ATTN_WIDTH // LANES
    blk = ATTN_BLOCK
    n_steps = b * n_pairs
    w1_shape, w2_shape = w1_all.shape[1:], w2_all.shape[1:]
    r1, r2 = w1_shape[0] // n_steps, w2_shape[0] // n_steps
    assert r1 * n_steps == w1_shape[0] and r2 * n_steps == w2_shape[0]

    def spec(offset):
        return pl.BlockSpec((None, seq, LANES), lambda i, j: (i, 0, offset + j))

    return pl.pallas_call(
        _attn_kernel,
        grid=(b, n_pairs),
        in_specs=[
            pl.BlockSpec(memory_space=pltpu.SMEM),
            spec(0), spec(n_pairs), spec(2 * n_pairs),
            pl.BlockSpec((None, r1, w1_shape[1]), lambda i, j: (layer, i * n_pairs + j, 0)),
            pl.BlockSpec((None, r2, w2_shape[1]), lambda i, j: (layer, i * n_pairs + j, 0)),
        ],
        out_specs=[
            pl.BlockSpec((None, seq, LANES), lambda i, j: (i, 0, j)),
            pl.BlockSpec((r1, w1_shape[1]), lambda i, j: (i * n_pairs + j, 0)),
            pl.BlockSpec((r2, w2_shape[1]), lambda i, j: (i * n_pairs + j, 0)),
        ],
        out_shape=[
            jax.ShapeDtypeStruct((b, seq, ATTN_WIDTH), BF16),
            jax.ShapeDtypeStruct(w1_shape, BF16),
            jax.ShapeDtypeStruct(w2_shape, BF16),
        ],
        scratch_shapes=[pltpu.VMEM((seq, LANES), F32)] * 7
        + [pltpu.VMEM((seq, LANES), BF16)] * 6
        + [pltpu.VMEM((3, seq, LANES), F32)] * 2
        + [pltpu.VMEM((2 * len(DILATED_BRANCHES), 2 * blk, 2 * blk), F32)],
        compiler_params=_params(("parallel", "parallel")),
        name="attention",
    )(slopes, main3, main3, main3, w1_all, w2_all)


def _ssd_kernel(z_ref, xs_ref, bc_ref, dt_ref, dtb_ref, alog_ref, dskip_ref, g_ref,
                y_ref, state_ref):
    tb = xs_ref.shape[0]
    q = SSD_CHUNK
    gw = SSD_WIDTH // SSD_GROUPS
    pairs_per_group = gw // LANES

    @pl.when(pl.program_id(1) == 0)
    def _():
        state_ref[...] = jnp.zeros(state_ref.shape, F32)

    low = lax.broadcasted_iota(jnp.int32, (q, LANES), 1) < SSD_HEAD_DIM
    ri = lax.broadcasted_iota(jnp.int32, (q, q), 0)
    ci = lax.broadcasted_iota(jnp.int32, (q, q), 1)
    tri = ri >= ci
    tri_bf = jnp.where(tri, 1.0, 0.0).astype(BF16)
    a_neg = -jnp.exp(alog_ref[...])

    def cumsum_rows(v):
        hi = v.astype(BF16)
        r1 = v - hi.astype(F32)
        mid = r1.astype(BF16)
        lo = (r1 - mid.astype(F32)).astype(BF16)
        return _dot(tri_bf, hi) + _dot(tri_bf, mid) + _dot(tri_bf, lo)

    def lane_bcast(v, h):
        return jnp.broadcast_to(v[:, h:h + 1], (q, LANES))

    def chunk(c):
        r0 = c * q
        u = jnp.concatenate([xs_ref[r0:r0 + q, :], bc_ref[r0:r0 + q, :]], axis=1).astype(F32)
        x = u[:, :SSD_WIDTH]
        dt_in = dt_ref[pl.ds(r0, q), :] + dtb_ref[...]
        dt = jnp.maximum(dt_in, 0.0) + jnp.log(1.0 + jnp.exp(-jnp.abs(dt_in)))
        acum = cumsum_rows(dt * a_neg) * LOG2_E
        acum_t = acum.T

        y_parts = []
        for g in range(SSD_GROUPS):
            bg = u[:, SSD_WIDTH + g * SSD_STATE:SSD_WIDTH + (g + 1) * SSD_STATE]
            cg = u[:, SSD_WIDTH + (SSD_GROUPS + g) * SSD_STATE:SSD_WIDTH + (SSD_GROUPS + g + 1) * SSD_STATE]
            cg_bf = cg.astype(BF16)
            cb = _dot_nt(cg_bf, bg.astype(BF16))
            st = state_ref[g]
            y_off = _dot(cg_bf, st.astype(BF16))
            xd_parts, decay_parts = [], []
            for pp in range(pairs_per_group):
                p = g * pairs_per_group + pp
                h0 = HEAD_PAIR * p
                xp = x[:, p * LANES:(p + 1) * LANES]
                cols = [lane_bcast(acum, h0 + e) for e in range(HEAD_PAIR)]
                colc = jnp.where(low, cols[0], cols[1])
                ms = []
                for e in range(HEAD_PAIR):
                    h = h0 + e
                    seg = jnp.exp2(jnp.where(tri, cols[e] - acum_t[h:h + 1, :], NEG_BIG))
                    ms.append(cb * seg)
                m_cat = jnp.concatenate(ms, axis=1).astype(BF16)
                dtc = jnp.where(low, lane_bcast(dt, h0), lane_bcast(dt, h0 + 1))
                xdt = xp * dtc
                x_blk = jnp.concatenate([jnp.where(low, xdt, 0.0), jnp.where(low, 0.0, xdt)],
                                        axis=0).astype(BF16)
                y_diag = _dot(m_cat, x_blk)
                y_o = y_off[:, pp * LANES:(pp + 1) * LANES] * jnp.exp2(colc)
                y_parts.append(y_diag + y_o + dskip_ref[:, p * LANES:(p + 1) * LANES] * xp)
                last = colc[q - 1:q, :]
                xd_parts.append(xdt * jnp.exp2(last - colc))
                decay_parts.append(jnp.exp2(last))
            xd = jnp.concatenate(xd_parts, axis=1).astype(BF16)
            decay = jnp.concatenate(decay_parts, axis=1)
            state_ref[g] = st * decay + _dot(bg.T.astype(BF16), xd)

        z = z_ref[pl.ds(r0, q), :].astype(F32)
        outs = []
        for g in range(SSD_GROUPS):
            yg = jnp.concatenate(y_parts[g * pairs_per_group:(g + 1) * pairs_per_group], axis=1)
            yg = yg * _silu(z[:, g * gw:(g + 1) * gw])
            outs.append(yg * _rms_scale(yg) * g_ref[:, g * gw:(g + 1) * gw])
        y_ref[pl.ds(r0, q), :] = jnp.concatenate(outs, axis=1).astype(y_ref.dtype)

    for c in range(tb // q):
        chunk(c)


def _ssd(main3, dt3, dt_bias, a_log, d_skip, norm_g, *, tb):
    b, seq, _ = main3.shape
    z_blk = (3 * ATTN_WIDTH) // SSD_WIDTH
    xs_blk = (3 * ATTN_WIDTH + SSD_WIDTH) // SSD_WIDTH
    bc_blk = (3 * ATTN_WIDTH + 2 * SSD_WIDTH) // BC_WIDTH

    def const(shape):
        return pl.BlockSpec(shape, lambda i, j: (0,) * len(shape))

    return pl.pallas_call(
        _ssd_kernel,
        grid=(b, seq // tb),
        in_specs=[
            pl.BlockSpec((None, tb, SSD_WIDTH), lambda i, j: (i, j, z_blk)),
            pl.BlockSpec((None, tb, SSD_WIDTH), lambda i, j: (i, j, xs_blk)),
            pl.BlockSpec((None, tb, BC_WIDTH), lambda i, j: (i, j, bc_blk)),
            pl.BlockSpec((None, tb, LANES), lambda i, j: (i, j, 0)),
            const((1, LANES)), const((1, LANES)),
            const((1, SSD_WIDTH)), const((1, SSD_WIDTH)),
        ],
        out_specs=pl.BlockSpec((None, tb, SSD_WIDTH), lambda i, j: (i, j, 0)),
        out_shape=jax.ShapeDtypeStruct((b, seq, SSD_WIDTH), BF16),
        scratch_shapes=[
            pltpu.VMEM((SSD_GROUPS, SSD_STATE, SSD_WIDTH // SSD_GROUPS), F32),
        ],
        compiler_params=_params(("parallel", "arbitrary")),
        name="ssd",
    )(main3, main3, main3, dt3, dt_bias, a_log, d_skip, norm_g)


def _out_proj_kernel(a_ref, y_ref, x_ref, g_ref, w_ref, o_ref):
    a = a_ref[...].astype(F32)
    an = (a * _rms_scale(a) * g_ref[...]).astype(BF16)
    mixed = _dot(an, w_ref[0:ATTN_WIDTH, :]) + _dot(y_ref[...], w_ref[ATTN_WIDTH:, :])
    o_ref[...] = x_ref[...] + mixed


def _out_proj(attn, y, x, g, w_all, layer, *, tm):
    m, d = x.shape
    return pl.pallas_call(
        _out_proj_kernel,
        grid=(m // tm,),
        in_specs=[
            pl.BlockSpec((tm, ATTN_WIDTH), lambda i: (i, 0)),
            pl.BlockSpec((tm, SSD_WIDTH), lambda i: (i, 0)),
            pl.BlockSpec((tm, d), lambda i: (i, 0)),
            _resident((1, ATTN_WIDTH)),
            _resident(w_all.shape[1:], layer),
        ],
        out_specs=pl.BlockSpec((tm, d), lambda i: (i, 0)),
        out_shape=jax.ShapeDtypeStruct((m, d), F32),
        compiler_params=_params(("parallel",)),
        name="out_proj",
    )(attn, y, x, g, w_all)


def _mlp_kernel(x_ref, g_ref, w1_ref, w2_ref, gf_ref, o_ref, hn_ref, *, final_norm):
    f = pl.program_id(1)

    @pl.when(f == 0)
    def _():
        x = x_ref[...]
        hn_ref[...] = (x * _rms_scale(x) * g_ref[...]).astype(BF16)
        o_ref[...] = x

    h = jnp.maximum(_dot(hn_ref[...], w1_ref[...]), 0.0)
    o_ref[...] += _dot((h * h).astype(BF16), w2_ref[...])

    if final_norm:
        @pl.when(f == pl.num_programs(1) - 1)
        def _():
            o = o_ref[...]
            o_ref[...] = o * _rms_scale(o) * gf_ref[...]


def _mlp(x, g, w1, w2, gf, *, tm, tf, final_norm):
    m, d = x.shape
    dff = w1.shape[1]
    return pl.pallas_call(
        functools.partial(_mlp_kernel, final_norm=final_norm),
        grid=(m // tm, dff // tf),
        in_specs=[
            pl.BlockSpec((tm, d), lambda i, j: (i, 0)),
            pl.BlockSpec((1, d), lambda i, j: (0, 0)),
            pl.BlockSpec((d, tf), lambda i, j: (0, j)),
            pl.BlockSpec((tf, d), lambda i, j: (j, 0)),
            pl.BlockSpec((1, d), lambda i, j: (0, 0)),
        ],
        out_specs=pl.BlockSpec((tm, d), lambda i, j: (i, 0)),
        out_shape=jax.ShapeDtypeStruct((m, d), F32),
        scratch_shapes=[pltpu.VMEM((tm, d), BF16)],
        compiler_params=_params(("parallel", "arbitrary")),
        name="mlp",
    )(x, g, w1, w2, gf)


def _alibi_slopes():
    return jnp.asarray(2.0 ** (-8.0 * (np.arange(ATTN_HEADS) + 1) / ATTN_HEADS), dtype=F32)


def _pad_lanes(v):
    return jnp.pad(v, ((0, 0), (0, LANES - v.shape[1])))


def kernel(x, ln1_g, w_in, conv_w, conv_b, dt_bias, a_log, d_skip, attn_norm_g, ssd_norm_g, w_out,
           ln2_g, w_mlp_in, w_mlp_out, final_norm_g):
    b, seq, d = x.shape
    m = b * seq
    slopes = _alibi_slopes()
    xf = x.reshape(m, d)
    w_in_bf = w_in.astype(BF16)
    w_dt_bf = jnp.pad(w_in[:, :, MAIN_WIDTH:], ((0, 0), (0, 0), (0, LANES - (w_in.shape[2] - MAIN_WIDTH)))).astype(BF16)
    w_out_bf = w_out.astype(BF16)
    for l in range(DEPTH):
        main, dt_raw = _in_proj(xf, ln1_g[l][None, :], w_in_bf, w_dt_bf, l, conv_w[l], conv_b[l][None, :],
                                seq=seq, tm=IN_PROJ_ROWS, tn=IN_PROJ_COLS)
        main3 = main.reshape(b, seq, MAIN_WIDTH)
        attn, w1_bf, w2_bf = _attention(slopes, main3, w_mlp_in, w_mlp_out, l)
        y = _ssd(
            main3, dt_raw.reshape(b, seq, LANES),
            _pad_lanes(dt_bias[l][None, :]), _pad_lanes(a_log[l][None, :]),
            jnp.repeat(d_skip[l], SSD_HEAD_DIM)[None, :], ssd_norm_g[l][None, :], tb=SSD_TOKENS)
        x1 = _out_proj(attn.reshape(m, ATTN_WIDTH), y.reshape(m, SSD_WIDTH), xf,
                       attn_norm_g[l][None, :], w_out_bf, l, tm=OUT_PROJ_ROWS)
        xf = _mlp(x1, ln2_g[l][None, :], w1_bf, w2_bf,
                  final_norm_g[None, :], tm=MLP_ROWS, tf=MLP_HIDDEN, final_norm=(l == DEPTH - 1))
    return xf.reshape(b, seq, d)
```

```python
import functools

import jax
import jax.numpy as jnp
import numpy as np
from jax import lax
from jax.experimental import pallas as pl
from jax.experimental.pallas import tpu as pltpu

D_MODEL = 2048
DEPTH = 2
HEAD_DIM = 64
ATTN_WIDTH = D_MODEL // 2
ATTN_HEADS = ATTN_WIDTH // HEAD_DIM
DILATED_BRANCHES = ((128, 1), (512, 4), (2048, 16))
ATTN_BLOCK = 128
SSD_WIDTH = D_MODEL // 2
SSD_HEAD_DIM = 64
SSD_HEADS = SSD_WIDTH // SSD_HEAD_DIM
SSD_GROUPS = 2
SSD_STATE = 128
SSD_CONV = 4
SSD_CHUNK = 128
BC_WIDTH = 2 * SSD_GROUPS * SSD_STATE
CONV_CH = SSD_WIDTH + BC_WIDTH
MAIN_WIDTH = 3 * ATTN_WIDTH + SSD_WIDTH + CONV_CH
D_FF = 4 * D_MODEL
NORM_EPS = 1e-5

LANES = 128
HEAD_PAIR = LANES // HEAD_DIM
NEG_BIG = -1e30
LOG2_E = 1.4426950408889634
Q_PRESCALE = HEAD_DIM ** -0.5 * LOG2_E
VMEM_LIMIT = 56 * 1024 * 1024

IN_PROJ_ROWS, IN_PROJ_COLS = 512, 512
OUT_PROJ_ROWS = 512
MLP_ROWS, MLP_HIDDEN = 1024, 512
SSD_TOKENS = 1024

F32 = jnp.float32
BF16 = jnp.bfloat16


def _dot(a, b):
    return jnp.dot(a, b, preferred_element_type=F32)


def _dot_nt(a, b):
    return lax.dot_general(a, b, (((1,), (1,)), ((), ())), preferred_element_type=F32)


def _rms_scale(x):
    return lax.rsqrt(jnp.mean(x * x, axis=-1, keepdims=True) + NORM_EPS)


def _silu(x):
    return x * (1.0 / (1.0 + jnp.exp2(x * (-LOG2_E))))


def _params(semantics):
    return pltpu.CompilerParams(dimension_semantics=semantics, vmem_limit_bytes=VMEM_LIMIT)


def _in_proj_kernel(x_ref, g_ref, w_ref, wdt_ref, cw_ref, cb_ref, main_ref, dt_ref,
                    hn_ref, ext_ref, tail_ref, *, tn, tiles_per_seq):
    tm = x_ref.shape[0]
    halo = 8
    conv0 = MAIN_WIDTH - CONV_CH
    assert conv0 % tn == 0 and CONV_CH % tn == 0

    @pl.when(pl.program_id(0) % tiles_per_seq == 0)
    def _():
        tail_ref[...] = jnp.zeros(tail_ref.shape, F32)

    x = x_ref[...]
    hn_ref[...] = (x * _rms_scale(x) * g_ref[...]).astype(BF16)
    dt_ref[...] = _dot(hn_ref[...], wdt_ref[...])
    n_chunks = main_ref.shape[1] // tn
    for j in list(range(conv0 // tn, n_chunks)) + list(range(conv0 // tn)):
        cols = slice(j * tn, (j + 1) * tn)
        acc = _dot(hn_ref[...], w_ref[:, cols])
        if (j + 1) * tn <= ATTN_WIDTH:
            acc = acc * Q_PRESCALE
        if j * tn >= conv0:
            c = (j * tn - conv0) // tn
            ccols = slice(j * tn - conv0, (j + 1) * tn - conv0)
            ext_ref[0:halo, :] = tail_ref[c]
            ext_ref[halo:halo + tm, :] = acc
            tail_ref[c] = acc[tm - halo:tm]
            u = cb_ref[:, ccols]
            for k in range(SSD_CONV):
                u = u + cw_ref[k:k + 1, ccols] * ext_ref[pl.ds(halo - (SSD_CONV - 1) + k, tm), :]
            acc = _silu(u)
        main_ref[:, cols] = acc.astype(main_ref.dtype)


def _resident(shape, layer=None):
    if layer is None:
        return pl.BlockSpec(shape, lambda i: (0,) * len(shape), pipeline_mode=pl.Buffered(1))
    return pl.BlockSpec((None,) + tuple(shape), lambda i: (layer,) + (0,) * len(shape),
                        pipeline_mode=pl.Buffered(1))


def _in_proj(x, g, w_all, w_dt, layer, conv_w, conv_b, *, seq, tm, tn):
    m, d = x.shape
    n = MAIN_WIDTH
    assert m % tm == 0 and n % tn == 0 and seq % tm == 0
    return pl.pallas_call(
        functools.partial(_in_proj_kernel, tn=tn, tiles_per_seq=seq // tm),
        grid=(m // tm,),
        in_specs=[
            pl.BlockSpec((tm, d), lambda i: (i, 0)),
            _resident((1, d)),
            _resident(w_all.shape[1:], layer),
            _resident((d, LANES), layer),
            _resident((SSD_CONV, CONV_CH)),
            _resident((1, CONV_CH)),
        ],
        out_specs=[
            pl.BlockSpec((tm, n), lambda i: (i, 0)),
            pl.BlockSpec((tm, LANES), lambda i: (i, 0)),
        ],
        out_shape=[
            jax.ShapeDtypeStruct((m, n), BF16),
            jax.ShapeDtypeStruct((m, LANES), F32),
        ],
        scratch_shapes=[
            pltpu.VMEM((tm, d), BF16),
            pltpu.VMEM((tm + 8, tn), F32),
            pltpu.VMEM((CONV_CH // tn, 8, tn), F32),
        ],
        compiler_params=_params(("arbitrary",)),
        name="in_proj",
    )(x, g, w_all, w_dt, conv_w, conv_b)


def _attn_kernel(slopes_ref, q_ref, k_ref, v_ref, w1_ref, w2_ref, o_ref, w1o_ref, w2o_ref,
                 qf_ref, kf_ref, vf_ref, qm_ref, km_ref, vm_ref, of_ref,
                 qd_ref, kd_ref, vd_ref, qh_ref, kh_ref, vh_ref, st0_ref, st1_ref, bias_ref):
    seq = q_ref.shape[0]
    blk = ATTN_BLOCK
    hp = pl.program_id(1)

    w1o_ref[...] = w1_ref[...].astype(BF16)
    w2o_ref[...] = w2_ref[...].astype(BF16)

    qf_ref[...] = q_ref[...].astype(F32)
    kf_ref[...] = k_ref[...].astype(F32)
    vf_ref[...] = v_ref[...].astype(F32)

    low = lax.broadcasted_iota(jnp.int32, (blk, LANES), 1) < HEAD_DIM
    ones_bf = jnp.ones((2 * blk, LANES), BF16)

    row = lax.broadcasted_iota(jnp.int32, (2 * blk, 2 * blk), 0)
    col = lax.broadcasted_iota(jnp.int32, (2 * blk, 2 * blk), 1)
    delta = (row & (blk - 1)) - col + blk
    slope = jnp.where(row < blk, slopes_ref[HEAD_PAIR * hp], slopes_ref[HEAD_PAIR * hp + 1])
    for t, (window, dil) in enumerate(DILATED_BRANCHES):
        valid = (delta >= 0) & (delta <= window // dil)
        bias = jnp.where(valid, -slope * (delta * dil).astype(F32) * LOG2_E, NEG_BIG)
        bias_ref[2 * t] = bias
        bias_ref[2 * t + 1] = jnp.where(col >= blk, bias, NEG_BIG)

    def pair_rows(x):
        return jnp.where(low, jnp.broadcast_to(x[:blk], (blk, LANES)),
                         jnp.broadcast_to(x[blk:], (blk, LANES)))

    def pair_halves(x):
        return jnp.where(low, x[:blk], x[blk:])

    zero_bf = jnp.zeros((), BF16)
    n_blocks = seq // blk

    def block_rows(i):
        start = i * blk
        return pl.ds(start if isinstance(start, int) else pl.multiple_of(start, blk), blk)

    (_, d_lo), (_, d_mid), (_, d_hi) = DILATED_BRANCHES
    ratio = d_hi // d_mid
    assert d_lo == 1 and d_hi == d_mid * ratio and seq % (blk * d_hi) == 0
    mid_len = seq // d_mid

    def log2(n):
        assert n & (n - 1) == 0
        return n.bit_length() - 1

    def split_block(idx, dil):
        nb = n_blocks // dil
        return idx >> log2(nb), idx & (nb - 1)

    def token_rows(idx, dil):
        r, n = split_block(idx, dil)
        return pl.ds(r + n * (blk * dil), blk, stride=dil) if dil > 1 else block_rows(idx)

    def hi_rows_in_mid(idx):
        r, n = split_block(idx, d_hi)
        return pl.ds((r & (d_mid - 1)) * mid_len + n * (blk * ratio) + (r >> log2(d_mid)), blk, stride=ratio)

    def gather_mid(idx):
        rows = token_rows(idx, d_mid)
        for f_ref, mid_ref, d_ref in ((qf_ref, qm_ref, qd_ref), (kf_ref, km_ref, kd_ref), (vf_ref, vm_ref, vd_ref)):
            x = f_ref[rows, :]
            mid_ref[block_rows(idx), :] = x
            d_ref[block_rows(idx), :] = x.astype(BF16)

    def gather_hi(idx):
        rows = hi_rows_in_mid(idx)
        for mid_ref, h_ref in ((qm_ref, qh_ref), (km_ref, kh_ref), (vm_ref, vh_ref)):
            h_ref[block_rows(idx), :] = mid_ref[rows, :].astype(BF16)

    def block_softmax(t, idx, q_src, k_src, v_src):
        first = split_block(idx, DILATED_BRANCHES[t][1])[1] == 0
        cur = block_rows(idx)
        prev = block_rows(idx if first else idx - 1)
        q = q_src[cur, :]
        q2 = jnp.concatenate([jnp.where(low, q, zero_bf), jnp.where(low, zero_bf, q)], axis=0)
        kk = jnp.concatenate([k_src[prev, :], k_src[cur, :]], axis=0)
        vv = jnp.concatenate([jnp.concatenate([v_src[prev, :], v_src[cur, :]], axis=0), ones_bf], axis=1)
        s = _dot_nt(q2, kk) + bias_ref[2 * t + int(first)]
        m = jnp.max(s, axis=-1, keepdims=True)
        p = jnp.exp2(s - m)
        pv = _dot(p.astype(BF16), vv)
        return pair_rows(m), pair_halves(pv[:, LANES:]), pair_halves(pv[:, :LANES])

    def combine(a, b):
        (ma, la, acca), (mb, lb, accb) = a, b
        m = jnp.maximum(ma, mb)
        wa, wb = jnp.exp2(ma - m), jnp.exp2(mb - m)
        return m, wa * la + wb * lb, wa * acca + wb * accb

    def load_state(st_ref, rows):
        return tuple(st_ref[i, rows, :] for i in range(3))

    def store_state(st_ref, rows, state):
        for i, x in enumerate(state):
            st_ref[i, rows, :] = x

    def branch0(idx):
        store_state(st0_ref, block_rows(idx), block_softmax(0, idx, q_ref, k_ref, v_ref))
        gather_mid(idx)

    def branch1(idx):
        mine = block_softmax(1, idx, qd_ref, kd_ref, vd_ref)
        store_state(st1_ref, block_rows(idx), combine(load_state(st0_ref, token_rows(idx, d_mid)), mine))
        gather_hi(idx)

    def branch2(idx):
        mine = block_softmax(2, idx, qh_ref, kh_ref, vh_ref)
        _, l_all, acc_all = combine(load_state(st1_ref, hi_rows_in_mid(idx)), mine)
        of_ref[token_rows(idx, d_hi), :] = acc_all / l_all

    for branch in (branch0, branch1, branch2):
        for idx in range(n_blocks):
            branch(idx)
    o_ref[...] = of_ref[...].astype(o_ref.dtype)


def _attention(slopes, main3, w1_all, w2_all, layer):
    b, seq, _ = main3.shape
    n_pairs = ATTN_WIDTH // LANES
    blk = ATTN_BLOCK
    n_steps = b * n_pairs
    w1_shape, w2_shape = w1_all.shape[1:], w2_all.shape[1:]
    r1, r2 = w1_shape[0] // n_steps, w2_shape[0] // n_steps
    assert r1 * n_steps == w1_shape[0] and r2 * n_steps == w2_shape[0]

    def spec(offset):
        return pl.BlockSpec((None, seq, LANES), lambda i, j: (i, 0, offset + j))

    return pl.pallas_call(
        _attn_kernel,
        grid=(b, n_pairs),
        in_specs=[
            pl.BlockSpec(memory_space=pltpu.SMEM),
            spec(0), spec(n_pairs), spec(2 * n_pairs),
            pl.BlockSpec((None, r1, w1_shape[1]), lambda i, j: (layer, i * n_pairs + j, 0)),
            pl.BlockSpec((None, r2, w2_shape[1]), lambda i, j: (layer, i * n_pairs + j, 0)),
        ],
        out_specs=[
            pl.BlockSpec((None, seq, LANES), lambda i, j: (i, 0, j)),
            pl.BlockSpec((r1, w1_shape[1]), lambda i, j: (i * n_pairs + j, 0)),
            pl.BlockSpec((r2, w2_shape[1]), lambda i, j: (i * n_pairs + j, 0)),
        ],
        out_shape=[
            jax.ShapeDtypeStruct((b, seq, ATTN_WIDTH), BF16),
            jax.ShapeDtypeStruct(w1_shape, BF16),
            jax.ShapeDtypeStruct(w2_shape, BF16),
        ],
        scratch_shapes=[pltpu.VMEM((seq, LANES), F32)] * 7
        + [pltpu.VMEM((seq, LANES), BF16)] * 6
        + [pltpu.VMEM((3, seq, LANES), F32)] * 2
        + [pltpu.VMEM((2 * len(DILATED_BRANCHES), 2 * blk, 2 * blk), F32)],
        compiler_params=_params(("parallel", "parallel")),
        name="attention",
    )(slopes, main3, main3, main3, w1_all, w2_all)


def _ssd_kernel(z_ref, xs_ref, bc_ref, dt_ref, dtb_ref, alog_ref, dskip_ref, g_ref,
                y_ref, state_ref):
    tb = xs_ref.shape[0]
    q = SSD_CHUNK
    gw = SSD_WIDTH // SSD_GROUPS
    pairs_per_group = gw // LANES

    @pl.when(pl.program_id(1) == 0)
    def _():
        state_ref[...] = jnp.zeros(state_ref.shape, F32)

    low = lax.broadcasted_iota(jnp.int32, (q, LANES), 1) < SSD_HEAD_DIM
    ri = lax.broadcasted_iota(jnp.int32, (q, q), 0)
    ci = lax.broadcasted_iota(jnp.int32, (q, q), 1)
    tri = ri >= ci
    tri_bf = jnp.where(tri, 1.0, 0.0).astype(BF16)
    a_neg = -jnp.exp(alog_ref[...])

    def cumsum_rows(v):
        hi = v.astype(BF16)
        r1 = v - hi.astype(F32)
        mid = r1.astype(BF16)
        lo = (r1 - mid.astype(F32)).astype(BF16)
        return _dot(tri_bf, hi) + _dot(tri_bf, mid) + _dot(tri_bf, lo)

    def lane_bcast(v, h):
        return jnp.broadcast_to(v[:, h:h + 1], (q, LANES))

    def chunk(c):
        r0 = c * q
        u = jnp.concatenate([xs_ref[r0:r0 + q, :], bc_ref[r0:r0 + q, :]], axis=1).astype(F32)
        x = u[:, :SSD_WIDTH]
        dt_in = dt_ref[pl.ds(r0, q), :] + dtb_ref[...]
        dt = jnp.maximum(dt_in, 0.0) + jnp.log(1.0 + jnp.exp(-jnp.abs(dt_in)))
        acum = cumsum_rows(dt * a_neg) * LOG2_E
        acum_t = acum.T

        y_parts = []
        for g in range(SSD_GROUPS):
            bg = u[:, SSD_WIDTH + g * SSD_STATE:SSD_WIDTH + (g + 1) * SSD_STATE]
            cg = u[:, SSD_WIDTH + (SSD_GROUPS + g) * SSD_STATE:SSD_WIDTH + (SSD_GROUPS + g + 1) * SSD_STATE]
            cg_bf = cg.astype(BF16)
            cb = _dot_nt(cg_bf, bg.astype(BF16))
            st = state_ref[g]
            y_off = _dot(cg_bf, st.astype(BF16))
            xd_parts, decay_parts = [], []
            for pp in range(pairs_per_group):
                p = g * pairs_per_group + pp
                h0 = HEAD_PAIR * p
                xp = x[:, p * LANES:(p + 1) * LANES]
                cols = [lane_bcast(acum, h0 + e) for e in range(HEAD_PAIR)]
                colc = jnp.where(low, cols[0], cols[1])
                ms = []
                for e in range(HEAD_PAIR):
                    h = h0 + e
                    seg = jnp.exp2(jnp.where(tri, cols[e] - acum_t[h:h + 1, :], NEG_BIG))
                    ms.append(cb * seg)
                m_cat = jnp.concatenate(ms, axis=1).astype(BF16)
                dtc = jnp.where(low, lane_bcast(dt, h0), lane_bcast(dt, h0 + 1))
                xdt = xp * dtc
                x_blk = jnp.concatenate([jnp.where(low, xdt, 0.0), jnp.where(low, 0.0, xdt)],
                                        axis=0).astype(BF16)
                y_diag = _dot(m_cat, x_blk)
                y_o = y_off[:, pp * LANES:(pp + 1) * LANES] * jnp.exp2(colc)
                y_parts.append(y_diag + y_o + dskip_ref[:, p * LANES:(p + 1) * LANES] * xp)
                last = colc[q - 1:q, :]
                xd_parts.append(xdt * jnp.exp2(last - colc))
                decay_parts.append(jnp.exp2(last))
            xd = jnp.concatenate(xd_parts, axis=1).astype(BF16)
            decay = jnp.concatenate(decay_parts, axis=1)
            state_ref[g] = st * decay + _dot(bg.T.astype(BF16), xd)

        z = z_ref[pl.ds(r0, q), :].astype(F32)
        outs = []
        for g in range(SSD_GROUPS):
            yg = jnp.concatenate(y_parts[g * pairs_per_group:(g + 1) * pairs_per_group], axis=1)
            yg = yg * _silu(z[:, g * gw:(g + 1) * gw])
            outs.append(yg * _rms_scale(yg) * g_ref[:, g * gw:(g + 1) * gw])
        y_ref[pl.ds(r0, q), :] = jnp.concatenate(outs, axis=1).astype(y_ref.dtype)

    for c in range(tb // q):
        chunk(c)


def _ssd(main3, dt3, dt_bias, a_log, d_skip, norm_g, *, tb):
    b, seq, _ = main3.shape
    z_blk = (3 * ATTN_WIDTH) // SSD_WIDTH
    xs_blk = (3 * ATTN_WIDTH + SSD_WIDTH) // SSD_WIDTH
    bc_blk = (3 * ATTN_WIDTH + 2 * SSD_WIDTH) // BC_WIDTH

    def const(shape):
        return pl.BlockSpec(shape, lambda i, j: (0,) * len(shape))

    return pl.pallas_call(
        _ssd_kernel,
        grid=(b, seq // tb),
        in_specs=[
            pl.BlockSpec((None, tb, SSD_WIDTH), lambda i, j: (i, j, z_blk)),
            pl.BlockSpec((None, tb, SSD_WIDTH), lambda i, j: (i, j, xs_blk)),
            pl.BlockSpec((None, tb, BC_WIDTH), lambda i, j: (i, j, bc_blk)),
            pl.BlockSpec((None, tb, LANES), lambda i, j: (i, j, 0)),
            const((1, LANES)), const((1, LANES)),
            const((1, SSD_WIDTH)), const((1, SSD_WIDTH)),
        ],
        out_specs=pl.BlockSpec((None, tb, SSD_WIDTH), lambda i, j: (i, j, 0)),
        out_shape=jax.ShapeDtypeStruct((b, seq, SSD_WIDTH), BF16),
        scratch_shapes=[
            pltpu.VMEM((SSD_GROUPS, SSD_STATE, SSD_WIDTH // SSD_GROUPS), F32),
        ],
        compiler_params=_params(("parallel", "arbitrary")),
        name="ssd",
    )(main3, main3, main3, dt3, dt_bias, a_log, d_skip, norm_g)


def _out_proj_kernel(a_ref, y_ref, x_ref, g_ref, w_ref, o_ref):
    a = a_ref[...].astype(F32)
    an = (a * _rms_scale(a) * g_ref[...]).astype(BF16)
    mixed = _dot(an, w_ref[0:ATTN_WIDTH, :]) + _dot(y_ref[...], w_ref[ATTN_WIDTH:, :])
    o_ref[...] = x_ref[...] + mixed


def _out_proj(attn, y, x, g, w_all, layer, *, tm):
    m, d = x.shape
    return pl.pallas_call(
        _out_proj_kernel,
        grid=(m // tm,),
        in_specs=[
            pl.BlockSpec((tm, ATTN_WIDTH), lambda i: (i, 0)),
            pl.BlockSpec((tm, SSD_WIDTH), lambda i: (i, 0)),
            pl.BlockSpec((tm, d), lambda i: (i, 0)),
            _resident((1, ATTN_WIDTH)),
            _resident(w_all.shape[1:], layer),
        ],
        out_specs=pl.BlockSpec((tm, d), lambda i: (i, 0)),
        out_shape=jax.ShapeDtypeStruct((m, d), F32),
        compiler_params=_params(("parallel",)),
        name="out_proj",
    )(attn, y, x, g, w_all)


def _mlp_kernel(x_ref, g_ref, w1_ref, w2_ref, gf_ref, o_ref, hn_ref, *, final_norm):
    f = pl.program_id(1)

    @pl.when(f == 0)
    def _():
        x = x_ref[...]
        hn_ref[...] = (x * _rms_scale(x) * g_ref[...]).astype(BF16)
        o_ref[...] = x

    h = jnp.maximum(_dot(hn_ref[...], w1_ref[...]), 0.0)
    o_ref[...] += _dot((h * h).astype(BF16), w2_ref[...])

    if final_norm:
        @pl.when(f == pl.num_programs(1) - 1)
        def _():
            o = o_ref[...]
            o_ref[...] = o * _rms_scale(o) * gf_ref[...]


def _mlp(x, g, w1, w2, gf, *, tm, tf, final_norm):
    m, d = x.shape
    dff = w1.shape[1]
    return pl.pallas_call(
        functools.partial(_mlp_kernel, final_norm=final_norm),
        grid=(m // tm, dff // tf),
        in_specs=[
            pl.BlockSpec((tm, d), lambda i, j: (i, 0)),
            pl.BlockSpec((1, d), lambda i, j: (0, 0)),
            pl.BlockSpec((d, tf), lambda i, j: (0, j)),
            pl.BlockSpec((tf, d), lambda i, j: (j, 0)),
            pl.BlockSpec((1, d), lambda i, j: (0, 0)),
        ],
        out_specs=pl.BlockSpec((tm, d), lambda i, j: (i, 0)),
        out_shape=jax.ShapeDtypeStruct((m, d), F32),
        scratch_shapes=[pltpu.VMEM((tm, d), BF16)],
        compiler_params=_params(("parallel", "arbitrary")),
        name="mlp",
    )(x, g, w1, w2, gf)


def _alibi_slopes():
    return jnp.asarray(2.0 ** (-8.0 * (np.arange(ATTN_HEADS) + 1) / ATTN_HEADS), dtype=F32)


def _pad_lanes(v):
    return jnp.pad(v, ((0, 0), (0, LANES - v.shape[1])))


def kernel(x, ln1_g, w_in, conv_w, conv_b, dt_bias, a_log, d_skip, attn_norm_g, ssd_norm_g, w_out,
           ln2_g, w_mlp_in, w_mlp_out, final_norm_g):
    b, seq, d = x.shape
    m = b * seq
    slopes = _alibi_slopes()
    xf = x.reshape(m, d)
    w_in_bf = w_in.astype(BF16)
    w_dt_bf = jnp.pad(w_in[:, :, MAIN_WIDTH:], ((0, 0), (0, 0), (0, LANES - (w_in.shape[2] - MAIN_WIDTH)))).astype(BF16)
    w_out_bf = w_out.astype(BF16)
    for l in range(DEPTH):
        main, dt_raw = _in_proj(xf, ln1_g[l][None, :], w_in_bf, w_dt_bf, l, conv_w[l], conv_b[l][None, :],
                                seq=seq, tm=IN_PROJ_ROWS, tn=IN_PROJ_COLS)
        main3 = main.reshape(b, seq, MAIN_WIDTH)
        attn, w1_bf, w2_bf = _attention(slopes, main3, w_mlp_in, w_mlp_out, l)
        y = _ssd(
            main3, dt_raw.reshape(b, seq, LANES),
            _pad_lanes(dt_bias[l][None, :]), _pad_lanes(a_log[l][None, :]),
            jnp.repeat(d_skip[l], SSD_HEAD_DIM)[None, :], ssd_norm_g[l][None, :], tb=SSD_TOKENS)
        x1 = _out_proj(attn.reshape(m, ATTN_WIDTH), y.reshape(m, SSD_WIDTH), xf,
                       attn_norm_g[l][None, :], w_out_bf, l, tm=OUT_PROJ_ROWS)
        xf = _mlp(x1, ln2_g[l][None, :], w1_bf, w2_bf,
                  final_norm_g[None, :], tm=MLP_ROWS, tf=MLP_HIDDEN, final_norm=(l == DEPTH - 1))
    return xf.reshape(b, seq, d)
```
